```python
import jax, jax.numpy as jnp
from jax import lax
import numpy as np

D_MODEL = 2048
BATCH = 4
SEQ = 2048
DEPTH = 1

HG_HEADS = 8
HG_HEAD_DIM = 128
HG_WIDTH = HG_HEADS * HG_HEAD_DIM
ML_HEADS = 4
ML_QK_DIM = 128
ML_V_DIM = 256
ML_QK_WIDTH = ML_HEADS * ML_QK_DIM
ML_V_WIDTH = ML_HEADS * ML_V_DIM
ML_CONV = 4
CHUNK = 64
D_FF = 5632
FFN_CONV = 3
EPS = 1e-6
SPLIT_SIZES = (HG_WIDTH, HG_WIDTH, HG_WIDTH, HG_WIDTH, 2 * ML_QK_WIDTH, ML_V_WIDTH, ML_V_WIDTH, ML_HEADS, ML_HEADS, D_MODEL, D_MODEL)
IN_WIDTH = sum(SPLIT_SIZES)
FG_OFFSET = sum(SPLIT_SIZES[:8])

kernel_name = "hybrid_hgrn2_mlstm_convffn_block"


def rms_norm(u, gain):
    u32 = u.astype(jnp.float32)
    y = u32 * lax.rsqrt(jnp.mean(u32 * u32, axis=-1, keepdims=True) + EPS)
    return (y * gain.astype(jnp.float32)).astype(u.dtype)


def head_rms_norm(u, n_heads, gain):
    B, T, W = u.shape
    uh = u.reshape(B, T, n_heads, W // n_heads)
    uh = uh * lax.rsqrt(jnp.mean(uh * uh, axis=-1, keepdims=True) + EPS)
    return uh.reshape(B, T, W) * gain.astype(jnp.float32)


def causal_dwconv(u, w, bias):
    K = w.shape[0]
    T = u.shape[1]
    up = jnp.pad(u, ((0, 0), (K - 1, 0), (0, 0)))
    out = bias
    for j in range(K):
        out = out + w[j] * up[:, j:j + T]
    return out


def to_chunks(t, n_heads):
    B, T, W = t.shape
    return t.reshape(B, T // CHUNK, CHUNK, n_heads, W // n_heads).transpose(0, 3, 1, 2, 4)


def from_chunks(t):
    B, H, N, C, d = t.shape
    return t.transpose(0, 2, 3, 1, 4).reshape(B, N * C, H * d)


def gate_chunks(g):
    B, T, H = g.shape
    return g.reshape(B, T // CHUNK, CHUNK, H).transpose(0, 3, 1, 2)


def hgrn2_mix(q_pre, f_pre, i_in, g_pre, lb, gain):
    B = q_pre.shape[0]
    q = jax.nn.silu(q_pre) * (HG_HEAD_DIM ** -0.5)
    f = lb + (1.0 - lb) * jax.nn.sigmoid(f_pre)
    k = 1.0 - f
    q, k, v, log_f = to_chunks(q, HG_HEADS), to_chunks(k, HG_HEADS), to_chunks(i_in, HG_HEADS), to_chunks(jnp.log(f), HG_HEADS)
    b = jnp.cumsum(log_f, axis=3)
    b_last = b[:, :, :, -1]
    q_dec = q * jnp.exp(b)
    k_inv = k * jnp.exp(-b)
    k_end = k * jnp.exp(b_last[:, :, :, None] - b)
    causal = jnp.tril(jnp.ones((CHUNK, CHUNK), dtype=bool))
    attn = jnp.where(causal, jnp.einsum("bhntd,bhnsd->bhnts", q_dec, k_inv), 0.0)
    o_intra = jnp.einsum("bhnts,bhnse->bhnte", attn, v)

    def step(S, inp):
        decay, kc, vc = inp
        S_new = decay[..., None] * S + jnp.einsum("bhcd,bhce->bhde", kc, vc)
        return S_new, S

    S0 = jnp.zeros((B, HG_HEADS, HG_HEAD_DIM, HG_HEAD_DIM), jnp.float32)
    xs = (jnp.moveaxis(jnp.exp(b_last), 2, 0), jnp.moveaxis(k_end, 2, 0), jnp.moveaxis(v, 2, 0))
    _, S_prev = lax.scan(step, S0, xs)
    o_inter = jnp.einsum("bhntd,nbhde->bhnte", q_dec, S_prev)
    o = from_chunks(o_intra + o_inter)
    return head_rms_norm(o, HG_HEADS, gain) * jax.nn.silu(g_pre)


def mlstm_mix(q_in, k_in, v_in, o_pre, ig_pre, fg_pre, gain):
    B = q_in.shape[0]
    q = to_chunks(q_in, ML_HEADS)
    k = to_chunks(k_in, ML_HEADS) * (ML_QK_DIM ** -0.5)
    v = to_chunks(v_in, ML_HEADS)
    ig = gate_chunks(ig_pre)
    log_f = jax.nn.log_sigmoid(gate_chunks(fg_pre))
    b = jnp.cumsum(log_f, axis=-1)
    b_last = b[..., -1]
    logw = b_last[..., None] - b + ig

    def step(carry, inp):
        C, n, m = carry
        bl, lw, kc, vc = inp
        m_new = jnp.maximum(bl + m, jnp.max(lw, axis=-1))
        decay = jnp.exp(bl + m - m_new)
        w = jnp.exp(lw - m_new[..., None])
        C_new = decay[..., None, None] * C + jnp.einsum("bhc,bhcd,bhce->bhde", w, kc, vc)
        n_new = decay[..., None] * n + jnp.einsum("bhc,bhcd->bhd", w, kc)
        return (C_new, n_new, m_new), (C, n, m)

    init = (jnp.zeros((B, ML_HEADS, ML_QK_DIM, ML_V_DIM), jnp.float32),
            jnp.zeros((B, ML_HEADS, ML_QK_DIM), jnp.float32),
            jnp.zeros((B, ML_HEADS), jnp.float32))
    xs = (jnp.moveaxis(b_last, 2, 0), jnp.moveaxis(logw, 2, 0), jnp.moveaxis(k, 2, 0), jnp.moveaxis(v, 2, 0))
    _, (C_prev, n_prev, m_prev) = lax.scan(step, init, xs)
    m_prev = jnp.moveaxis(m_prev, 0, 2)

    causal = jnp.tril(jnp.ones((CHUNK, CHUNK), dtype=bool))
    log_d = jnp.where(causal, b[..., :, None] - b[..., None, :] + ig[..., None, :], -jnp.inf)
    m_inter = b + m_prev[..., None]
    m = jnp.maximum(m_inter, jnp.max(log_d, axis=-1))
    scores = jnp.einsum("bhntd,bhnsd->bhnts", q, k) * jnp.exp(log_d - m[..., None])
    inter_scale = jnp.exp(m_inter - m)
    num = jnp.einsum("bhnts,bhnse->bhnte", scores, v) + inter_scale[..., None] * jnp.einsum("bhntd,nbhde->bhnte", q, C_prev)
    den = jnp.sum(scores, axis=-1) + inter_scale * jnp.einsum("bhntd,nbhd->bhnt", q, n_prev)
    h = num / jnp.maximum(jnp.abs(den), jnp.exp(-m))[..., None]
    h = head_rms_norm(from_chunks(h), ML_HEADS, gain)
    return jax.nn.sigmoid(o_pre) * h


def setup_inputs(seed: int = 0) -> dict:
    key = jax.random.key(seed)
    ks = jax.random.split(key, 24)
    f32 = jnp.float32

    def nrm(k, shape, scale):
        return jax.random.normal(k, shape, f32) * scale

    def gains(k, shape):
        return 1.0 + nrm(k, shape, 0.05)

    b_in = nrm(ks[6], (DEPTH, IN_WIDTH), 0.02)
    b_in = b_in.at[:, FG_OFFSET:FG_OFFSET + ML_HEADS].add(jnp.linspace(3.0, 6.0, ML_HEADS, dtype=f32))
    return {
        "x": nrm(ks[0], (BATCH, SEQ, D_MODEL), 1.0),
        "norm_mix_pre": gains(ks[1], (DEPTH, D_MODEL)),
        "norm_mix_post": gains(ks[2], (DEPTH, D_MODEL)),
        "norm_ffn_pre": gains(ks[3], (DEPTH, D_MODEL)),
        "norm_ffn_post": gains(ks[4], (DEPTH, D_MODEL)),
        "w_in": nrm(ks[5], (DEPTH, D_MODEL, IN_WIDTH), D_MODEL ** -0.5),
        "b_in": b_in,
        "hg_lb_logits": nrm(ks[7], (DEPTH + 1, HG_WIDTH), 0.1),
        "hg_norm": gains(ks[8], (DEPTH, HG_WIDTH)),
        "ml_conv_w": nrm(ks[9], (DEPTH, ML_CONV, 2 * ML_QK_WIDTH), ML_CONV ** -0.5),
        "ml_conv_b": nrm(ks[10], (DEPTH, 2 * ML_QK_WIDTH), 0.02),
        "ml_norm": gains(ks[11], (DEPTH, ML_V_WIDTH)),
        "w_up_a": nrm(ks[12], (DEPTH, HG_WIDTH, D_MODEL), HG_WIDTH ** -0.5),
        "w_up_b": nrm(ks[13], (DEPTH, ML_V_WIDTH, D_MODEL), ML_V_WIDTH ** -0.5),
        "w_out": nrm(ks[14], (DEPTH, D_MODEL, D_MODEL), D_MODEL ** -0.5),
        "ffn_w_gate": nrm(ks[15], (DEPTH, D_MODEL, D_FF), D_MODEL ** -0.5),
        "ffn_w_up": nrm(ks[16], (DEPTH, D_MODEL, D_FF), D_MODEL ** -0.5),
        "ffn_conv_w": nrm(ks[17], (DEPTH, FFN_CONV, D_FF), FFN_CONV ** -0.5),
        "ffn_conv_b": nrm(ks[18], (DEPTH, D_FF), 0.02),
        "ffn_w_down": nrm(ks[19], (DEPTH, D_FF, D_MODEL), D_FF ** -0.5),
    }


def reference(x, norm_mix_pre, norm_mix_post, norm_ffn_pre, norm_ffn_post, w_in, b_in, hg_lb_logits,
              hg_norm, ml_conv_w, ml_conv_b, ml_norm, w_up_a, w_up_b, w_out, ffn_w_gate, ffn_w_up,
              ffn_conv_w, ffn_conv_b, ffn_w_down):
    f32 = jnp.float32
    split_points = [int(p) for p in np.cumsum(SPLIT_SIZES)[:-1]]
    lb_all = jnp.cumsum(jax.nn.softmax(hg_lb_logits.astype(f32), axis=0), axis=0)
    for l in range(DEPTH):
        h = rms_norm(x, norm_mix_pre[l])
        proj = jnp.einsum("btd,dp->btp", h, w_in[l]) + b_in[l]
        (hg_q, hg_f, hg_i, hg_g, ml_qk, ml_v, ml_o, ml_ig, ml_fg, gate_a, gate_b) = jnp.split(proj, split_points, axis=-1)
        y_a = hgrn2_mix(hg_q.astype(f32), hg_f.astype(f32), hg_i.astype(f32), hg_g.astype(f32), lb_all[l], hg_norm[l])
        ml_qk = jax.nn.silu(causal_dwconv(ml_qk, ml_conv_w[l], ml_conv_b[l]))
        ml_q, ml_k = jnp.split(ml_qk, 2, axis=-1)
        y_b = mlstm_mix(ml_q.astype(f32), ml_k.astype(f32), ml_v.astype(f32), ml_o.astype(f32),
                        ml_ig.astype(f32), ml_fg.astype(f32), ml_norm[l])
        branch_a = jnp.einsum("btc,cd->btd", y_a.astype(x.dtype), w_up_a[l])
        branch_b = jnp.einsum("btc,cd->btd", y_b.astype(x.dtype), w_up_b[l])
        merged = jax.nn.sigmoid(gate_a) * branch_a + jax.nn.sigmoid(gate_b) * branch_b
        mix_out = jnp.einsum("btd,de->bte", merged, w_out[l])
        x = x + rms_norm(mix_out, norm_mix_post[l])
        h = rms_norm(x, norm_ffn_pre[l])
        gate = causal_dwconv(jnp.einsum("btd,df->btf", h, ffn_w_gate[l]), ffn_conv_w[l], ffn_conv_b[l])
        up = jnp.einsum("btd,df->btf", h, ffn_w_up[l])
        ffn_out = jnp.einsum("btf,fd->btd", jax.nn.gelu(gate, approximate=True) * up, ffn_w_down[l])
        x = x + rms_norm(ffn_out, norm_ffn_post[l])
    return x
```

```python
import functools

import jax
import jax.numpy as jnp
from jax import lax
from jax.experimental import pallas as pl
from jax.experimental.pallas import tpu as pltpu

HG_HEADS = 8
HG_HEAD_DIM = 128
ML_HEADS = 4
ML_QK_DIM = 128
ML_V_DIM = 256
CHUNK = 64
EPS = 1e-6

F32 = jnp.float32
BF16 = jnp.bfloat16
HIGHEST = lax.Precision.HIGHEST

VMEM_LIMIT_BYTES = 56 * 1024 * 1024
SUBLANES = 8


def _params(*sem):
    return pltpu.CompilerParams(dimension_semantics=sem, vmem_limit_bytes=VMEM_LIMIT_BYTES)


def _dot(a, b):
    return jnp.dot(a, b, preferred_element_type=F32)


def _dot_nt(a, b):
    return lax.dot_general(a, b, (((1,), (1,)), ((), ())), preferred_element_type=F32)


def _dot_tn(a, b):
    return lax.dot_general(a, b, (((0,), (0,)), ((), ())), preferred_element_type=F32)


def _rms(u, gain):
    return u * lax.rsqrt(jnp.mean(u * u, axis=-1, keepdims=True) + EPS) * gain


def _rmsnorm_kernel(x_ref, g_ref, o_ref):
    o_ref[...] = _rms(x_ref[...], g_ref[...]).astype(o_ref.dtype)


def _rmsnorm_cast(x2, gain, tm=512):
    m, d = x2.shape
    return pl.pallas_call(
        _rmsnorm_kernel,
        grid=(m // tm,),
        in_specs=[pl.BlockSpec((tm, d), lambda i: (i, 0)),
                  pl.BlockSpec((1, d), lambda i: (0, 0))],
        out_specs=pl.BlockSpec((tm, d), lambda i: (i, 0)),
        out_shape=jax.ShapeDtypeStruct((m, d), BF16),
        compiler_params=_params("parallel"),
        name="rmsnorm_cast",
    )(x2, gain.reshape(1, d))


def _mm_bias_kernel(h_ref, w_ref, b_ref, o_ref, *, sigmoid):
    acc = _dot(h_ref[...], w_ref[...]) + b_ref[...]
    if sigmoid:
        acc = jax.nn.sigmoid(acc)
    o_ref[...] = acc.astype(o_ref.dtype)


def _mm_bias(h, w, b, *, tm, tn, out_dtype, sigmoid=False, name):
    m, k = h.shape
    n = w.shape[1]
    return pl.pallas_call(
        functools.partial(_mm_bias_kernel, sigmoid=sigmoid),
        grid=(n // tn, m // tm),
        in_specs=[pl.BlockSpec((tm, k), lambda j, i: (i, 0)),
                  pl.BlockSpec((k, tn), lambda j, i: (0, j)),
                  pl.BlockSpec((1, tn), lambda j, i: (0, j))],
        out_specs=pl.BlockSpec((tm, tn), lambda j, i: (i, j)),
        out_shape=jax.ShapeDtypeStruct((m, n), out_dtype),
        compiler_params=_params("parallel", "parallel"),
        name=name,
    )(h, w, b.reshape(1, n))


def _tril(n):
    r = lax.broadcasted_iota(jnp.int32, (n, n), 0)
    c = lax.broadcasted_iota(jnp.int32, (n, n), 1)
    return r >= c


def _hgrn2_kernel(q_ref, f_ref, i_ref, g_ref, lbl_ref, gain_ref, o_ref, st_ref, *, layer, n_chunks):
    @pl.when(pl.program_id(2) == 0)
    def _():
        st_ref[...] = jnp.zeros_like(st_ref)

    logits = lbl_ref[...]
    e = jnp.exp(logits - jnp.max(logits, axis=0, keepdims=True))
    sm = e / jnp.sum(e, axis=0, keepdims=True)
    lb = jnp.sum(sm[:layer + 1], axis=0, keepdims=True)

    q = jax.nn.silu(q_ref[...]) * (HG_HEAD_DIM ** -0.5)
    f = lb + (1.0 - lb) * jax.nn.sigmoid(f_ref[...])
    k = 1.0 - f
    logf = jnp.log(f)
    v = i_ref[...].astype(BF16)
    causal = _tril(CHUNK)
    tril_f = causal.astype(F32)

    outs = []
    st = st_ref[...]
    for c in range(n_chunks):
        sl = slice(c * CHUNK, (c + 1) * CHUNK)
        b = jnp.dot(tril_f, logf[sl], precision=HIGHEST, preferred_element_type=F32)
        b_last = b[CHUNK - 1:CHUNK, :]
        q_dec = (q[sl] * jnp.exp(b)).astype(BF16)
        k_inv = (k[sl] * jnp.exp(-b)).astype(BF16)
        k_end = (k[sl] * jnp.exp(b_last - b)).astype(BF16)
        attn = jnp.where(causal, _dot_nt(q_dec, k_inv), 0.0)
        o = _dot(attn.astype(BF16), v[sl]) + _dot_nt(q_dec, st.astype(BF16))
        st = jnp.exp(b_last) * st + _dot_tn(v[sl], k_end)
        outs.append(o)
    st_ref[...] = st
    o = jnp.concatenate(outs, axis=0) if n_chunks > 1 else outs[0]
    o_ref[...] = (_rms(o, gain_ref[...]) * jax.nn.silu(g_ref[...])).astype(o_ref.dtype)


def _hgrn2(p_main, lb_logits, gain, *, batch, seq, layer, tt=512):
    m = p_main.shape[0]
    nt = seq // tt
    w = HG_HEADS * HG_HEAD_DIM
    dk = HG_HEAD_DIM

    def col(blk):
        return pl.BlockSpec((tt, dk), lambda b, h, t: (b * nt + t, blk * HG_HEADS + h))

    return pl.pallas_call(
        functools.partial(_hgrn2_kernel, layer=layer, n_chunks=tt // CHUNK),
        grid=(batch, HG_HEADS, nt),
        in_specs=[col(0), col(1), col(2), col(3),
                  pl.BlockSpec((lb_logits.shape[0], dk), lambda b, h, t: (0, h)),
                  pl.BlockSpec((1, dk), lambda b, h, t: (0, h))],
        out_specs=pl.BlockSpec((tt, dk), lambda b, h, t: (b * nt + t, h)),
        out_shape=jax.ShapeDtypeStruct((m, w), BF16),
        scratch_shapes=[pltpu.VMEM((dk, dk), F32)],
        compiler_params=_params("parallel", "parallel", "arbitrary"),
        name="hgrn2",
    )(p_main, p_main, p_main, p_main, lb_logits, gain.reshape(1, w))


def _causal_conv_silu(buf_ref, x_ref, w_ref, b_ref, tt, first):
    kw = w_ref.shape[0]

    @pl.when(first)
    def _():
        buf_ref[0:SUBLANES, :] = jnp.zeros((SUBLANES, buf_ref.shape[1]), F32)

    buf_ref[SUBLANES:SUBLANES + tt, :] = x_ref[...]
    acc = b_ref[...] + w_ref[kw - 1:kw, :] * x_ref[...]
    for j in range(kw - 1):
        s = kw - 1 - j
        acc = acc + w_ref[j:j + 1, :] * buf_ref[SUBLANES - s:SUBLANES - s + tt, :]
    buf_ref[0:SUBLANES, :] = buf_ref[tt:tt + SUBLANES, :]
    return jax.nn.silu(acc)


def _mlstm_kernel(q_ref, k_ref, v_ref, o_ref, gates_ref, wq_ref, wk_ref, bq_ref, bk_ref, gain_ref,
                  y_ref, ct_ref, m_ref, qbuf_ref, kbuf_ref, *, n_chunks, tt):
    h = pl.program_id(1)
    first = pl.program_id(2) == 0

    @pl.when(first)
    def _():
        ct_ref[...] = jnp.zeros_like(ct_ref)
        m_ref[...] = jnp.zeros_like(m_ref)

    q = _causal_conv_silu(qbuf_ref, q_ref, wq_ref, bq_ref, tt, first)
    k = _causal_conv_silu(kbuf_ref, k_ref, wk_ref, bk_ref, tt, first) * (ML_QK_DIM ** -0.5)
    qb = q.astype(BF16)
    v = v_ref[...]
    v_aug = jnp.concatenate([v, jnp.ones((tt, ML_QK_DIM), F32)], axis=1).astype(BF16)

    gates = gates_ref[...]
    lane = lax.broadcasted_iota(jnp.int32, gates.shape, 1)
    ig_all = jnp.where(lane == h, gates, 0.0)
    logf_all = jnp.where(lane == ML_HEADS + h, jax.nn.log_sigmoid(gates), 0.0)
    causal = _tril(CHUNK)
    tril_f = causal.astype(F32)
    ones_rows = jnp.ones((CHUNK, gates.shape[1]), F32)

    ct = ct_ref[...]
    m_prev = m_ref[0:1, 0:1]
    outs = []
    for c in range(n_chunks):
        sl = slice(c * CHUNK, (c + 1) * CHUNK)
        b_all = jnp.dot(tril_f, logf_all[sl], precision=HIGHEST, preferred_element_type=F32)
        b = jnp.sum(b_all, axis=1, keepdims=True)
        ig = jnp.sum(ig_all[sl], axis=1, keepdims=True)
        b_last = b[CHUNK - 1:CHUNK, :]
        r = lax.dot_general(ones_rows, ig_all[sl] - b_all, (((1,), (1,)), ((), ())),
                            precision=HIGHEST, preferred_element_type=F32)
        log_d = jnp.where(causal, b + r, -jnp.inf)
        m_inter = b + m_prev
        m_tot = jnp.maximum(m_inter, jnp.max(log_d, axis=1, keepdims=True))
        scores = _dot_nt(qb[sl], k[sl].astype(BF16)) * jnp.exp(log_d - m_tot)
        inter_scale = jnp.exp(m_inter - m_tot)
        tot = _dot(scores.astype(BF16), v_aug[sl]) + inter_scale * _dot_nt(qb[sl], ct.astype(BF16))
        num = tot[:, :ML_V_DIM]
        den = tot[:, ML_V_DIM:ML_V_DIM + 1]
        outs.append(num / jnp.maximum(jnp.abs(den), jnp.exp(-m_tot)))
        logw = b_last - b + ig
        m_new = jnp.maximum(b_last + m_prev, jnp.max(logw, axis=0, keepdims=True))
        decay = jnp.exp(b_last + m_prev - m_new)
        kw = (k[sl] * jnp.exp(logw - m_new)).astype(BF16)
        ct = decay * ct + _dot_tn(v_aug[sl], kw)
        m_prev = m_new
    ct_ref[...] = ct
    m_ref[...] = jnp.broadcast_to(m_prev, m_ref.shape)
    hh = jnp.concatenate(outs, axis=0) if n_chunks > 1 else outs[0]
    y_ref[...] = (jax.nn.sigmoid(o_ref[...]) * _rms(hh, gain_ref[...])).astype(y_ref.dtype)


def _mlstm(p_main, gates, conv_w, conv_b, gain, *, batch, seq, tt=512):
    m = p_main.shape[0]
    nt = seq // tt
    dk, dv = ML_QK_DIM, ML_V_DIM
    qk_w = ML_HEADS * dk
    v_w = ML_HEADS * dv
    hg_w = HG_HEADS * HG_HEAD_DIM
    q_blk = 4 * hg_w // dk
    k_blk = q_blk + ML_HEADS
    v_blk = (4 * hg_w + 2 * qk_w) // dv
    o_blk = v_blk + ML_HEADS
    kw = conv_w.shape[0]

    def rows(b, h, t):
        return b * nt + t

    return pl.pallas_call(
        functools.partial(_mlstm_kernel, n_chunks=tt // CHUNK, tt=tt),
        grid=(batch, ML_HEADS, nt),
        in_specs=[pl.BlockSpec((tt, dk), lambda b, h, t: (rows(b, h, t), q_blk + h)),
                  pl.BlockSpec((tt, dk), lambda b, h, t: (rows(b, h, t), k_blk + h)),
                  pl.BlockSpec((tt, dv), lambda b, h, t: (rows(b, h, t), v_blk + h)),
                  pl.BlockSpec((tt, dv), lambda b, h, t: (rows(b, h, t), o_blk + h)),
                  pl.BlockSpec((tt, gates.shape[1]), lambda b, h, t: (rows(b, h, t), 0)),
                  pl.BlockSpec((kw, dk), lambda b, h, t: (0, h)),
                  pl.BlockSpec((kw, dk), lambda b, h, t: (0, ML_HEADS + h)),
                  pl.BlockSpec((1, dk), lambda b, h, t: (0, h)),
                  pl.BlockSpec((1, dk), lambda b, h, t: (0, ML_HEADS + h)),
                  pl.BlockSpec((1, dv), lambda b, h, t: (0, h))],
        out_specs=pl.BlockSpec((tt, dv), lambda b, h, t: (rows(b, h, t), h)),
        out_shape=jax.ShapeDtypeStruct((m, v_w), BF16),
        scratch_shapes=[pltpu.VMEM((dv + dk, dk), F32),
                        pltpu.VMEM((SUBLANES, 128), F32),
                        pltpu.VMEM((tt + SUBLANES, dk), F32),
                        pltpu.VMEM((tt + SUBLANES, dk), F32)],
        compiler_params=_params("parallel", "parallel", "arbitrary"),
        name="mlstm",
    )(p_main, p_main, p_main, p_main, gates, conv_w, conv_w,
      conv_b.reshape(1, 2 * qk_w), conv_b.reshape(1, 2 * qk_w), gain.reshape(1, v_w))


def _merge_kernel(ya_ref, yb_ref, ga_ref, gb_ref, x_ref, wua_ref, wub_ref, wout_ref, gpost_ref, gpre_ref,
                  x1_ref, h2_ref):
    merged = (ga_ref[...].astype(F32) * _dot(ya_ref[...], wua_ref[...])
              + gb_ref[...].astype(F32) * _dot(yb_ref[...], wub_ref[...]))
    mix = _dot(merged.astype(BF16), wout_ref[...])
    x1 = x_ref[...] + _rms(mix, gpost_ref[...])
    x1_ref[...] = x1
    h2_ref[...] = _rms(x1, gpre_ref[...]).astype(h2_ref.dtype)


def _merge(y_a, y_b, p_gate, x2, w_ua, w_ub, w_out, g_post, g_pre, tm=256):
    m, d = x2.shape
    wa = y_a.shape[1]
    wb = y_b.shape[1]
    const = lambda i: (0, 0)
    single = pl.Buffered(1)
    return pl.pallas_call(
        _merge_kernel,
        grid=(m // tm,),
        in_specs=[pl.BlockSpec((tm, wa), lambda i: (i, 0)),
                  pl.BlockSpec((tm, wb), lambda i: (i, 0)),
                  pl.BlockSpec((tm, d), lambda i: (i, 0)),
                  pl.BlockSpec((tm, d), lambda i: (i, 1)),
                  pl.BlockSpec((tm, d), lambda i: (i, 0)),
                  pl.BlockSpec((wa, d), const, pipeline_mode=single),
                  pl.BlockSpec((wb, d), const, pipeline_mode=single),
                  pl.BlockSpec((d, d), const, pipeline_mode=single),
                  pl.BlockSpec((1, d), const),
                  pl.BlockSpec((1, d), const)],
        out_specs=[pl.BlockSpec((tm, d), lambda i: (i, 0)),
                   pl.BlockSpec((tm, d), lambda i: (i, 0))],
        out_shape=[jax.ShapeDtypeStruct((m, d), F32),
                   jax.ShapeDtypeStruct((m, d), BF16)],
        compiler_params=_params("parallel"),
        name="merge_outproj",
    )(y_a, y_b, p_gate, p_gate, x2, w_ua, w_ub, w_out, g_post.reshape(1, d), g_pre.reshape(1, d))


def _ffn_up_kernel(h_ref, wg_ref, wu_ref, cw_ref, cb_ref, a_ref, gbuf_ref, *, tm, tiles_per_seq):
    kw = cw_ref.shape[0]
    h = h_ref[...]

    @pl.when(pl.program_id(1) % tiles_per_seq == 0)
    def _():
        gbuf_ref[0:SUBLANES, :] = jnp.zeros((SUBLANES, gbuf_ref.shape[1]), F32)

    gbuf_ref[SUBLANES:SUBLANES + tm, :] = _dot(h, wg_ref[...])
    acc = cb_ref[...] + cw_ref[kw - 1:kw, :] * gbuf_ref[SUBLANES:SUBLANES + tm, :]
    for j in range(kw - 1):
        s = kw - 1 - j
        acc = acc + cw_ref[j:j + 1, :] * gbuf_ref[SUBLANES - s:SUBLANES - s + tm, :]
    gbuf_ref[0:SUBLANES, :] = gbuf_ref[tm:tm + SUBLANES, :]
    a_ref[...] = (jax.nn.gelu(acc, approximate=True) * _dot(h, wu_ref[...])).astype(a_ref.dtype)


def _ffn_up(h2, w_gate, w_up, conv_w, conv_b, *, seq, tm=1024, tf=512):
    m, d = h2.shape
    f = w_gate.shape[1]
    kw = conv_w.shape[0]
    return pl.pallas_call(
        functools.partial(_ffn_up_kernel, tm=tm, tiles_per_seq=seq // tm),
        grid=(f // tf, m // tm),
        in_specs=[pl.BlockSpec((tm, d), lambda j, i: (i, 0)),
                  pl.BlockSpec((d, tf), lambda j, i: (0, j)),
                  pl.BlockSpec((d, tf), lambda j, i: (0, j)),
                  pl.BlockSpec((kw, tf), lambda j, i: (0, j)),
                  pl.BlockSpec((1, tf), lambda j, i: (0, j))],
        out_specs=pl.BlockSpec((tm, tf), lambda j, i: (i, j)),
        out_shape=jax.ShapeDtypeStruct((m, f), BF16),
        scratch_shapes=[pltpu.VMEM((tm + SUBLANES, tf), F32)],
        compiler_params=_params("parallel", "arbitrary"),
        name="ffn_up",
    )(h2, w_gate, w_up, conv_w, conv_b.reshape(1, f))


def _ffn_down_kernel(a_ref, wd_ref, x1_ref, g_ref, o_ref, acc_ref):
    kk = pl.program_id(1)

    @pl.when(kk == 0)
    def _():
        acc_ref[...] = jnp.zeros_like(acc_ref)

    acc_ref[...] += _dot(a_ref[...], wd_ref[...])

    @pl.when(kk == pl.num_programs(1) - 1)
    def _():
        o_ref[...] = x1_ref[...] + _rms(acc_ref[...], g_ref[...])


def _ffn_down(act, w_down, x1, gain, tm=512, tk=512):
    m, f = act.shape
    d = w_down.shape[1]
    return pl.pallas_call(
        _ffn_down_kernel,
        grid=(m // tm, f // tk),
        in_specs=[pl.BlockSpec((tm, tk), lambda i, k: (i, k)),
                  pl.BlockSpec((tk, d), lambda i, k: (k, 0)),
                  pl.BlockSpec((tm, d), lambda i, k: (i, 0)),
                  pl.BlockSpec((1, d), lambda i, k: (0, 0))],
        out_specs=pl.BlockSpec((tm, d), lambda i, k: (i, 0)),
        out_shape=jax.ShapeDtypeStruct((m, d), F32),
        scratch_shapes=[pltpu.VMEM((tm, d), F32)],
        compiler_params=_params("parallel", "arbitrary"),
        name="ffn_down",
    )(act, w_down, x1, gain.reshape(1, d))


def kernel(x, norm_mix_pre, norm_mix_post, norm_ffn_pre, norm_ffn_post, w_in, b_in, hg_lb_logits, hg_norm,
           ml_conv_w, ml_conv_b, ml_norm, w_up_a, w_up_b, w_out, ffn_w_gate, ffn_w_up, ffn_conv_w, ffn_conv_b,
           ffn_w_down):
    batch, seq, d = x.shape
    depth = w_in.shape[0]
    hg_w = HG_HEADS * HG_HEAD_DIM
    qk_w = ML_HEADS * ML_QK_DIM
    v_w = ML_HEADS * ML_V_DIM
    main_w = 4 * hg_w + 2 * qk_w + 2 * v_w
    gate_off = main_w + 2 * ML_HEADS
    lanes = 128

    x2 = x.reshape(batch * seq, d)
    for l in range(depth):
        w_main = w_in[l, :, :main_w].astype(BF16)
        w_gate = w_in[l, :, gate_off:].astype(BF16)
        w_small = jnp.pad(w_in[l, :, main_w:gate_off], ((0, 0), (0, lanes - 2 * ML_HEADS))).astype(BF16)
        b_small = jnp.pad(b_in[l, main_w:gate_off], (0, lanes - 2 * ML_HEADS))

        h1 = _rmsnorm_cast(x2, norm_mix_pre[l])
        p_main = _mm_bias(h1, w_main, b_in[l, :main_w], tm=1024, tn=1024, out_dtype=F32, name="in_proj_main")
        p_gate = _mm_bias(h1, w_gate, b_in[l, gate_off:], tm=1024, tn=1024, out_dtype=BF16, sigmoid=True,
                          name="in_proj_gate")
        p_small = _mm_bias(h1, w_small, b_small, tm=1024, tn=lanes, out_dtype=F32, name="in_proj_small")

        y_a = _hgrn2(p_main, hg_lb_logits, hg_norm[l], batch=batch, seq=seq, layer=l)
        y_b = _mlstm(p_main, p_small, ml_conv_w[l], ml_conv_b[l], ml_norm[l], batch=batch, seq=seq)

        x2, h2 = _merge(y_a, y_b, p_gate, x2, w_up_a[l].astype(BF16), w_up_b[l].astype(BF16),
                        w_out[l].astype(BF16), norm_mix_post[l], norm_ffn_pre[l])

        act = _ffn_up(h2, ffn_w_gate[l].astype(BF16), ffn_w_up[l].astype(BF16), ffn_conv_w[l], ffn_conv_b[l],
                      seq=seq)
        x2 = _ffn_down(act, ffn_w_down[l].astype(BF16), x2, norm_ffn_post[l])
    return x2.reshape(batch, seq, d)
```

```python
import functools

import jax
import jax.numpy as jnp
from jax import lax
from jax.experimental import pallas as pl
from jax.experimental.pallas import tpu as pltpu

HG_HEADS = 8
HG_HEAD_DIM = 128
ML_HEADS = 4
ML_QK_DIM = 128
ML_V_DIM = 256
CHUNK = 64
EPS = 1e-6

F32 = jnp.float32
BF16 = jnp.bfloat16

VMEM_LIMIT_BYTES = 56 * 1024 * 1024
SUBLANES = 8


def _params(*sem):
    return pltpu.CompilerParams(dimension_semantics=sem, vmem_limit_bytes=VMEM_LIMIT_BYTES)


def _dot(a, b):
    return jnp.dot(a, b, preferred_element_type=F32)


def _dot_nt(a, b):
    return lax.dot_general(a, b, (((1,), (1,)), ((), ())), preferred_element_type=F32)


def _dot_tn(a, b):
    return lax.dot_general(a, b, (((0,), (0,)), ((), ())), preferred_element_type=F32)


def _rms(u, gain):
    return u * lax.rsqrt(jnp.mean(u * u, axis=-1, keepdims=True) + EPS) * gain


def _rmsnorm_kernel(x_ref, g_ref, o_ref):
    o_ref[...] = _rms(x_ref[...], g_ref[...]).astype(o_ref.dtype)


def _rmsnorm_cast(x2, gain, tm=512):
    m, d = x2.shape
    return pl.pallas_call(
        _rmsnorm_kernel,
        grid=(m // tm,),
        in_specs=[pl.BlockSpec((tm, d), lambda i: (i, 0)),
                  pl.BlockSpec((1, d), lambda i: (0, 0))],
        out_specs=pl.BlockSpec((tm, d), lambda i: (i, 0)),
        out_shape=jax.ShapeDtypeStruct((m, d), BF16),
        compiler_params=_params("parallel"),
        name="rmsnorm_cast",
    )(x2, gain.reshape(1, d))


def _mm_bias_kernel(h_ref, w_ref, b_ref, o_ref, *, sigmoid):
    acc = _dot(h_ref[...], w_ref[...]) + b_ref[...]
    if sigmoid:
        acc = jax.nn.sigmoid(acc)
    o_ref[...] = acc.astype(o_ref.dtype)


def _mm_bias(h, w, b, *, tm, tn, out_dtype, sigmoid=False, name):
    m, k = h.shape
    n = w.shape[1]
    return pl.pallas_call(
        functools.partial(_mm_bias_kernel, sigmoid=sigmoid),
        grid=(n // tn, m // tm),
        in_specs=[pl.BlockSpec((tm, k), lambda j, i: (i, 0)),
                  pl.BlockSpec((k, tn), lambda j, i: (0, j)),
                  pl.BlockSpec((1, tn), lambda j, i: (0, j))],
        out_specs=pl.BlockSpec((tm, tn), lambda j, i: (i, j)),
        out_shape=jax.ShapeDtypeStruct((m, n), out_dtype),
        compiler_params=_params("parallel", "parallel"),
        name=name,
    )(h, w, b.reshape(1, n))


def _tril(n):
    r = lax.broadcasted_iota(jnp.int32, (n, n), 0)
    c = lax.broadcasted_iota(jnp.int32, (n, n), 1)
    return r >= c


def _chunk_cumsum(x):
    pos = lax.broadcasted_iota(jnp.int32, x.shape, 0) & (CHUNK - 1)
    s = 1
    while s < CHUNK:
        x = x + jnp.where(pos >= s, pltpu.roll(x, s, axis=0), 0.0)
        s *= 2
    return x


def _hgrn2_kernel(q_ref, f_ref, i_ref, g_ref, lbl_ref, gain_ref, o_ref, st_ref, *, layer, n_chunks):
    @pl.when(pl.program_id(2) == 0)
    def _():
        st_ref[...] = jnp.zeros_like(st_ref)

    logits = lbl_ref[...]
    e = jnp.exp(logits - jnp.max(logits, axis=0, keepdims=True))
    sm = e / jnp.sum(e, axis=0, keepdims=True)
    lb = jnp.sum(sm[:layer + 1], axis=0, keepdims=True)

    q = jax.nn.silu(q_ref[...]) * (HG_HEAD_DIM ** -0.5)
    f = lb + (1.0 - lb) * jax.nn.sigmoid(f_ref[...])
    k = 1.0 - f
    b = _chunk_cumsum(jnp.log(f))
    v = i_ref[...].astype(BF16)
    causal = _tril(CHUNK)
    q_dec = (q * jnp.exp(b)).astype(BF16)
    k_inv = k * jnp.exp(-b)
    k_inv_b = k_inv.astype(BF16)

    chunks = [slice(c * CHUNK, (c + 1) * CHUNK) for c in range(n_chunks)]
    decay = [jnp.exp(b[sl.stop - 1:sl.stop, :]) for sl in chunks]
    attn = [jnp.where(causal, _dot_nt(q_dec[sl], k_inv_b[sl]), 0.0).astype(BF16) for sl in chunks]
    kv = [_dot_tn(v[sl], (k_inv[sl] * d).astype(BF16)) for sl, d in zip(chunks, decay)]
    o_intra = [_dot(a, v[sl]) for a, sl in zip(attn, chunks)]
    st = st_ref[...]
    outs = []
    for c, sl in enumerate(chunks):
        outs.append(o_intra[c] + _dot_nt(q_dec[sl], st.astype(BF16)))
        st = decay[c] * st + kv[c]
    st_ref[...] = st
    o = jnp.concatenate(outs, axis=0) if n_chunks > 1 else outs[0]
    o_ref[...] = (_rms(o, gain_ref[...]) * jax.nn.silu(g_ref[...])).astype(o_ref.dtype)


def _hgrn2(p_main, lb_logits, gain, *, batch, seq, layer, tt=512):
    m = p_main.shape[0]
    nt = seq // tt
    w = HG_HEADS * HG_HEAD_DIM
    dk = HG_HEAD_DIM

    def col(blk):
        return pl.BlockSpec((tt, dk), lambda b, h, t: (b * nt + t, blk * HG_HEADS + h))

    return pl.pallas_call(
        functools.partial(_hgrn2_kernel, layer=layer, n_chunks=tt // CHUNK),
        grid=(batch, HG_HEADS, nt),
        in_specs=[col(0), col(1), col(2), col(3),
                  pl.BlockSpec((lb_logits.shape[0], dk), lambda b, h, t: (0, h)),
                  pl.BlockSpec((1, dk), lambda b, h, t: (0, h))],
        out_specs=pl.BlockSpec((tt, dk), lambda b, h, t: (b * nt + t, h)),
        out_shape=jax.ShapeDtypeStruct((m, w), BF16),
        scratch_shapes=[pltpu.VMEM((dk, dk), F32)],
        compiler_params=_params("parallel", "parallel", "arbitrary"),
        name="hgrn2",
    )(p_main, p_main, p_main, p_main, lb_logits, gain.reshape(1, w))


def _causal_conv_silu(buf_ref, x_ref, w_ref, b_ref, tt, first):
    kw = w_ref.shape[0]

    @pl.when(first)
    def _():
        buf_ref[0:SUBLANES, :] = jnp.zeros((SUBLANES, buf_ref.shape[1]), F32)

    buf_ref[SUBLANES:SUBLANES + tt, :] = x_ref[...]
    acc = b_ref[...] + w_ref[kw - 1:kw, :] * x_ref[...]
    for j in range(kw - 1):
        s = kw - 1 - j
        acc = acc + w_ref[j:j + 1, :] * buf_ref[SUBLANES - s:SUBLANES - s + tt, :]
    buf_ref[0:SUBLANES, :] = buf_ref[tt:tt + SUBLANES, :]
    return jax.nn.silu(acc)


def _mlstm_kernel(q_ref, k_ref, v_ref, o_ref, gates_ref, wq_ref, wk_ref, bq_ref, bk_ref, gain_ref,
                  y_ref, ct_ref, m_ref, qbuf_ref, kbuf_ref, *, n_chunks, tt):
    h = pl.program_id(1)
    first = pl.program_id(2) == 0

    @pl.when(first)
    def _():
        ct_ref[...] = jnp.zeros_like(ct_ref)
        m_ref[...] = jnp.zeros_like(m_ref)

    q = _causal_conv_silu(qbuf_ref, q_ref, wq_ref, bq_ref, tt, first)
    k = _causal_conv_silu(kbuf_ref, k_ref, wk_ref, bk_ref, tt, first) * (ML_QK_DIM ** -0.5)
    qb = q.astype(BF16)
    v = v_ref[...]
    v_aug = jnp.concatenate([v, jnp.ones((tt, ML_QK_DIM), F32)], axis=1).astype(BF16)

    kb = k.astype(BF16)
    gates = gates_ref[...]
    lane = lax.broadcasted_iota(jnp.int32, gates.shape, 1)
    ig = jnp.broadcast_to(jnp.sum(jnp.where(lane == h, gates, 0.0), axis=1, keepdims=True), gates.shape)
    fg = jnp.broadcast_to(jnp.sum(jnp.where(lane == ML_HEADS + h, gates, 0.0), axis=1, keepdims=True), gates.shape)
    b = _chunk_cumsum(jax.nn.log_sigmoid(fg))
    igb = ig - b
    causal = _tril(CHUNK)
    chunks = [slice(c * CHUNK, (c + 1) * CHUNK) for c in range(n_chunks)]

    b_last = [b[sl.stop - 1:sl.stop, 0:1] for sl in chunks]
    log_d = [jnp.where(causal, b[sl, :CHUNK] + igb[sl].T[:CHUNK, :], -jnp.inf) for sl in chunks]
    m_intra = [jnp.max(ld, axis=1, keepdims=True) for ld in log_d]
    logw = [bl + igb[sl, 0:1] for bl, sl in zip(b_last, chunks)]
    logw_max = [jnp.max(lw, axis=0, keepdims=True) for lw in logw]
    qk = [_dot_nt(qb[sl], kb[sl]) for sl in chunks]

    m_prev = m_ref[0:1, 0:1]
    m_in, m_out = [], []
    for c in range(n_chunks):
        m_in.append(m_prev)
        m_prev = jnp.maximum(b_last[c] + m_prev, logw_max[c])
        m_out.append(m_prev)
    m_ref[...] = jnp.broadcast_to(m_prev, m_ref.shape)

    m_inter = [b[sl, 0:1] + mi for sl, mi in zip(chunks, m_in)]
    m_tot = [jnp.maximum(a, bb) for a, bb in zip(m_inter, m_intra)]
    scores = [(qk[c] * jnp.exp(log_d[c] - m_tot[c])).astype(BF16) for c in range(n_chunks)]
    intra = [_dot(scores[c], v_aug[sl]) for c, sl in enumerate(chunks)]
    kv = [_dot_tn(v_aug[sl], (k[sl] * jnp.exp(logw[c] - m_out[c])).astype(BF16))
          for c, sl in enumerate(chunks)]
    ct = ct_ref[...]
    outs = []
    for c, sl in enumerate(chunks):
        tot = intra[c] + jnp.exp(m_inter[c] - m_tot[c]) * _dot_nt(qb[sl], ct.astype(BF16))
        num = tot[:, :ML_V_DIM]
        den = tot[:, ML_V_DIM:ML_V_DIM + 1]
        outs.append(num / jnp.maximum(jnp.abs(den), jnp.exp(-m_tot[c])))
        ct = jnp.exp(b_last[c] + m_in[c] - m_out[c]) * ct + kv[c]
    ct_ref[...] = ct
    hh = jnp.concatenate(outs, axis=0) if n_chunks > 1 else outs[0]
    y_ref[...] = (jax.nn.sigmoid(o_ref[...]) * _rms(hh, gain_ref[...])).astype(y_ref.dtype)


def _mlstm(p_main, gates, conv_w, conv_b, gain, *, batch, seq, tt=512):
    m = p_main.shape[0]
    nt = seq // tt
    dk, dv = ML_QK_DIM, ML_V_DIM
    qk_w = ML_HEADS * dk
    v_w = ML_HEADS * dv
    hg_w = HG_HEADS * HG_HEAD_DIM
    q_blk = 4 * hg_w // dk
    k_blk = q_blk + ML_HEADS
    v_blk = (4 * hg_w + 2 * qk_w) // dv
    o_blk = v_blk + ML_HEADS
    kw = conv_w.shape[0]

    def rows(b, h, t):
        return b * nt + t

    return pl.pallas_call(
        functools.partial(_mlstm_kernel, n_chunks=tt // CHUNK, tt=tt),
        grid=(batch, ML_HEADS, nt),
        in_specs=[pl.BlockSpec((tt, dk), lambda b, h, t: (rows(b, h, t), q_blk + h)),
                  pl.BlockSpec((tt, dk), lambda b, h, t: (rows(b, h, t), k_blk + h)),
                  pl.BlockSpec((tt, dv), lambda b, h, t: (rows(b, h, t), v_blk + h)),
                  pl.BlockSpec((tt, dv), lambda b, h, t: (rows(b, h, t), o_blk + h)),
                  pl.BlockSpec((tt, gates.shape[1]), lambda b, h, t: (rows(b, h, t), 0)),
                  pl.BlockSpec((kw, dk), lambda b, h, t: (0, h)),
                  pl.BlockSpec((kw, dk), lambda b, h, t: (0, ML_HEADS + h)),
                  pl.BlockSpec((1, dk), lambda b, h, t: (0, h)),
                  pl.BlockSpec((1, dk), lambda b, h, t: (0, ML_HEADS + h)),
                  pl.BlockSpec((1, dv), lambda b, h, t: (0, h))],
        out_specs=pl.BlockSpec((tt, dv), lambda b, h, t: (rows(b, h, t), h)),
        out_shape=jax.ShapeDtypeStruct((m, v_w), BF16),
        scratch_shapes=[pltpu.VMEM((dv + dk, dk), F32),
                        pltpu.VMEM((SUBLANES, 128), F32),
                        pltpu.VMEM((tt + SUBLANES, dk), F32),
                        pltpu.VMEM((tt + SUBLANES, dk), F32)],
        compiler_params=_params("parallel", "parallel", "arbitrary"),
        name="mlstm",
    )(p_main, p_main, p_main, p_main, gates, conv_w, conv_w,
      conv_b.reshape(1, 2 * qk_w), conv_b.reshape(1, 2 * qk_w), gain.reshape(1, v_w))


def _merge_kernel(ya_ref, yb_ref, ga_ref, gb_ref, x_ref, wua_ref, wub_ref, wout_ref, gpost_ref, gpre_ref,
                  x1_ref, h2_ref):
    merged = (ga_ref[...].astype(F32) * _dot(ya_ref[...], wua_ref[...])
              + gb_ref[...].astype(F32) * _dot(yb_ref[...], wub_ref[...]))
    mix = _dot(merged.astype(BF16), wout_ref[...])
    x1 = x_ref[...] + _rms(mix, gpost_ref[...])
    x1_ref[...] = x1
    h2_ref[...] = _rms(x1, gpre_ref[...]).astype(h2_ref.dtype)


def _merge(y_a, y_b, p_gate, x2, w_ua, w_ub, w_out, g_post, g_pre, tm=256):
    m, d = x2.shape
    wa = y_a.shape[1]
    wb = y_b.shape[1]
    const = lambda i: (0, 0)
    single = pl.Buffered(1)
    return pl.pallas_call(
        _merge_kernel,
        grid=(m // tm,),
        in_specs=[pl.BlockSpec((tm, wa), lambda i: (i, 0)),
                  pl.BlockSpec((tm, wb), lambda i: (i, 0)),
                  pl.BlockSpec((tm, d), lambda i: (i, 0)),
                  pl.BlockSpec((tm, d), lambda i: (i, 1)),
                  pl.BlockSpec((tm, d), lambda i: (i, 0)),
                  pl.BlockSpec((wa, d), const, pipeline_mode=single),
                  pl.BlockSpec((wb, d), const, pipeline_mode=single),
                  pl.BlockSpec((d, d), const, pipeline_mode=single),
                  pl.BlockSpec((1, d), const),
                  pl.BlockSpec((1, d), const)],
        out_specs=[pl.BlockSpec((tm, d), lambda i: (i, 0)),
                   pl.BlockSpec((tm, d), lambda i: (i, 0))],
        out_shape=[jax.ShapeDtypeStruct((m, d), F32),
                   jax.ShapeDtypeStruct((m, d), BF16)],
        compiler_params=_params("parallel"),
        name="merge_outproj",
    )(y_a, y_b, p_gate, p_gate, x2, w_ua, w_ub, w_out, g_post.reshape(1, d), g_pre.reshape(1, d))


def _ffn_up_kernel(h_ref, wg_ref, wu_ref, cw_ref, cb_ref, a_ref, gbuf_ref, *, tm, tiles_per_seq):
    kw = cw_ref.shape[0]
    h = h_ref[...]

    @pl.when(pl.program_id(1) % tiles_per_seq == 0)
    def _():
        gbuf_ref[0:SUBLANES, :] = jnp.zeros((SUBLANES, gbuf_ref.shape[1]), F32)

    gbuf_ref[SUBLANES:SUBLANES + tm, :] = _dot(h, wg_ref[...])
    acc = cb_ref[...] + cw_ref[kw - 1:kw, :] * gbuf_ref[SUBLANES:SUBLANES + tm, :]
    for j in range(kw - 1):
        s = kw - 1 - j
        acc = acc + cw_ref[j:j + 1, :] * gbuf_ref[SUBLANES - s:SUBLANES - s + tm, :]
    gbuf_ref[0:SUBLANES, :] = gbuf_ref[tm:tm + SUBLANES, :]
    a_ref[...] = (jax.nn.gelu(acc, approximate=True) * _dot(h, wu_ref[...])).astype(a_ref.dtype)


def _ffn_up(h2, w_gate, w_up, conv_w, conv_b, *, seq, tm=1024, tf=512):
    m, d = h2.shape
    f = w_gate.shape[1]
    kw = conv_w.shape[0]
    return pl.pallas_call(
        functools.partial(_ffn_up_kernel, tm=tm, tiles_per_seq=seq // tm),
        grid=(f // tf, m // tm),
        in_specs=[pl.BlockSpec((tm, d), lambda j, i: (i, 0)),
                  pl.BlockSpec((d, tf), lambda j, i: (0, j)),
                  pl.BlockSpec((d, tf), lambda j, i: (0, j)),
                  pl.BlockSpec((kw, tf), lambda j, i: (0, j)),
                  pl.BlockSpec((1, tf), lambda j, i: (0, j))],
        out_specs=pl.BlockSpec((tm, tf), lambda j, i: (i, j)),
        out_shape=jax.ShapeDtypeStruct((m, f), BF16),
        scratch_shapes=[pltpu.VMEM((tm + SUBLANES, tf), F32)],
        compiler_params=_params("parallel", "arbitrary"),
        name="ffn_up",
    )(h2, w_gate, w_up, conv_w, conv_b.reshape(1, f))


def _ffn_down_kernel(a_ref, wd_ref, x1_ref, g_ref, o_ref, acc_ref):
    kk = pl.program_id(1)

    @pl.when(kk == 0)
    def _():
        acc_ref[...] = jnp.zeros_like(acc_ref)

    acc_ref[...] += _dot(a_ref[...], wd_ref[...])

    @pl.when(kk == pl.num_programs(1) - 1)
    def _():
        o_ref[...] = x1_ref[...] + _rms(acc_ref[...], g_ref[...])


def _ffn_down(act, w_down, x1, gain, tm=512, tk=512):
    m, f = act.shape
    d = w_down.shape[1]
    return pl.pallas_call(
        _ffn_down_kernel,
        grid=(m // tm, f // tk),
        in_specs=[pl.BlockSpec((tm, tk), lambda i, k: (i, k)),
                  pl.BlockSpec((tk, d), lambda i, k: (k, 0)),
                  pl.BlockSpec((tm, d), lambda i, k: (i, 0)),
                  pl.BlockSpec((1, d), lambda i, k: (0, 0))],
        out_specs=pl.BlockSpec((tm, d), lambda i, k: (i, 0)),
        out_shape=jax.ShapeDtypeStruct((m, d), F32),
        scratch_shapes=[pltpu.VMEM((tm, d), F32)],
        compiler_params=_params("parallel", "arbitrary"),
        name="ffn_down",
    )(act, w_down, x1, gain.reshape(1, d))


def kernel(x, norm_mix_pre, norm_mix_post, norm_ffn_pre, norm_ffn_post, w_in, b_in, hg_lb_logits, hg_norm,
           ml_conv_w, ml_conv_b, ml_norm, w_up_a, w_up_b, w_out, ffn_w_gate, ffn_w_up, ffn_conv_w, ffn_conv_b,
           ffn_w_down):
    batch, seq, d = x.shape
    depth = w_in.shape[0]
    hg_w = HG_HEADS * HG_HEAD_DIM
    qk_w = ML_HEADS * ML_QK_DIM
    v_w = ML_HEADS * ML_V_DIM
    main_w = 4 * hg_w + 2 * qk_w + 2 * v_w
    gate_off = main_w + 2 * ML_HEADS
    lanes = 128

    x2 = x.reshape(batch * seq, d)
    for l in range(depth):
        w_main = w_in[l, :, :main_w].astype(BF16)
        w_gate = w_in[l, :, gate_off:].astype(BF16)
        w_small = jnp.pad(w_in[l, :, main_w:gate_off], ((0, 0), (0, lanes - 2 * ML_HEADS))).astype(BF16)
        b_small = jnp.pad(b_in[l, main_w:gate_off], (0, lanes - 2 * ML_HEADS))

        h1 = _rmsnorm_cast(x2, norm_mix_pre[l])
        p_main = _mm_bias(h1, w_main, b_in[l, :main_w], tm=1024, tn=1024, out_dtype=F32, name="in_proj_main")
        p_gate = _mm_bias(h1, w_gate, b_in[l, gate_off:], tm=1024, tn=1024, out_dtype=BF16, sigmoid=True,
                          name="in_proj_gate")
        p_small = _mm_bias(h1, w_small, b_small, tm=1024, tn=lanes, out_dtype=F32, name="in_proj_small")

        y_a = _hgrn2(p_main, hg_lb_logits, hg_norm[l], batch=batch, seq=seq, layer=l)
        y_b = _mlstm(p_main, p_small, ml_conv_w[l], ml_conv_b[l], ml_norm[l], batch=batch, seq=seq)

        x2, h2 = _merge(y_a, y_b, p_gate, x2, w_up_a[l].astype(BF16), w_up_b[l].astype(BF16),
                        w_out[l].astype(BF16), norm_mix_post[l], norm_ffn_pre[l])

        act = _ffn_up(h2, ffn_w_gate[l].astype(BF16), ffn_w_up[l].astype(BF16), ffn_conv_w[l], ffn_conv_b[l],
                      seq=seq)
        x2 = _ffn_down(act, ffn_w_down[l].astype(BF16), x2, norm_ffn_post[l])
    return x2.reshape(batch, seq, d)
```

```python
import functools

import jax
import jax.numpy as jnp
from jax import lax
from jax.experimental import pallas as pl
from jax.experimental.pallas import tpu as pltpu

HG_HEADS = 8
HG_HEAD_DIM = 128
ML_HEADS = 4
ML_QK_DIM = 128
ML_V_DIM = 256
CHUNK = 64
EPS = 1e-6

F32 = jnp.float32
BF16 = jnp.bfloat16

VMEM_LIMIT_BYTES = 56 * 1024 * 1024
SUBLANES = 8


def _params(*sem):
    return pltpu.CompilerParams(dimension_semantics=sem, vmem_limit_bytes=VMEM_LIMIT_BYTES)


def _dot(a, b):
    return jnp.dot(a, b, preferred_element_type=F32)


def _dot_nt(a, b):
    return lax.dot_general(a, b, (((1,), (1,)), ((), ())), preferred_element_type=F32)


def _dot_tn(a, b):
    return lax.dot_general(a, b, (((0,), (0,)), ((), ())), preferred_element_type=F32)


def _rms(u, gain):
    return u * lax.rsqrt(jnp.mean(u * u, axis=-1, keepdims=True) + EPS) * gain


def _rmsnorm_kernel(x_ref, g_ref, o_ref):
    o_ref[...] = _rms(x_ref[...], g_ref[...]).astype(o_ref.dtype)


def _rmsnorm_cast(x2, gain, tm=512):
    m, d = x2.shape
    return pl.pallas_call(
        _rmsnorm_kernel,
        grid=(m // tm,),
        in_specs=[pl.BlockSpec((tm, d), lambda i: (i, 0)),
                  pl.BlockSpec((1, d), lambda i: (0, 0))],
        out_specs=pl.BlockSpec((tm, d), lambda i: (i, 0)),
        out_shape=jax.ShapeDtypeStruct((m, d), BF16),
        compiler_params=_params("parallel"),
        name="rmsnorm_cast",
    )(x2, gain.reshape(1, d))


def _mm_bias_kernel(h_ref, w_ref, b_ref, o_ref, wb_ref, *, sigmoid):
    @pl.when(pl.program_id(1) == 0)
    def _():
        wb_ref[...] = w_ref[...].astype(BF16)

    acc = _dot(h_ref[...], wb_ref[...]) + b_ref[...]
    if sigmoid:
        acc = jax.nn.sigmoid(acc)
    o_ref[...] = acc.astype(o_ref.dtype)


def _mm_bias(h, w, b, *, layer, n, tm, tn, out_dtype, sigmoid=False, name):
    m, k = h.shape
    return pl.pallas_call(
        functools.partial(_mm_bias_kernel, sigmoid=sigmoid),
        grid=(n // tn, m // tm),
        in_specs=[pl.BlockSpec((tm, k), lambda j, i: (i, 0)),
                  pl.BlockSpec((None, k, tn), lambda j, i: (layer, 0, j)),
                  pl.BlockSpec((1, tn), lambda j, i: (0, j))],
        out_specs=pl.BlockSpec((tm, tn), lambda j, i: (i, j)),
        out_shape=jax.ShapeDtypeStruct((m, n), out_dtype),
        scratch_shapes=[pltpu.VMEM((k, tn), BF16)],
        compiler_params=_params("parallel", "arbitrary"),
        name=name,
    )(h, w, b.reshape(1, -1))


def _tril(n):
    r = lax.broadcasted_iota(jnp.int32, (n, n), 0)
    c = lax.broadcasted_iota(jnp.int32, (n, n), 1)
    return r >= c


def _chunk_cumsum(x):
    pos = lax.broadcasted_iota(jnp.int32, x.shape, 0) & (CHUNK - 1)
    s = 1
    while s < CHUNK:
        x = x + jnp.where(pos >= s, pltpu.roll(x, s, axis=0), 0.0)
        s *= 2
    return x


def _hgrn2_kernel(q_ref, f_ref, i_ref, g_ref, lbl_ref, gain_ref, o_ref, st_ref, *, layer, n_chunks):
    @pl.when(pl.program_id(2) == 0)
    def _():
        st_ref[...] = jnp.zeros_like(st_ref)

    logits = lbl_ref[...]
    e = jnp.exp(logits - jnp.max(logits, axis=0, keepdims=True))
    sm = e / jnp.sum(e, axis=0, keepdims=True)
    lb = jnp.sum(sm[:layer + 1], axis=0, keepdims=True)

    q = jax.nn.silu(q_ref[...]) * (HG_HEAD_DIM ** -0.5)
    f = lb + (1.0 - lb) * jax.nn.sigmoid(f_ref[...])
    k = 1.0 - f
    b = _chunk_cumsum(jnp.log(f))
    v = i_ref[...].astype(BF16)
    causal = _tril(CHUNK)
    q_dec = (q * jnp.exp(b)).astype(BF16)
    k_inv = k * jnp.exp(-b)
    k_inv_b = k_inv.astype(BF16)

    chunks = [slice(c * CHUNK, (c + 1) * CHUNK) for c in range(n_chunks)]
    decay = [jnp.exp(b[sl.stop - 1:sl.stop, :]) for sl in chunks]
    attn = [jnp.where(causal, _dot_nt(q_dec[sl], k_inv_b[sl]), 0.0).astype(BF16) for sl in chunks]
    kv = [_dot_tn(v[sl], (k_inv[sl] * d).astype(BF16)) for sl, d in zip(chunks, decay)]
    o_intra = [_dot(a, v[sl]) for a, sl in zip(attn, chunks)]
    st = st_ref[...]
    outs = []
    for c, sl in enumerate(chunks):
        outs.append(o_intra[c] + _dot_nt(q_dec[sl], st.astype(BF16)))
        st = decay[c] * st + kv[c]
    st_ref[...] = st
    o = jnp.concatenate(outs, axis=0) if n_chunks > 1 else outs[0]
    o_ref[...] = (_rms(o, gain_ref[...]) * jax.nn.silu(g_ref[...])).astype(o_ref.dtype)


def _hgrn2(p_main, lb_logits, gain, *, batch, seq, layer, tt=512):
    m = p_main.shape[0]
    nt = seq // tt
    w = HG_HEADS * HG_HEAD_DIM
    dk = HG_HEAD_DIM

    def col(blk):
        return pl.BlockSpec((tt, dk), lambda b, h, t: (b * nt + t, blk * HG_HEADS + h))

    return pl.pallas_call(
        functools.partial(_hgrn2_kernel, layer=layer, n_chunks=tt // CHUNK),
        grid=(batch, HG_HEADS, nt),
        in_specs=[col(0), col(1), col(2), col(3),
                  pl.BlockSpec((lb_logits.shape[0], dk), lambda b, h, t: (0, h)),
                  pl.BlockSpec((1, dk), lambda b, h, t: (0, h))],
        out_specs=pl.BlockSpec((tt, dk), lambda b, h, t: (b * nt + t, h)),
        out_shape=jax.ShapeDtypeStruct((m, w), BF16),
        scratch_shapes=[pltpu.VMEM((dk, dk), F32)],
        compiler_params=_params("parallel", "parallel", "arbitrary"),
        name="hgrn2",
    )(p_main, p_main, p_main, p_main, lb_logits, gain.reshape(1, w))


def _causal_conv_silu(buf_ref, x_ref, w_ref, b_ref, tt, first):
    kw = w_ref.shape[0]

    @pl.when(first)
    def _():
        buf_ref[0:SUBLANES, :] = jnp.zeros((SUBLANES, buf_ref.shape[1]), F32)

    buf_ref[SUBLANES:SUBLANES + tt, :] = x_ref[...]
    acc = b_ref[...] + w_ref[kw - 1:kw, :] * x_ref[...]
    for j in range(kw - 1):
        s = kw - 1 - j
        acc = acc + w_ref[j:j + 1, :] * buf_ref[SUBLANES - s:SUBLANES - s + tt, :]
    buf_ref[0:SUBLANES, :] = buf_ref[tt:tt + SUBLANES, :]
    return jax.nn.silu(acc)


def _mlstm_kernel(q_ref, k_ref, v_ref, o_ref, gates_ref, wq_ref, wk_ref, bq_ref, bk_ref, gain_ref,
                  y_ref, ct_ref, m_ref, qbuf_ref, kbuf_ref, *, n_chunks, tt):
    h = pl.program_id(1)
    first = pl.program_id(2) == 0

    @pl.when(first)
    def _():
        ct_ref[...] = jnp.zeros_like(ct_ref)
        m_ref[...] = jnp.zeros_like(m_ref)

    q = _causal_conv_silu(qbuf_ref, q_ref, wq_ref, bq_ref, tt, first)
    k = _causal_conv_silu(kbuf_ref, k_ref, wk_ref, bk_ref, tt, first) * (ML_QK_DIM ** -0.5)
    qb = q.astype(BF16)
    v = v_ref[...]
    v_aug = jnp.concatenate([v, jnp.ones((tt, ML_QK_DIM), F32)], axis=1).astype(BF16)

    kb = k.astype(BF16)
    gates = gates_ref[...]
    lane = lax.broadcasted_iota(jnp.int32, gates.shape, 1)
    ig = jnp.broadcast_to(jnp.sum(jnp.where(lane == h, gates, 0.0), axis=1, keepdims=True), gates.shape)
    fg = jnp.broadcast_to(jnp.sum(jnp.where(lane == ML_HEADS + h, gates, 0.0), axis=1, keepdims=True), gates.shape)
    b = _chunk_cumsum(jax.nn.log_sigmoid(fg))
    igb = ig - b
    causal = _tril(CHUNK)
    chunks = [slice(c * CHUNK, (c + 1) * CHUNK) for c in range(n_chunks)]

    b_last = [b[sl.stop - 1:sl.stop, 0:1] for sl in chunks]
    log_d = [jnp.where(causal, b[sl, :CHUNK] + igb[sl].T[:CHUNK, :], -jnp.inf) for sl in chunks]
    m_intra = [jnp.max(ld, axis=1, keepdims=True) for ld in log_d]
    logw = [bl + igb[sl, 0:1] for bl, sl in zip(b_last, chunks)]
    logw_max = [jnp.max(lw, axis=0, keepdims=True) for lw in logw]
    qk = [_dot_nt(qb[sl], kb[sl]) for sl in chunks]

    m_prev = m_ref[0:1, 0:1]
    m_in, m_out = [], []
    for c in range(n_chunks):
        m_in.append(m_prev)
        m_prev = jnp.maximum(b_last[c] + m_prev, logw_max[c])
        m_out.append(m_prev)
    m_ref[...] = jnp.broadcast_to(m_prev, m_ref.shape)

    m_inter = [b[sl, 0:1] + mi for sl, mi in zip(chunks, m_in)]
    m_tot = [jnp.maximum(a, bb) for a, bb in zip(m_inter, m_intra)]
    scores = [(qk[c] * jnp.exp(log_d[c] - m_tot[c])).astype(BF16) for c in range(n_chunks)]
    intra = [_dot(scores[c], v_aug[sl]) for c, sl in enumerate(chunks)]
    kv = [_dot_tn(v_aug[sl], (k[sl] * jnp.exp(logw[c] - m_out[c])).astype(BF16))
          for c, sl in enumerate(chunks)]
    ct = ct_ref[...]
    outs = []
    for c, sl in enumerate(chunks):
        tot = intra[c] + jnp.exp(m_inter[c] - m_tot[c]) * _dot_nt(qb[sl], ct.astype(BF16))
        num = tot[:, :ML_V_DIM]
        den = tot[:, ML_V_DIM:ML_V_DIM + 1]
        outs.append(num / jnp.maximum(jnp.abs(den), jnp.exp(-m_tot[c])))
        ct = jnp.exp(b_last[c] + m_in[c] - m_out[c]) * ct + kv[c]
    ct_ref[...] = ct
    hh = jnp.concatenate(outs, axis=0) if n_chunks > 1 else outs[0]
    y_ref[...] = (jax.nn.sigmoid(o_ref[...]) * _rms(hh, gain_ref[...])).astype(y_ref.dtype)


def _mlstm(p_main, gates, conv_w, conv_b, gain, *, batch, seq, tt=512):
    m = p_main.shape[0]
    nt = seq // tt
    dk, dv = ML_QK_DIM, ML_V_DIM
    qk_w = ML_HEADS * dk
    v_w = ML_HEADS * dv
    hg_w = HG_HEADS * HG_HEAD_DIM
    q_blk = 4 * hg_w // dk
    k_blk = q_blk + ML_HEADS
    v_blk = (4 * hg_w + 2 * qk_w) // dv
    o_blk = v_blk + ML_HEADS
    kw = conv_w.shape[0]

    def rows(b, h, t):
        return b * nt + t

    return pl.pallas_call(
        functools.partial(_mlstm_kernel, n_chunks=tt // CHUNK, tt=tt),
        grid=(batch, ML_HEADS, nt),
        in_specs=[pl.BlockSpec((tt, dk), lambda b, h, t: (rows(b, h, t), q_blk + h)),
                  pl.BlockSpec((tt, dk), lambda b, h, t: (rows(b, h, t), k_blk + h)),
                  pl.BlockSpec((tt, dv), lambda b, h, t: (rows(b, h, t), v_blk + h)),
                  pl.BlockSpec((tt, dv), lambda b, h, t: (rows(b, h, t), o_blk + h)),
                  pl.BlockSpec((tt, gates.shape[1]), lambda b, h, t: (rows(b, h, t), 0)),
                  pl.BlockSpec((kw, dk), lambda b, h, t: (0, h)),
                  pl.BlockSpec((kw, dk), lambda b, h, t: (0, ML_HEADS + h)),
                  pl.BlockSpec((1, dk), lambda b, h, t: (0, h)),
                  pl.BlockSpec((1, dk), lambda b, h, t: (0, ML_HEADS + h)),
                  pl.BlockSpec((1, dv), lambda b, h, t: (0, h))],
        out_specs=pl.BlockSpec((tt, dv), lambda b, h, t: (rows(b, h, t), h)),
        out_shape=jax.ShapeDtypeStruct((m, v_w), BF16),
        scratch_shapes=[pltpu.VMEM((dv + dk, dk), F32),
                        pltpu.VMEM((SUBLANES, 128), F32),
                        pltpu.VMEM((tt + SUBLANES, dk), F32),
                        pltpu.VMEM((tt + SUBLANES, dk), F32)],
        compiler_params=_params("parallel", "parallel", "arbitrary"),
        name="mlstm",
    )(p_main, p_main, p_main, p_main, gates, conv_w, conv_w,
      conv_b.reshape(1, 2 * qk_w), conv_b.reshape(1, 2 * qk_w), gain.reshape(1, v_w))


def _merge_kernel(ya_ref, yb_ref, ga_ref, gb_ref, x_ref, wua_ref, wub_ref, wout_ref, gpost_ref, gpre_ref,
                  x1_ref, h2_ref):
    merged = (ga_ref[...].astype(F32) * _dot(ya_ref[...], wua_ref[...])
              + gb_ref[...].astype(F32) * _dot(yb_ref[...], wub_ref[...]))
    mix = _dot(merged.astype(BF16), wout_ref[...])
    x1 = x_ref[...] + _rms(mix, gpost_ref[...])
    x1_ref[...] = x1
    h2_ref[...] = _rms(x1, gpre_ref[...]).astype(h2_ref.dtype)


def _merge(y_a, y_b, p_gate, x2, w_ua, w_ub, w_out, g_post, g_pre, tm=256):
    m, d = x2.shape
    wa = y_a.shape[1]
    wb = y_b.shape[1]
    const = lambda i: (0, 0)
    single = pl.Buffered(1)
    return pl.pallas_call(
        _merge_kernel,
        grid=(m // tm,),
        in_specs=[pl.BlockSpec((tm, wa), lambda i: (i, 0)),
                  pl.BlockSpec((tm, wb), lambda i: (i, 0)),
                  pl.BlockSpec((tm, d), lambda i: (i, 0)),
                  pl.BlockSpec((tm, d), lambda i: (i, 1)),
                  pl.BlockSpec((tm, d), lambda i: (i, 0)),
                  pl.BlockSpec((wa, d), const, pipeline_mode=single),
                  pl.BlockSpec((wb, d), const, pipeline_mode=single),
                  pl.BlockSpec((d, d), const, pipeline_mode=single),
                  pl.BlockSpec((1, d), const),
                  pl.BlockSpec((1, d), const)],
        out_specs=[pl.BlockSpec((tm, d), lambda i: (i, 0)),
                   pl.BlockSpec((tm, d), lambda i: (i, 0))],
        out_shape=[jax.ShapeDtypeStruct((m, d), F32),
                   jax.ShapeDtypeStruct((m, d), BF16)],
        compiler_params=_params("parallel"),
        name="merge_outproj",
    )(y_a, y_b, p_gate, p_gate, x2, w_ua, w_ub, w_out, g_post.reshape(1, d), g_pre.reshape(1, d))


def _ffn_up_kernel(h_ref, wg_ref, wu_ref, cw_ref, cb_ref, a_ref, gbuf_ref, wgb_ref, wub_ref, *, tm, tiles_per_seq):
    kw = cw_ref.shape[0]
    h = h_ref[...]

    @pl.when(pl.program_id(1) == 0)
    def _():
        wgb_ref[...] = wg_ref[...].astype(BF16)
        wub_ref[...] = wu_ref[...].astype(BF16)

    @pl.when(pl.program_id(1) % tiles_per_seq == 0)
    def _():
        gbuf_ref[0:SUBLANES, :] = jnp.zeros((SUBLANES, gbuf_ref.shape[1]), F32)

    gbuf_ref[SUBLANES:SUBLANES + tm, :] = _dot(h, wgb_ref[...])
    acc = cb_ref[...] + cw_ref[kw - 1:kw, :] * gbuf_ref[SUBLANES:SUBLANES + tm, :]
    for j in range(kw - 1):
        s = kw - 1 - j
        acc = acc + cw_ref[j:j + 1, :] * gbuf_ref[SUBLANES - s:SUBLANES - s + tm, :]
    gbuf_ref[0:SUBLANES, :] = gbuf_ref[tm:tm + SUBLANES, :]
    a_ref[...] = (jax.nn.gelu(acc, approximate=True) * _dot(h, wub_ref[...])).astype(a_ref.dtype)


def _ffn_up(h2, w_gate, w_up, conv_w, conv_b, *, layer, seq, tm=1024, tf=512):
    m, d = h2.shape
    f = w_gate.shape[2]
    kw = conv_w.shape[0]
    wspec = pl.BlockSpec((None, d, tf), lambda j, i: (layer, 0, j))
    return pl.pallas_call(
        functools.partial(_ffn_up_kernel, tm=tm, tiles_per_seq=seq // tm),
        grid=(f // tf, m // tm),
        in_specs=[pl.BlockSpec((tm, d), lambda j, i: (i, 0)),
                  wspec, wspec,
                  pl.BlockSpec((kw, tf), lambda j, i: (0, j)),
                  pl.BlockSpec((1, tf), lambda j, i: (0, j))],
        out_specs=pl.BlockSpec((tm, tf), lambda j, i: (i, j)),
        out_shape=jax.ShapeDtypeStruct((m, f), BF16),
        scratch_shapes=[pltpu.VMEM((tm + SUBLANES, tf), F32),
                        pltpu.VMEM((d, tf), BF16),
                        pltpu.VMEM((d, tf), BF16)],
        compiler_params=_params("arbitrary", "arbitrary"),
        name="ffn_up",
    )(h2, w_gate, w_up, conv_w, conv_b.reshape(1, f))


def _ffn_down_kernel(a_ref, wd_ref, x1_ref, g_ref, o_ref):
    kk = pl.program_id(1)

    @pl.when(kk == 0)
    def _():
        o_ref[...] = jnp.zeros_like(o_ref)

    o_ref[...] += _dot(a_ref[...], wd_ref[...])

    @pl.when(kk == pl.num_programs(1) - 1)
    def _():
        o_ref[...] = x1_ref[...] + _rms(o_ref[...], g_ref[...])


def _ffn_down(act, w_down, x1, gain, tm=1024, tk=1408):
    m, f = act.shape
    d = w_down.shape[1]
    return pl.pallas_call(
        _ffn_down_kernel,
        grid=(m // tm, f // tk),
        in_specs=[pl.BlockSpec((tm, tk), lambda i, k: (i, k)),
                  pl.BlockSpec((tk, d), lambda i, k: (k, 0)),
                  pl.BlockSpec((tm, d), lambda i, k: (i, 0), pipeline_mode=pl.Buffered(1)),
                  pl.BlockSpec((1, d), lambda i, k: (0, 0))],
        out_specs=pl.BlockSpec((tm, d), lambda i, k: (i, 0)),
        out_shape=jax.ShapeDtypeStruct((m, d), F32),
        compiler_params=_params("parallel", "arbitrary"),
        name="ffn_down",
    )(act, w_down, x1, gain.reshape(1, d))


def kernel(x, norm_mix_pre, norm_mix_post, norm_ffn_pre, norm_ffn_post, w_in, b_in, hg_lb_logits, hg_norm,
           ml_conv_w, ml_conv_b, ml_norm, w_up_a, w_up_b, w_out, ffn_w_gate, ffn_w_up, ffn_conv_w, ffn_conv_b,
           ffn_w_down):
    batch, seq, d = x.shape
    depth = w_in.shape[0]
    hg_w = HG_HEADS * HG_HEAD_DIM
    qk_w = ML_HEADS * ML_QK_DIM
    v_w = ML_HEADS * ML_V_DIM
    main_w = 4 * hg_w + 2 * qk_w + 2 * v_w
    gate_off = main_w + 2 * ML_HEADS
    lanes = 128

    x2 = x.reshape(batch * seq, d)
    for l in range(depth):
        w_gate = w_in[l:l + 1, :, gate_off:]
        w_small = jnp.pad(w_in[l:l + 1, :, main_w:gate_off], ((0, 0), (0, 0), (0, lanes - 2 * ML_HEADS)))
        b_small = jnp.pad(b_in[l, main_w:gate_off], (0, lanes - 2 * ML_HEADS))

        h1 = _rmsnorm_cast(x2, norm_mix_pre[l])
        p_main = _mm_bias(h1, w_in, b_in[l], layer=l, n=main_w, tm=1024, tn=1024, out_dtype=F32,
                          name="in_proj_main")
        p_gate = _mm_bias(h1, w_gate, b_in[l, gate_off:], layer=0, n=2 * d, tm=1024, tn=1024, out_dtype=BF16,
                          sigmoid=True, name="in_proj_gate")
        p_small = _mm_bias(h1, w_small, b_small, layer=0, n=lanes, tm=1024, tn=lanes, out_dtype=F32,
                           name="in_proj_small")

        y_a = _hgrn2(p_main, hg_lb_logits, hg_norm[l], batch=batch, seq=seq, layer=l)
        y_b = _mlstm(p_main, p_small, ml_conv_w[l], ml_conv_b[l], ml_norm[l], batch=batch, seq=seq)

        x2, h2 = _merge(y_a, y_b, p_gate, x2, w_up_a[l].astype(BF16), w_up_b[l].astype(BF16),
                        w_out[l].astype(BF16), norm_mix_post[l], norm_ffn_pre[l])

        act = _ffn_up(h2, ffn_w_gate, ffn_w_up, ffn_conv_w[l], ffn_conv_b[l], layer=l, seq=seq)
        x2 = _ffn_down(act, ffn_w_down[l].astype(BF16), x2, norm_ffn_post[l])
    return x2.reshape(batch, seq, d)
```

```python
import functools

import jax
import jax.numpy as jnp
from jax import lax
from jax.experimental import pallas as pl
from jax.experimental.pallas import tpu as pltpu

HG_HEADS = 8
HG_HEAD_DIM = 128
ML_HEADS = 4
ML_QK_DIM = 128
ML_V_DIM = 256
CHUNK = 64
EPS = 1e-6

F32 = jnp.float32
BF16 = jnp.bfloat16

VMEM_LIMIT_BYTES = 56 * 1024 * 1024
SUBLANES = 8


def _params(*sem):
    return pltpu.CompilerParams(dimension_semantics=sem, vmem_limit_bytes=VMEM_LIMIT_BYTES)


def _dot(a, b):
    return jnp.dot(a, b, preferred_element_type=F32)


def _dot_nt(a, b):
    return lax.dot_general(a, b, (((1,), (1,)), ((), ())), preferred_element_type=F32)


def _dot_tn(a, b):
    return lax.dot_general(a, b, (((0,), (0,)), ((), ())), preferred_element_type=F32)


def _rms(u, gain):
    return u * lax.rsqrt(jnp.mean(u * u, axis=-1, keepdims=True) + EPS) * gain


def _rmsnorm_kernel(x_ref, g_ref, o_ref):
    o_ref[...] = _rms(x_ref[...], g_ref[...]).astype(o_ref.dtype)


def _rmsnorm_cast(x2, gain, tm=512):
    m, d = x2.shape
    return pl.pallas_call(
        _rmsnorm_kernel,
        grid=(m // tm,),
        in_specs=[pl.BlockSpec((tm, d), lambda i: (i, 0)),
                  pl.BlockSpec((1, d), lambda i: (0, 0))],
        out_specs=pl.BlockSpec((tm, d), lambda i: (i, 0)),
        out_shape=jax.ShapeDtypeStruct((m, d), BF16),
        compiler_params=_params("parallel"),
        name="rmsnorm_cast",
    )(x2, gain.reshape(1, d))


def _mm_bias_kernel(h_ref, w_ref, b_ref, o_ref, wb_ref, *, sigmoid):
    @pl.when(pl.program_id(1) == 0)
    def _():
        wb_ref[...] = w_ref[...].astype(BF16)

    acc = _dot_nt(h_ref[...], wb_ref[...]) + b_ref[...]
    if sigmoid:
        acc = jax.nn.sigmoid(acc)
    o_ref[...] = acc.astype(o_ref.dtype)


def _mm_bias_t(h, wt, b, *, layer, row0, n, tm, tn, out_dtype, sigmoid=False, name):
    m, k = h.shape
    row0 += layer * wt.shape[1]
    wt = wt.reshape(-1, k)
    return pl.pallas_call(
        functools.partial(_mm_bias_kernel, sigmoid=sigmoid),
        grid=(n // tn, m // tm),
        in_specs=[pl.BlockSpec((tm, k), lambda j, i: (i, 0)),
                  pl.BlockSpec((pl.Element(tn), pl.Element(k)),
                               lambda j, i: (pl.multiple_of(row0 + j * tn, SUBLANES), 0)),
                  pl.BlockSpec((1, tn), lambda j, i: (0, j))],
        out_specs=pl.BlockSpec((tm, tn), lambda j, i: (i, j)),
        out_shape=jax.ShapeDtypeStruct((m, n), out_dtype),
        scratch_shapes=[pltpu.VMEM((tn, k), BF16)],
        compiler_params=_params("parallel", "arbitrary"),
        name=name,
    )(h, wt, b.reshape(1, n))


def _tril(n):
    r = lax.broadcasted_iota(jnp.int32, (n, n), 0)
    c = lax.broadcasted_iota(jnp.int32, (n, n), 1)
    return r >= c


def _chunk_cumsum(x):
    pos = lax.broadcasted_iota(jnp.int32, x.shape, 0) & (CHUNK - 1)
    s = 1
    while s < CHUNK:
        x = x + jnp.where(pos >= s, pltpu.roll(x, s, axis=0), 0.0)
        s *= 2
    return x


def _hgrn2_kernel(q_ref, f_ref, i_ref, g_ref, lbl_ref, gain_ref, o_ref, st_ref, *, layer, n_chunks):
    @pl.when(pl.program_id(2) == 0)
    def _():
        st_ref[...] = jnp.zeros_like(st_ref)

    logits = lbl_ref[...]
    e = jnp.exp(logits - jnp.max(logits, axis=0, keepdims=True))
    sm = e / jnp.sum(e, axis=0, keepdims=True)
    lb = jnp.sum(sm[:layer + 1], axis=0, keepdims=True)

    q = jax.nn.silu(q_ref[...]) * (HG_HEAD_DIM ** -0.5)
    f = lb + (1.0 - lb) * jax.nn.sigmoid(f_ref[...])
    k = 1.0 - f
    b = _chunk_cumsum(jnp.log(f))
    v = i_ref[...].astype(BF16)
    causal = _tril(CHUNK)
    q_dec = (q * jnp.exp(b)).astype(BF16)
    k_inv = k * jnp.exp(-b)
    k_inv_b = k_inv.astype(BF16)

    chunks = [slice(c * CHUNK, (c + 1) * CHUNK) for c in range(n_chunks)]
    decay = [jnp.exp(b[sl.stop - 1:sl.stop, :]) for sl in chunks]
    attn = [jnp.where(causal, _dot_nt(q_dec[sl], k_inv_b[sl]), 0.0).astype(BF16) for sl in chunks]
    kv = [_dot_tn(v[sl], (k_inv[sl] * d).astype(BF16)) for sl, d in zip(chunks, decay)]
    o_intra = [_dot(a, v[sl]) for a, sl in zip(attn, chunks)]
    st = st_ref[...]
    outs = []
    for c, sl in enumerate(chunks):
        outs.append(o_intra[c] + _dot_nt(q_dec[sl], st.astype(BF16)))
        st = decay[c] * st + kv[c]
    st_ref[...] = st
    o = jnp.concatenate(outs, axis=0) if n_chunks > 1 else outs[0]
    o_ref[...] = (_rms(o, gain_ref[...]) * jax.nn.silu(g_ref[...])).astype(o_ref.dtype)


def _hgrn2(p_main, lb_logits, gain, *, batch, seq, layer, tt=512):
    m = p_main.shape[0]
    nt = seq // tt
    w = HG_HEADS * HG_HEAD_DIM
    dk = HG_HEAD_DIM

    def col(blk):
        return pl.BlockSpec((tt, dk), lambda b, h, t: (b * nt + t, blk * HG_HEADS + h))

    return pl.pallas_call(
        functools.partial(_hgrn2_kernel, layer=layer, n_chunks=tt // CHUNK),
        grid=(batch, HG_HEADS, nt),
        in_specs=[col(0), col(1), col(2), col(3),
                  pl.BlockSpec((lb_logits.shape[0], dk), lambda b, h, t: (0, h)),
                  pl.BlockSpec((1, dk), lambda b, h, t: (0, h))],
        out_specs=pl.BlockSpec((tt, dk), lambda b, h, t: (b * nt + t, h)),
        out_shape=jax.ShapeDtypeStruct((m, w), BF16),
        scratch_shapes=[pltpu.VMEM((dk, dk), F32)],
        compiler_params=_params("parallel", "parallel", "arbitrary"),
        name="hgrn2",
    )(p_main, p_main, p_main, p_main, lb_logits, gain.reshape(1, w))


def _causal_conv_silu(buf_ref, x_ref, w_ref, b_ref, tt, first):
    kw = w_ref.shape[0]

    @pl.when(first)
    def _():
        buf_ref[0:SUBLANES, :] = jnp.zeros((SUBLANES, buf_ref.shape[1]), F32)

    buf_ref[SUBLANES:SUBLANES + tt, :] = x_ref[...]
    acc = b_ref[...] + w_ref[kw - 1:kw, :] * x_ref[...]
    for j in range(kw - 1):
        s = kw - 1 - j
        acc = acc + w_ref[j:j + 1, :] * buf_ref[SUBLANES - s:SUBLANES - s + tt, :]
    buf_ref[0:SUBLANES, :] = buf_ref[tt:tt + SUBLANES, :]
    return jax.nn.silu(acc)


def _mlstm_kernel(q_ref, k_ref, v_ref, o_ref, gates_ref, wq_ref, wk_ref, bq_ref, bk_ref, gain_ref,
                  y_ref, ct_ref, m_ref, qbuf_ref, kbuf_ref, *, n_chunks, tt):
    h = pl.program_id(1)
    first = pl.program_id(2) == 0

    @pl.when(first)
    def _():
        ct_ref[...] = jnp.zeros_like(ct_ref)
        m_ref[...] = jnp.zeros_like(m_ref)

    q = _causal_conv_silu(qbuf_ref, q_ref, wq_ref, bq_ref, tt, first)
    k = _causal_conv_silu(kbuf_ref, k_ref, wk_ref, bk_ref, tt, first) * (ML_QK_DIM ** -0.5)
    qb = q.astype(BF16)
    v = v_ref[...]
    v_aug = jnp.concatenate([v, jnp.ones((tt, ML_QK_DIM), F32)], axis=1).astype(BF16)

    kb = k.astype(BF16)
    gates = gates_ref[...]
    lane = lax.broadcasted_iota(jnp.int32, gates.shape, 1)
    wide = (tt, ML_QK_DIM)
    ig = jnp.broadcast_to(jnp.sum(jnp.where(lane == h, gates, 0.0), axis=1, keepdims=True), wide)
    fg = jnp.broadcast_to(jnp.sum(jnp.where(lane == ML_HEADS + h, gates, 0.0), axis=1, keepdims=True), wide)
    b = _chunk_cumsum(jax.nn.log_sigmoid(fg))
    igb = ig - b
    causal = _tril(CHUNK)
    chunks = [slice(c * CHUNK, (c + 1) * CHUNK) for c in range(n_chunks)]

    b_last = [b[sl.stop - 1:sl.stop, 0:1] for sl in chunks]
    log_d = [jnp.where(causal, b[sl, :CHUNK] + igb[sl].T[:CHUNK, :], -jnp.inf) for sl in chunks]
    m_intra = [jnp.max(ld, axis=1, keepdims=True) for ld in log_d]
    logw = [bl + igb[sl, 0:1] for bl, sl in zip(b_last, chunks)]
    logw_max = [jnp.max(lw, axis=0, keepdims=True) for lw in logw]
    qk = [_dot_nt(qb[sl], kb[sl]) for sl in chunks]

    m_prev = m_ref[0:1, 0:1]
    m_in, m_out = [], []
    for c in range(n_chunks):
        m_in.append(m_prev)
        m_prev = jnp.maximum(b_last[c] + m_prev, logw_max[c])
        m_out.append(m_prev)
    m_ref[...] = jnp.broadcast_to(m_prev, m_ref.shape)

    m_inter = [b[sl, 0:1] + mi for sl, mi in zip(chunks, m_in)]
    m_tot = [jnp.maximum(a, bb) for a, bb in zip(m_inter, m_intra)]
    scores = [(qk[c] * jnp.exp(log_d[c] - m_tot[c])).astype(BF16) for c in range(n_chunks)]
    intra = [_dot(scores[c], v_aug[sl]) for c, sl in enumerate(chunks)]
    kv = [_dot_tn(v_aug[sl], (k[sl] * jnp.exp(logw[c] - m_out[c])).astype(BF16))
          for c, sl in enumerate(chunks)]
    ct = ct_ref[...]
    outs = []
    for c, sl in enumerate(chunks):
        tot = intra[c] + jnp.exp(m_inter[c] - m_tot[c]) * _dot_nt(qb[sl], ct.astype(BF16))
        num = tot[:, :ML_V_DIM]
        den = tot[:, ML_V_DIM:ML_V_DIM + 1]
        outs.append(num / jnp.maximum(jnp.abs(den), jnp.exp(-m_tot[c])))
        ct = jnp.exp(b_last[c] + m_in[c] - m_out[c]) * ct + kv[c]
    ct_ref[...] = ct
    hh = jnp.concatenate(outs, axis=0) if n_chunks > 1 else outs[0]
    y_ref[...] = (jax.nn.sigmoid(o_ref[...]) * _rms(hh, gain_ref[...])).astype(y_ref.dtype)


def _mlstm(p_main, gates, conv_w, conv_b, gain, *, batch, seq, tt=512):
    m = p_main.shape[0]
    nt = seq // tt
    dk, dv = ML_QK_DIM, ML_V_DIM
    qk_w = ML_HEADS * dk
    v_w = ML_HEADS * dv
    hg_w = HG_HEADS * HG_HEAD_DIM
    q_blk = 4 * hg_w // dk
    k_blk = q_blk + ML_HEADS
    v_blk = (4 * hg_w + 2 * qk_w) // dv
    o_blk = v_blk + ML_HEADS
    kw = conv_w.shape[0]

    def rows(b, h, t):
        return b * nt + t

    return pl.pallas_call(
        functools.partial(_mlstm_kernel, n_chunks=tt // CHUNK, tt=tt),
        grid=(batch, ML_HEADS, nt),
        in_specs=[pl.BlockSpec((tt, dk), lambda b, h, t: (rows(b, h, t), q_blk + h)),
                  pl.BlockSpec((tt, dk), lambda b, h, t: (rows(b, h, t), k_blk + h)),
                  pl.BlockSpec((tt, dv), lambda b, h, t: (rows(b, h, t), v_blk + h)),
                  pl.BlockSpec((tt, dv), lambda b, h, t: (rows(b, h, t), o_blk + h)),
                  pl.BlockSpec((tt, gates.shape[1]), lambda b, h, t: (rows(b, h, t), 0)),
                  pl.BlockSpec((kw, dk), lambda b, h, t: (0, h)),
                  pl.BlockSpec((kw, dk), lambda b, h, t: (0, ML_HEADS + h)),
                  pl.BlockSpec((1, dk), lambda b, h, t: (0, h)),
                  pl.BlockSpec((1, dk), lambda b, h, t: (0, ML_HEADS + h)),
                  pl.BlockSpec((1, dv), lambda b, h, t: (0, h))],
        out_specs=pl.BlockSpec((tt, dv), lambda b, h, t: (rows(b, h, t), h)),
        out_shape=jax.ShapeDtypeStruct((m, v_w), BF16),
        scratch_shapes=[pltpu.VMEM((dv + dk, dk), F32),
                        pltpu.VMEM((SUBLANES, 128), F32),
                        pltpu.VMEM((tt + SUBLANES, dk), F32),
                        pltpu.VMEM((tt + SUBLANES, dk), F32)],
        compiler_params=_params("parallel", "parallel", "arbitrary"),
        name="mlstm",
    )(p_main, p_main, p_main, p_main, gates, conv_w, conv_w,
      conv_b.reshape(1, 2 * qk_w), conv_b.reshape(1, 2 * qk_w), gain.reshape(1, v_w))


def _merge_kernel(ya_ref, yb_ref, ga_ref, gb_ref, x_ref, wua_ref, wub_ref, wout_ref, gpost_ref, gpre_ref,
                  x1_ref, h2_ref):
    merged = (ga_ref[...].astype(F32) * _dot(ya_ref[...], wua_ref[...])
              + gb_ref[...].astype(F32) * _dot(yb_ref[...], wub_ref[...]))
    mix = _dot(merged.astype(BF16), wout_ref[...])
    x1 = x_ref[...] + _rms(mix, gpost_ref[...])
    x1_ref[...] = x1
    h2_ref[...] = _rms(x1, gpre_ref[...]).astype(h2_ref.dtype)


def _merge(y_a, y_b, p_gate, x2, w_ua, w_ub, w_out, g_post, g_pre, tm=256):
    m, d = x2.shape
    wa = y_a.shape[1]
    wb = y_b.shape[1]
    const = lambda i: (0, 0)
    single = pl.Buffered(1)
    return pl.pallas_call(
        _merge_kernel,
        grid=(m // tm,),
        in_specs=[pl.BlockSpec((tm, wa), lambda i: (i, 0)),
                  pl.BlockSpec((tm, wb), lambda i: (i, 0)),
                  pl.BlockSpec((tm, d), lambda i: (i, 0)),
                  pl.BlockSpec((tm, d), lambda i: (i, 1)),
                  pl.BlockSpec((tm, d), lambda i: (i, 0)),
                  pl.BlockSpec((wa, d), const, pipeline_mode=single),
                  pl.BlockSpec((wb, d), const, pipeline_mode=single),
                  pl.BlockSpec((d, d), const, pipeline_mode=single),
                  pl.BlockSpec((1, d), const),
                  pl.BlockSpec((1, d), const)],
        out_specs=[pl.BlockSpec((tm, d), lambda i: (i, 0)),
                   pl.BlockSpec((tm, d), lambda i: (i, 0))],
        out_shape=[jax.ShapeDtypeStruct((m, d), F32),
                   jax.ShapeDtypeStruct((m, d), BF16)],
        compiler_params=_params("parallel"),
        name="merge_outproj",
    )(y_a, y_b, p_gate, p_gate, x2, w_ua, w_ub, w_out, g_post.reshape(1, d), g_pre.reshape(1, d))


def _ffn_up_kernel(h_ref, wg_ref, wu_ref, cw_ref, cb_ref, a_ref, gbuf_ref, wgb_ref, wub_ref, *, tm, tiles_per_seq):
    kw = cw_ref.shape[0]
    h = h_ref[...]

    @pl.when(pl.program_id(1) == 0)
    def _():
        wgb_ref[...] = wg_ref[...].astype(BF16)
        wub_ref[...] = wu_ref[...].astype(BF16)

    @pl.when(pl.program_id(1) % tiles_per_seq == 0)
    def _():
        gbuf_ref[0:SUBLANES, :] = jnp.zeros((SUBLANES, gbuf_ref.shape[1]), F32)

    gbuf_ref[SUBLANES:SUBLANES + tm, :] = _dot(h, wgb_ref[...])
    acc = cb_ref[...] + cw_ref[kw - 1:kw, :] * gbuf_ref[SUBLANES:SUBLANES + tm, :]
    for j in range(kw - 1):
        s = kw - 1 - j
        acc = acc + cw_ref[j:j + 1, :] * gbuf_ref[SUBLANES - s:SUBLANES - s + tm, :]
    gbuf_ref[0:SUBLANES, :] = gbuf_ref[tm:tm + SUBLANES, :]
    a_ref[...] = (jax.nn.gelu(acc, approximate=True) * _dot(h, wub_ref[...])).astype(a_ref.dtype)


def _ffn_up(h2, w_gate, w_up, conv_w, conv_b, *, layer, seq, tm=1024, tf=512):
    m, d = h2.shape
    f = w_gate.shape[2]
    kw = conv_w.shape[0]
    wspec = pl.BlockSpec((None, d, tf), lambda j, i: (layer, 0, j))
    return pl.pallas_call(
        functools.partial(_ffn_up_kernel, tm=tm, tiles_per_seq=seq // tm),
        grid=(f // tf, m // tm),
        in_specs=[pl.BlockSpec((tm, d), lambda j, i: (i, 0)),
                  wspec, wspec,
                  pl.BlockSpec((kw, tf), lambda j, i: (0, j)),
                  pl.BlockSpec((1, tf), lambda j, i: (0, j))],
        out_specs=pl.BlockSpec((tm, tf), lambda j, i: (i, j)),
        out_shape=jax.ShapeDtypeStruct((m, f), BF16),
        scratch_shapes=[pltpu.VMEM((tm + SUBLANES, tf), F32),
                        pltpu.VMEM((d, tf), BF16),
                        pltpu.VMEM((d, tf), BF16)],
        compiler_params=_params("arbitrary", "arbitrary"),
        name="ffn_up",
    )(h2, w_gate, w_up, conv_w, conv_b.reshape(1, f))


def _ffn_down_kernel(a_ref, wd_ref, x1_ref, g_ref, o_ref):
    kk = pl.program_id(1)

    @pl.when(kk == 0)
    def _():
        o_ref[...] = jnp.zeros_like(o_ref)

    o_ref[...] += _dot(a_ref[...], wd_ref[...])

    @pl.when(kk == pl.num_programs(1) - 1)
    def _():
        o_ref[...] = x1_ref[...] + _rms(o_ref[...], g_ref[...])


def _ffn_down(act, w_down, x1, gain, tm=1024, tk=1408):
    m, f = act.shape
    d = w_down.shape[1]
    return pl.pallas_call(
        _ffn_down_kernel,
        grid=(m // tm, f // tk),
        in_specs=[pl.BlockSpec((tm, tk), lambda i, k: (i, k)),
                  pl.BlockSpec((tk, d), lambda i, k: (k, 0)),
                  pl.BlockSpec((tm, d), lambda i, k: (i, 0), pipeline_mode=pl.Buffered(1)),
                  pl.BlockSpec((1, d), lambda i, k: (0, 0))],
        out_specs=pl.BlockSpec((tm, d), lambda i, k: (i, 0)),
        out_shape=jax.ShapeDtypeStruct((m, d), F32),
        compiler_params=_params("parallel", "arbitrary"),
        name="ffn_down",
    )(act, w_down, x1, gain.reshape(1, d))


def kernel(x, norm_mix_pre, norm_mix_post, norm_ffn_pre, norm_ffn_post, w_in, b_in, hg_lb_logits, hg_norm,
           ml_conv_w, ml_conv_b, ml_norm, w_up_a, w_up_b, w_out, ffn_w_gate, ffn_w_up, ffn_conv_w, ffn_conv_b,
           ffn_w_down):
    batch, seq, d = x.shape
    depth = w_in.shape[0]
    hg_w = HG_HEADS * HG_HEAD_DIM
    qk_w = ML_HEADS * ML_QK_DIM
    v_w = ML_HEADS * ML_V_DIM
    main_w = 4 * hg_w + 2 * qk_w + 2 * v_w
    gate_off = main_w + 2 * ML_HEADS
    w_in_t = jnp.swapaxes(w_in, 1, 2)

    x2 = x.reshape(batch * seq, d)
    for l in range(depth):
        h1 = _rmsnorm_cast(x2, norm_mix_pre[l])
        p_main = _mm_bias_t(h1, w_in_t, b_in[l, :main_w], layer=l, row0=0, n=main_w, tm=1024, tn=1024,
                            out_dtype=F32, name="in_proj_main")
        p_gate = _mm_bias_t(h1, w_in_t, b_in[l, gate_off:], layer=l, row0=gate_off, n=2 * d, tm=1024, tn=1024,
                            out_dtype=BF16, sigmoid=True, name="in_proj_gate")
        p_small = _mm_bias_t(h1, w_in_t, b_in[l, main_w:gate_off], layer=l, row0=main_w, n=2 * ML_HEADS,
                             tm=1024, tn=2 * ML_HEADS, out_dtype=F32, name="in_proj_small")

        y_a = _hgrn2(p_main, hg_lb_logits, hg_norm[l], batch=batch, seq=seq, layer=l)
        y_b = _mlstm(p_main, p_small, ml_conv_w[l], ml_conv_b[l], ml_norm[l], batch=batch, seq=seq)

        x2, h2 = _merge(y_a, y_b, p_gate, x2, w_up_a[l].astype(BF16), w_up_b[l].astype(BF16),
                        w_out[l].astype(BF16), norm_mix_post[l], norm_ffn_pre[l])

        act = _ffn_up(h2, ffn_w_gate, ffn_w_up, ffn_conv_w[l], ffn_conv_b[l], layer=l, seq=seq)
        x2 = _ffn_down(act, ffn_w_down[l].astype(BF16), x2, norm_ffn_post[l])
    return x2.reshape(batch, seq, d)
```

```python
import functools

import jax
import jax.numpy as jnp
from jax import lax
from jax.experimental import pallas as pl
from jax.experimental.pallas import tpu as pltpu

HG_HEADS = 8
HG_HEAD_DIM = 128
ML_HEADS = 4
ML_QK_DIM = 128
ML_V_DIM = 256
CHUNK = 64
EPS = 1e-6

F32 = jnp.float32
BF16 = jnp.bfloat16

VMEM_LIMIT_BYTES = 56 * 1024 * 1024
SUBLANES = 8


def _params(*sem):
    return pltpu.CompilerParams(dimension_semantics=sem, vmem_limit_bytes=VMEM_LIMIT_BYTES)


def _dot(a, b):
    return jnp.dot(a, b, preferred_element_type=F32)


def _dot_nt(a, b):
    return lax.dot_general(a, b, (((1,), (1,)), ((), ())), preferred_element_type=F32)


def _dot_tn(a, b):
    return lax.dot_general(a, b, (((0,), (0,)), ((), ())), preferred_element_type=F32)


def _rms(u, gain):
    return u * lax.rsqrt(jnp.mean(u * u, axis=-1, keepdims=True) + EPS) * gain


def _rmsnorm_kernel(x_ref, g_ref, o_ref):
    o_ref[...] = _rms(x_ref[...], g_ref[...]).astype(o_ref.dtype)


def _rmsnorm_cast(x2, gain, tm=512):
    m, d = x2.shape
    return pl.pallas_call(
        _rmsnorm_kernel,
        grid=(m // tm,),
        in_specs=[pl.BlockSpec((tm, d), lambda i: (i, 0)),
                  pl.BlockSpec((1, d), lambda i: (0, 0))],
        out_specs=pl.BlockSpec((tm, d), lambda i: (i, 0)),
        out_shape=jax.ShapeDtypeStruct((m, d), BF16),
        compiler_params=_params("parallel"),
        name="rmsnorm_cast",
    )(x2, gain.reshape(1, d))


def _mm_bias_kernel(h_ref, w_ref, b_ref, o_ref, wb_ref, *, sigmoid):
    @pl.when(pl.program_id(1) == 0)
    def _():
        wb_ref[...] = w_ref[...].astype(BF16)

    acc = _dot_nt(h_ref[...], wb_ref[...]) + b_ref[...]
    if sigmoid:
        acc = jax.nn.sigmoid(acc)
    o_ref[...] = acc.astype(o_ref.dtype)


def _mm_bias_t(h, wt, b, *, layer, row0, n, tm, tn, out_dtype, sigmoid=False, name):
    m, k = h.shape
    row0 += layer * wt.shape[1]
    wt = wt.reshape(-1, k)
    return pl.pallas_call(
        functools.partial(_mm_bias_kernel, sigmoid=sigmoid),
        grid=(n // tn, m // tm),
        in_specs=[pl.BlockSpec((tm, k), lambda j, i: (i, 0)),
                  pl.BlockSpec((pl.Element(tn), pl.Element(k)),
                               lambda j, i: (pl.multiple_of(row0 + j * tn, SUBLANES), 0)),
                  pl.BlockSpec((1, tn), lambda j, i: (0, j))],
        out_specs=pl.BlockSpec((tm, tn), lambda j, i: (i, j)),
        out_shape=jax.ShapeDtypeStruct((m, n), out_dtype),
        scratch_shapes=[pltpu.VMEM((tn, k), BF16)],
        compiler_params=_params("parallel", "arbitrary"),
        name=name,
    )(h, wt, b.reshape(1, n))


def _tril(n):
    r = lax.broadcasted_iota(jnp.int32, (n, n), 0)
    c = lax.broadcasted_iota(jnp.int32, (n, n), 1)
    return r >= c


def _chunk_cumsum(x):
    pos = lax.broadcasted_iota(jnp.int32, x.shape, 0) & (CHUNK - 1)
    s = 1
    while s < CHUNK:
        x = x + jnp.where(pos >= s, pltpu.roll(x, s, axis=0), 0.0)
        s *= 2
    return x


def _chunk_slices(rows):
    return [slice(c * CHUNK, (c + 1) * CHUNK) for c in range(rows // CHUNK)]


def _hgrn2_head(q_pre, f_pre, v_in, g_pre, lb, gain, st):
    q = jax.nn.silu(q_pre) * (HG_HEAD_DIM ** -0.5)
    f = lb + (1.0 - lb) * jax.nn.sigmoid(f_pre)
    k = 1.0 - f
    b = _chunk_cumsum(jnp.log(f))
    v = v_in.astype(BF16)
    causal = _tril(CHUNK)
    q_dec = (q * jnp.exp(b)).astype(BF16)
    k_inv = k * jnp.exp(-b)
    k_inv_b = k_inv.astype(BF16)

    chunks = _chunk_slices(q_pre.shape[0])
    decay = [jnp.exp(b[sl.stop - 1:sl.stop, :]) for sl in chunks]
    attn = [jnp.where(causal, _dot_nt(q_dec[sl], k_inv_b[sl]), 0.0).astype(BF16) for sl in chunks]
    kv = [_dot_tn(v[sl], (k_inv[sl] * d).astype(BF16)) for sl, d in zip(chunks, decay)]
    o_intra = [_dot(a, v[sl]) for a, sl in zip(attn, chunks)]
    outs = []
    for c, sl in enumerate(chunks):
        outs.append(o_intra[c] + _dot_nt(q_dec[sl], st.astype(BF16)))
        st = decay[c] * st + kv[c]
    o = jnp.concatenate(outs, axis=0)
    return _rms(o, gain) * jax.nn.silu(g_pre), st


def _mlstm_head(q, k, v, o_pre, ig_col, fg_col, gain, ct, m_prev):
    rows = q.shape[0]
    qb = q.astype(BF16)
    kb = k.astype(BF16)
    v_aug = jnp.concatenate([v, jnp.ones((rows, ML_QK_DIM), F32)], axis=1).astype(BF16)
    wide = (rows, ML_QK_DIM)
    ig = jnp.broadcast_to(ig_col, wide)
    b = _chunk_cumsum(jax.nn.log_sigmoid(jnp.broadcast_to(fg_col, wide)))
    igb = ig - b
    causal = _tril(CHUNK)
    chunks = _chunk_slices(rows)
    n_chunks = len(chunks)

    b_last = [b[sl.stop - 1:sl.stop, 0:1] for sl in chunks]
    log_d = [jnp.where(causal, b[sl, :CHUNK] + igb[sl].T[:CHUNK, :], -jnp.inf) for sl in chunks]
    m_intra = [jnp.max(ld, axis=1, keepdims=True) for ld in log_d]
    logw = [bl + igb[sl, 0:1] for bl, sl in zip(b_last, chunks)]
    logw_max = [jnp.max(lw, axis=0, keepdims=True) for lw in logw]
    qk = [_dot_nt(qb[sl], kb[sl]) for sl in chunks]

    m_in, m_out = [], []
    for c in range(n_chunks):
        m_in.append(m_prev)
        m_prev = jnp.maximum(b_last[c] + m_prev, logw_max[c])
        m_out.append(m_prev)

    m_inter = [b[sl, 0:1] + mi for sl, mi in zip(chunks, m_in)]
    m_tot = [jnp.maximum(a, bb) for a, bb in zip(m_inter, m_intra)]
    scores = [(qk[c] * jnp.exp(log_d[c] - m_tot[c])).astype(BF16) for c in range(n_chunks)]
    intra = [_dot(scores[c], v_aug[sl]) for c, sl in enumerate(chunks)]
    kv = [_dot_tn(v_aug[sl], (k[sl] * jnp.exp(logw[c] - m_out[c])).astype(BF16))
          for c, sl in enumerate(chunks)]
    outs = []
    for c, sl in enumerate(chunks):
        tot = intra[c] + jnp.exp(m_inter[c] - m_tot[c]) * _dot_nt(qb[sl], ct.astype(BF16))
        num = tot[:, :ML_V_DIM]
        den = tot[:, ML_V_DIM:ML_V_DIM + 1]
        outs.append(num / jnp.maximum(jnp.abs(den), jnp.exp(-m_tot[c])))
        ct = jnp.exp(b_last[c] + m_in[c] - m_out[c]) * ct + kv[c]
    hh = jnp.concatenate(outs, axis=0)
    return jax.nn.sigmoid(o_pre) * _rms(hh, gain), ct, m_prev


def _causal_conv(buf_ref, x, w_ref, b_ref):
    kw = w_ref.shape[0]
    rows = x.shape[0]
    buf_ref[SUBLANES:SUBLANES + rows, :] = x
    acc = b_ref[...] + w_ref[kw - 1:kw, :] * x
    for j in range(kw - 1):
        s = kw - 1 - j
        acc = acc + w_ref[j:j + 1, :] * buf_ref[SUBLANES - s:SUBLANES - s + rows, :]
    buf_ref[0:SUBLANES, :] = buf_ref[rows:rows + SUBLANES, :]
    return acc


def _mixer_kernel(pm_ref, ps_ref, lbl_ref, hgn_ref, cw_ref, cb_ref, mln_ref,
                  pg_ref, x_ref, wua_ref, wub_ref, wout_ref, gpost_ref, gpre_ref,
                  x1_ref, h2_ref,
                  st_ref, ct_ref, m_ref, qkbuf_ref, ya_ref, yb_ref, *, layer, tiles_per_seq, n_tiles):
    s = pl.program_id(0)
    r = jnp.minimum(s, n_tiles - 1)

    @pl.when(s == 0)
    def _():
        ya_ref[...] = jnp.zeros_like(ya_ref)
        yb_ref[...] = jnp.zeros_like(yb_ref)

    @pl.when(r % tiles_per_seq == 0)
    def _():
        st_ref[...] = jnp.zeros_like(st_ref)
        ct_ref[...] = jnp.zeros_like(ct_ref)
        m_ref[...] = jnp.zeros_like(m_ref)
        qkbuf_ref[0:SUBLANES, :] = jnp.zeros((SUBLANES, qkbuf_ref.shape[1]), F32)

    d = x_ref.shape[1]
    merged = (pg_ref[:, :d].astype(F32) * _dot(ya_ref[...], wua_ref[...])
              + pg_ref[:, d:].astype(F32) * _dot(yb_ref[...], wub_ref[...]))
    mix = _dot(merged.astype(BF16), wout_ref[...])
    x1 = x_ref[...] + _rms(mix, gpost_ref[...])
    x1_ref[...] = x1
    h2_ref[...] = _rms(x1, gpre_ref[...]).astype(h2_ref.dtype)

    hg_w = HG_HEADS * HG_HEAD_DIM
    logits = lbl_ref[...]
    e = jnp.exp(logits - jnp.max(logits, axis=0, keepdims=True))
    sm = e / jnp.sum(e, axis=0, keepdims=True)
    lb = jnp.sum(sm[:layer + 1], axis=0, keepdims=True)
    for h in range(HG_HEADS):
        cols = [slice(j * hg_w + h * HG_HEAD_DIM, j * hg_w + (h + 1) * HG_HEAD_DIM) for j in range(4)]
        hs = slice(h * HG_HEAD_DIM, (h + 1) * HG_HEAD_DIM)
        y, st = _hgrn2_head(pm_ref[:, cols[0]], pm_ref[:, cols[1]], pm_ref[:, cols[2]], pm_ref[:, cols[3]],
                            lb[:, hs], hgn_ref[:, hs], st_ref[h])
        st_ref[h] = st
        ya_ref[:, hs] = y.astype(ya_ref.dtype)

    qk_w = ML_HEADS * ML_QK_DIM
    qk_off = 4 * hg_w
    v_off = qk_off + 2 * qk_w
    o_off = v_off + ML_HEADS * ML_V_DIM
    qk = jax.nn.silu(_causal_conv(qkbuf_ref, pm_ref[:, qk_off:qk_off + 2 * qk_w], cw_ref, cb_ref))
    for h in range(ML_HEADS):
        qs = slice(h * ML_QK_DIM, (h + 1) * ML_QK_DIM)
        ks = slice(qk_w + h * ML_QK_DIM, qk_w + (h + 1) * ML_QK_DIM)
        vs = slice(h * ML_V_DIM, (h + 1) * ML_V_DIM)
        y, ct, m_new = _mlstm_head(qk[:, qs], qk[:, ks] * (ML_QK_DIM ** -0.5),
                                   pm_ref[:, v_off + vs.start:v_off + vs.stop],
                                   pm_ref[:, o_off + vs.start:o_off + vs.stop],
                                   ps_ref[:, h:h + 1], ps_ref[:, ML_HEADS + h:ML_HEADS + h + 1],
                                   mln_ref[:, vs], ct_ref[h], m_ref[h, 0:1, 0:1])
        ct_ref[h] = ct
        m_ref[h] = jnp.broadcast_to(m_new, m_ref.shape[1:])
        yb_ref[:, vs] = y.astype(yb_ref.dtype)


def _mixer(p_main, p_small, p_gate, x2, lb_logits, hg_gain, conv_w, conv_b, ml_gain, w_ua, w_ub, w_out,
           g_post, g_pre, *, seq, layer, tm=256):
    m, d = x2.shape
    n_tiles = m // tm
    hg_w = HG_HEADS * HG_HEAD_DIM
    qk_w = ML_HEADS * ML_QK_DIM
    v_w = ML_HEADS * ML_V_DIM
    const = lambda s: (0, 0)
    cur = lambda s: (jnp.minimum(s, n_tiles - 1), 0)
    prev = lambda s: (jnp.maximum(s - 1, 0), 0)
    single = pl.Buffered(1)
    return pl.pallas_call(
        functools.partial(_mixer_kernel, layer=layer, tiles_per_seq=seq // tm, n_tiles=n_tiles),
        grid=(n_tiles + 1,),
        in_specs=[pl.BlockSpec((tm, p_main.shape[1]), cur),
                  pl.BlockSpec((tm, p_small.shape[1]), cur),
                  pl.BlockSpec(lb_logits.shape, const),
                  pl.BlockSpec((1, hg_w), const),
                  pl.BlockSpec(conv_w.shape, const),
                  pl.BlockSpec((1, 2 * qk_w), const),
                  pl.BlockSpec((1, v_w), const),
                  pl.BlockSpec((tm, 2 * d), prev),
                  pl.BlockSpec((tm, d), prev),
                  pl.BlockSpec((hg_w, d), const, pipeline_mode=single),
                  pl.BlockSpec((v_w, d), const, pipeline_mode=single),
                  pl.BlockSpec((d, d), const, pipeline_mode=single),
                  pl.BlockSpec((1, d), const),
                  pl.BlockSpec((1, d), const)],
        out_specs=[pl.BlockSpec((tm, d), prev),
                   pl.BlockSpec((tm, d), prev)],
        out_shape=[jax.ShapeDtypeStruct((m, d), F32),
                   jax.ShapeDtypeStruct((m, d), BF16)],
        scratch_shapes=[pltpu.VMEM((HG_HEADS, HG_HEAD_DIM, HG_HEAD_DIM), F32),
                        pltpu.VMEM((ML_HEADS, ML_V_DIM + ML_QK_DIM, ML_QK_DIM), F32),
                        pltpu.VMEM((ML_HEADS, SUBLANES, 128), F32),
                        pltpu.VMEM((tm + SUBLANES, 2 * qk_w), F32),
                        pltpu.VMEM((tm, hg_w), BF16),
                        pltpu.VMEM((tm, v_w), BF16)],
        compiler_params=_params("arbitrary"),
        name="mixer",
    )(p_main, p_small, lb_logits, hg_gain.reshape(1, hg_w), conv_w, conv_b.reshape(1, 2 * qk_w),
      ml_gain.reshape(1, v_w), p_gate, x2, w_ua, w_ub, w_out, g_post.reshape(1, d), g_pre.reshape(1, d))


def _ffn_up_kernel(h_ref, wg_ref, wu_ref, cw_ref, cb_ref, a_ref, gbuf_ref, wgb_ref, wub_ref, *, tm, tiles_per_seq):
    kw = cw_ref.shape[0]
    h = h_ref[...]

    @pl.when(pl.program_id(1) == 0)
    def _():
        wgb_ref[...] = wg_ref[...].astype(BF16)
        wub_ref[...] = wu_ref[...].astype(BF16)

    @pl.when(pl.program_id(1) % tiles_per_seq == 0)
    def _():
        gbuf_ref[0:SUBLANES, :] = jnp.zeros((SUBLANES, gbuf_ref.shape[1]), F32)

    gbuf_ref[SUBLANES:SUBLANES + tm, :] = _dot(h, wgb_ref[...])
    acc = cb_ref[...] + cw_ref[kw - 1:kw, :] * gbuf_ref[SUBLANES:SUBLANES + tm, :]
    for j in range(kw - 1):
        s = kw - 1 - j
        acc = acc + cw_ref[j:j + 1, :] * gbuf_ref[SUBLANES - s:SUBLANES - s + tm, :]
    gbuf_ref[0:SUBLANES, :] = gbuf_ref[tm:tm + SUBLANES, :]
    a_ref[...] = (jax.nn.gelu(acc, approximate=True) * _dot(h, wub_ref[...])).astype(a_ref.dtype)


def _ffn_up(h2, w_gate, w_up, conv_w, conv_b, *, layer, seq, tm=1024, tf=512):
    m, d = h2.shape
    f = w_gate.shape[2]
    kw = conv_w.shape[0]
    wspec = pl.BlockSpec((None, d, tf), lambda j, i: (layer, 0, j))
    return pl.pallas_call(
        functools.partial(_ffn_up_kernel, tm=tm, tiles_per_seq=seq // tm),
        grid=(f // tf, m // tm),
        in_specs=[pl.BlockSpec((tm, d), lambda j, i: (i, 0)),
                  wspec, wspec,
                  pl.BlockSpec((kw, tf), lambda j, i: (0, j)),
                  pl.BlockSpec((1, tf), lambda j, i: (0, j))],
        out_specs=pl.BlockSpec((tm, tf), lambda j, i: (i, j)),
        out_shape=jax.ShapeDtypeStruct((m, f), BF16),
        scratch_shapes=[pltpu.VMEM((tm + SUBLANES, tf), F32),
                        pltpu.VMEM((d, tf), BF16),
                        pltpu.VMEM((d, tf), BF16)],
        compiler_params=_params("arbitrary", "arbitrary"),
        name="ffn_up",
    )(h2, w_gate, w_up, conv_w, conv_b.reshape(1, f))


def _ffn_down_kernel(a_ref, wd_ref, x1_ref, g_ref, o_ref):
    kk = pl.program_id(1)

    @pl.when(kk == 0)
    def _():
        o_ref[...] = jnp.zeros_like(o_ref)

    o_ref[...] += _dot(a_ref[...], wd_ref[...])

    @pl.when(kk == pl.num_programs(1) - 1)
    def _():
        o_ref[...] = x1_ref[...] + _rms(o_ref[...], g_ref[...])


def _ffn_down(act, w_down, x1, gain, tm=1024, tk=1408):
    m, f = act.shape
    d = w_down.shape[1]
    return pl.pallas_call(
        _ffn_down_kernel,
        grid=(m // tm, f // tk),
        in_specs=[pl.BlockSpec((tm, tk), lambda i, k: (i, k)),
                  pl.BlockSpec((tk, d), lambda i, k: (k, 0)),
                  pl.BlockSpec((tm, d), lambda i, k: (i, 0), pipeline_mode=pl.Buffered(1)),
                  pl.BlockSpec((1, d), lambda i, k: (0, 0))],
        out_specs=pl.BlockSpec((tm, d), lambda i, k: (i, 0)),
        out_shape=jax.ShapeDtypeStruct((m, d), F32),
        compiler_params=_params("parallel", "arbitrary"),
        name="ffn_down",
    )(act, w_down, x1, gain.reshape(1, d))


def kernel(x, norm_mix_pre, norm_mix_post, norm_ffn_pre, norm_ffn_post, w_in, b_in, hg_lb_logits, hg_norm,
           ml_conv_w, ml_conv_b, ml_norm, w_up_a, w_up_b, w_out, ffn_w_gate, ffn_w_up, ffn_conv_w, ffn_conv_b,
           ffn_w_down):
    batch, seq, d = x.shape
    depth = w_in.shape[0]
    hg_w = HG_HEADS * HG_HEAD_DIM
    qk_w = ML_HEADS * ML_QK_DIM
    v_w = ML_HEADS * ML_V_DIM
    main_w = 4 * hg_w + 2 * qk_w + 2 * v_w
    gate_off = main_w + 2 * ML_HEADS
    w_in_t = jnp.swapaxes(w_in, 1, 2)

    x2 = x.reshape(batch * seq, d)
    for l in range(depth):
        h1 = _rmsnorm_cast(x2, norm_mix_pre[l])
        p_main = _mm_bias_t(h1, w_in_t, b_in[l, :main_w], layer=l, row0=0, n=main_w, tm=1024, tn=1024,
                            out_dtype=F32, name="in_proj_main")
        p_gate = _mm_bias_t(h1, w_in_t, b_in[l, gate_off:], layer=l, row0=gate_off, n=2 * d, tm=1024, tn=1024,
                            out_dtype=BF16, sigmoid=True, name="in_proj_gate")
        p_small = _mm_bias_t(h1, w_in_t, b_in[l, main_w:gate_off], layer=l, row0=main_w, n=2 * ML_HEADS,
                             tm=1024, tn=2 * ML_HEADS, out_dtype=F32, name="in_proj_small")

        x2, h2 = _mixer(p_main, p_small, p_gate, x2, hg_lb_logits, hg_norm[l], ml_conv_w[l], ml_conv_b[l],
                        ml_norm[l], w_up_a[l].astype(BF16), w_up_b[l].astype(BF16), w_out[l].astype(BF16),
                        norm_mix_post[l], norm_ffn_pre[l], seq=seq, layer=l)

        act = _ffn_up(h2, ffn_w_gate, ffn_w_up, ffn_conv_w[l], ffn_conv_b[l], layer=l, seq=seq)
        x2 = _ffn_down(act, ffn_w_down[l].astype(BF16), x2, norm_ffn_post[l])
    return x2.reshape(batch, seq, d)
```

```python
import functools

import jax
import jax.numpy as jnp
from jax import lax
from jax.experimental import pallas as pl
from jax.experimental.pallas import tpu as pltpu

HG_HEADS = 8
HG_HEAD_DIM = 128
ML_HEADS = 4
ML_QK_DIM = 128
ML_V_DIM = 256
CHUNK = 64
EPS = 1e-6

F32 = jnp.float32
BF16 = jnp.bfloat16

VMEM_LIMIT_BYTES = 56 * 1024 * 1024
SUBLANES = 8


def _params(*sem):
    return pltpu.CompilerParams(dimension_semantics=sem, vmem_limit_bytes=VMEM_LIMIT_BYTES)


def _dot(a, b):
    return jnp.dot(a, b, preferred_element_type=F32)


def _dot_nt(a, b):
    return lax.dot_general(a, b, (((1,), (1,)), ((), ())), preferred_element_type=F32)


def _dot_tn(a, b):
    return lax.dot_general(a, b, (((0,), (0,)), ((), ())), preferred_element_type=F32)


def _rms(u, gain):
    return u * lax.rsqrt(jnp.mean(u * u, axis=-1, keepdims=True) + EPS) * gain


def _rmsnorm_kernel(x_ref, g_ref, o_ref):
    o_ref[...] = _rms(x_ref[...], g_ref[...]).astype(o_ref.dtype)


def _rmsnorm_cast(x2, gain, tm=512):
    m, d = x2.shape
    return pl.pallas_call(
        _rmsnorm_kernel,
        grid=(m // tm,),
        in_specs=[pl.BlockSpec((tm, d), lambda i: (i, 0)),
                  pl.BlockSpec((1, d), lambda i: (0, 0))],
        out_specs=pl.BlockSpec((tm, d), lambda i: (i, 0)),
        out_shape=jax.ShapeDtypeStruct((m, d), BF16),
        compiler_params=_params("parallel"),
        name="rmsnorm_cast",
    )(x2, gain.reshape(1, d))


def _mm_bias_kernel(h_ref, w_ref, b_ref, o_ref, wb_ref, *, sigmoid):
    @pl.when(pl.program_id(1) == 0)
    def _():
        wb_ref[...] = w_ref[...].astype(BF16)

    acc = _dot_nt(h_ref[...], wb_ref[...]) + b_ref[...]
    if sigmoid:
        acc = jax.nn.sigmoid(acc)
    o_ref[...] = acc.astype(o_ref.dtype)


def _mm_bias_t(h, wt, b, *, layer, row0, n, tm, tn, out_dtype, sigmoid=False, name):
    m, k = h.shape
    row0 += layer * wt.shape[1]
    wt = wt.reshape(-1, k)
    return pl.pallas_call(
        functools.partial(_mm_bias_kernel, sigmoid=sigmoid),
        grid=(n // tn, m // tm),
        in_specs=[pl.BlockSpec((tm, k), lambda j, i: (i, 0)),
                  pl.BlockSpec((pl.Element(tn), pl.Element(k)),
                               lambda j, i: (pl.multiple_of(row0 + j * tn, SUBLANES), 0)),
                  pl.BlockSpec((1, tn), lambda j, i: (0, j))],
        out_specs=pl.BlockSpec((tm, tn), lambda j, i: (i, j)),
        out_shape=jax.ShapeDtypeStruct((m, n), out_dtype),
        scratch_shapes=[pltpu.VMEM((tn, k), BF16)],
        compiler_params=_params("parallel", "arbitrary"),
        name=name,
    )(h, wt, b.reshape(1, n))


def _tril(n):
    r = lax.broadcasted_iota(jnp.int32, (n, n), 0)
    c = lax.broadcasted_iota(jnp.int32, (n, n), 1)
    return r >= c


def _chunk_cumsum(x):
    pos = lax.broadcasted_iota(jnp.int32, x.shape, 0) & (CHUNK - 1)
    s = 1
    while s < CHUNK:
        x = x + jnp.where(pos >= s, pltpu.roll(x, s, axis=0), 0.0)
        s *= 2
    return x


def _chunk_slices(rows):
    return [slice(c * CHUNK, (c + 1) * CHUNK) for c in range(rows // CHUNK)]


def _hgrn2_head(q_pre, f_pre, v_in, g_pre, lb, gain, st):
    q = jax.nn.silu(q_pre) * (HG_HEAD_DIM ** -0.5)
    f = lb + (1.0 - lb) * jax.nn.sigmoid(f_pre)
    k = 1.0 - f
    b = _chunk_cumsum(jnp.log(f))
    v = v_in.astype(BF16)
    causal = _tril(CHUNK)
    q_dec = (q * jnp.exp(b)).astype(BF16)
    k_inv = k * jnp.exp(-b)
    k_inv_b = k_inv.astype(BF16)

    chunks = _chunk_slices(q_pre.shape[0])
    decay = [jnp.exp(b[sl.stop - 1:sl.stop, :]) for sl in chunks]
    attn = [jnp.where(causal, _dot_nt(q_dec[sl], k_inv_b[sl]), 0.0).astype(BF16) for sl in chunks]
    kv = [_dot_tn(v[sl], (k_inv[sl] * d).astype(BF16)) for sl, d in zip(chunks, decay)]
    o_intra = [_dot(a, v[sl]) for a, sl in zip(attn, chunks)]
    outs = []
    for c, sl in enumerate(chunks):
        outs.append(o_intra[c] + _dot_nt(q_dec[sl], st.astype(BF16)))
        st = decay[c] * st + kv[c]
    o = jnp.concatenate(outs, axis=0)
    return _rms(o, gain) * jax.nn.silu(g_pre), st


def _mlstm_head(q, k, v, o_pre, ig_col, b_col, gain, ct, m_prev):
    rows = q.shape[0]
    qb = q.astype(BF16)
    kb = k.astype(BF16)
    v_aug = jnp.concatenate([v, jnp.ones((rows, ML_QK_DIM), F32)], axis=1).astype(BF16)
    wide = (rows, ML_QK_DIM)
    b = jnp.broadcast_to(b_col, wide)
    igb = jnp.broadcast_to(ig_col - b_col, wide)
    causal = _tril(CHUNK)
    chunks = _chunk_slices(rows)
    n_chunks = len(chunks)

    b_last = [b[sl.stop - 1:sl.stop, 0:1] for sl in chunks]
    log_d = [jnp.where(causal, b[sl, :CHUNK] + igb[sl].T[:CHUNK, :], -jnp.inf) for sl in chunks]
    m_intra = [jnp.max(ld, axis=1, keepdims=True) for ld in log_d]
    logw = [bl + igb[sl, 0:1] for bl, sl in zip(b_last, chunks)]
    logw_max = [jnp.max(lw, axis=0, keepdims=True) for lw in logw]
    qk = [_dot_nt(qb[sl], kb[sl]) for sl in chunks]

    m_in, m_out = [], []
    for c in range(n_chunks):
        m_in.append(m_prev)
        m_prev = jnp.maximum(b_last[c] + m_prev, logw_max[c])
        m_out.append(m_prev)

    m_inter = [b[sl, 0:1] + mi for sl, mi in zip(chunks, m_in)]
    m_tot = [jnp.maximum(a, bb) for a, bb in zip(m_inter, m_intra)]
    scores = [(qk[c] * jnp.exp(log_d[c] - m_tot[c])).astype(BF16) for c in range(n_chunks)]
    intra = [_dot(scores[c], v_aug[sl]) for c, sl in enumerate(chunks)]
    kv = [_dot_tn(v_aug[sl], (k[sl] * jnp.exp(logw[c] - m_out[c])).astype(BF16))
          for c, sl in enumerate(chunks)]
    outs = []
    for c, sl in enumerate(chunks):
        tot = intra[c] + jnp.exp(m_inter[c] - m_tot[c]) * _dot_nt(qb[sl], ct.astype(BF16))
        num = tot[:, :ML_V_DIM]
        den = tot[:, ML_V_DIM:ML_V_DIM + 1]
        outs.append(num / jnp.maximum(jnp.abs(den), jnp.exp(-m_tot[c])))
        ct = jnp.exp(b_last[c] + m_in[c] - m_out[c]) * ct + kv[c]
    hh = jnp.concatenate(outs, axis=0)
    return jax.nn.sigmoid(o_pre) * _rms(hh, gain), ct, m_prev


def _causal_conv(buf_ref, x, w_ref, b_ref):
    kw = w_ref.shape[0]
    rows = x.shape[0]
    buf_ref[SUBLANES:SUBLANES + rows, :] = x
    acc = b_ref[...] + w_ref[kw - 1:kw, :] * x
    for j in range(kw - 1):
        s = kw - 1 - j
        acc = acc + w_ref[j:j + 1, :] * buf_ref[SUBLANES - s:SUBLANES - s + rows, :]
    buf_ref[0:SUBLANES, :] = buf_ref[rows:rows + SUBLANES, :]
    return acc


def _mixer_kernel(pm_ref, ps_ref, lbl_ref, hgn_ref, cw_ref, cb_ref, mln_ref,
                  pg_ref, x_ref, wua_ref, wub_ref, wout_ref, gpost_ref, gpre_ref,
                  x1_ref, h2_ref,
                  st_ref, ct_ref, m_ref, qkbuf_ref, ya_ref, yb_ref, *, layer, tiles_per_seq, n_tiles):
    s = pl.program_id(0)
    r = jnp.minimum(s, n_tiles - 1)

    @pl.when(s == 0)
    def _():
        ya_ref[...] = jnp.zeros_like(ya_ref)
        yb_ref[...] = jnp.zeros_like(yb_ref)

    @pl.when(r % tiles_per_seq == 0)
    def _():
        st_ref[...] = jnp.zeros_like(st_ref)
        ct_ref[...] = jnp.zeros_like(ct_ref)
        m_ref[...] = jnp.zeros_like(m_ref)
        qkbuf_ref[0:SUBLANES, :] = jnp.zeros((SUBLANES, qkbuf_ref.shape[1]), F32)

    d = x_ref.shape[1]
    hg_w = HG_HEADS * HG_HEAD_DIM
    qk_w = ML_HEADS * ML_QK_DIM
    qk_off = 4 * hg_w
    v_off = qk_off + 2 * qk_w
    o_off = v_off + ML_HEADS * ML_V_DIM
    n_pieces = ML_HEADS
    pw = d // n_pieces

    def lift_piece(j):
        cs = slice(j * pw, (j + 1) * pw)
        return (pg_ref[:, cs] * _dot(ya_ref[...], wua_ref[:, cs]).astype(BF16)
                + pg_ref[:, d + j * pw:d + (j + 1) * pw] * _dot(yb_ref[...], wub_ref[:, cs]).astype(BF16))

    def hgrn2_piece(h):
        cols = [slice(j * hg_w + h * HG_HEAD_DIM, j * hg_w + (h + 1) * HG_HEAD_DIM) for j in range(4)]
        hs = slice(h * HG_HEAD_DIM, (h + 1) * HG_HEAD_DIM)
        y, st = _hgrn2_head(pm_ref[:, cols[0]], pm_ref[:, cols[1]], pm_ref[:, cols[2]], pm_ref[:, cols[3]],
                            lb[:, hs], hgn_ref[:, hs], st_ref[h])
        st_ref[h] = st
        return y.astype(ya_ref.dtype)

    def mlstm_piece(h):
        qs = slice(h * ML_QK_DIM, (h + 1) * ML_QK_DIM)
        ks = slice(qk_w + h * ML_QK_DIM, qk_w + (h + 1) * ML_QK_DIM)
        vs = slice(h * ML_V_DIM, (h + 1) * ML_V_DIM)
        y, ct, m_new = _mlstm_head(qk[:, qs], qk[:, ks] * (ML_QK_DIM ** -0.5),
                                   pm_ref[:, v_off + vs.start:v_off + vs.stop],
                                   pm_ref[:, o_off + vs.start:o_off + vs.stop],
                                   gates[:, h:h + 1], gates_b[:, ML_HEADS + h:ML_HEADS + h + 1],
                                   mln_ref[:, vs], ct_ref[h], m_ref[h, 0:1, 0:1])
        ct_ref[h] = ct
        m_ref[h] = jnp.broadcast_to(m_new, m_ref.shape[1:])
        return y.astype(yb_ref.dtype)

    logits = lbl_ref[...]
    e = jnp.exp(logits - jnp.max(logits, axis=0, keepdims=True))
    sm = e / jnp.sum(e, axis=0, keepdims=True)
    lb = jnp.sum(sm[:layer + 1], axis=0, keepdims=True)
    qk = jax.nn.silu(_causal_conv(qkbuf_ref, pm_ref[:, qk_off:qk_off + 2 * qk_w], cw_ref, cb_ref))
    gates = ps_ref[...]
    gates_b = _chunk_cumsum(jax.nn.log_sigmoid(gates))

    hg_per_piece = HG_HEADS // n_pieces
    merged, ya_new, yb_new, mix = [], [], [], []
    for j in range(n_pieces):
        merged.append(lift_piece(j))
        ya_new += [hgrn2_piece(h) for h in range(j * hg_per_piece, (j + 1) * hg_per_piece)]
    merged = jnp.concatenate(merged, axis=1)
    for j in range(n_pieces):
        mix.append(_dot(merged, wout_ref[:, j * pw:(j + 1) * pw]))
        yb_new.append(mlstm_piece(j))
    x1 = x_ref[...] + _rms(jnp.concatenate(mix, axis=1), gpost_ref[...])
    x1_ref[...] = x1
    h2_ref[...] = _rms(x1, gpre_ref[...]).astype(h2_ref.dtype)
    ya_ref[...] = jnp.concatenate(ya_new, axis=1)
    yb_ref[...] = jnp.concatenate(yb_new, axis=1)


def _mixer(p_main, p_small, p_gate, x2, lb_logits, hg_gain, conv_w, conv_b, ml_gain, w_ua, w_ub, w_out,
           g_post, g_pre, *, seq, layer, tm=256):
    m, d = x2.shape
    n_tiles = m // tm
    hg_w = HG_HEADS * HG_HEAD_DIM
    qk_w = ML_HEADS * ML_QK_DIM
    v_w = ML_HEADS * ML_V_DIM
    const = lambda s: (0, 0)
    cur = lambda s: (jnp.minimum(s, n_tiles - 1), 0)
    prev = lambda s: (jnp.maximum(s - 1, 0), 0)
    single = pl.Buffered(1)
    return pl.pallas_call(
        functools.partial(_mixer_kernel, layer=layer, tiles_per_seq=seq // tm, n_tiles=n_tiles),
        grid=(n_tiles + 1,),
        in_specs=[pl.BlockSpec((tm, p_main.shape[1]), cur),
                  pl.BlockSpec((tm, p_small.shape[1]), cur),
                  pl.BlockSpec(lb_logits.shape, const),
                  pl.BlockSpec((1, hg_w), const),
                  pl.BlockSpec(conv_w.shape, const),
                  pl.BlockSpec((1, 2 * qk_w), const),
                  pl.BlockSpec((1, v_w), const),
                  pl.BlockSpec((tm, 2 * d), prev),
                  pl.BlockSpec((tm, d), prev),
                  pl.BlockSpec((hg_w, d), const, pipeline_mode=single),
                  pl.BlockSpec((v_w, d), const, pipeline_mode=single),
                  pl.BlockSpec((d, d), const, pipeline_mode=single),
                  pl.BlockSpec((1, d), const),
                  pl.BlockSpec((1, d), const)],
        out_specs=[pl.BlockSpec((tm, d), prev),
                   pl.BlockSpec((tm, d), prev)],
        out_shape=[jax.ShapeDtypeStruct((m, d), F32),
                   jax.ShapeDtypeStruct((m, d), BF16)],
        scratch_shapes=[pltpu.VMEM((HG_HEADS, HG_HEAD_DIM, HG_HEAD_DIM), F32),
                        pltpu.VMEM((ML_HEADS, ML_V_DIM + ML_QK_DIM, ML_QK_DIM), F32),
                        pltpu.VMEM((ML_HEADS, SUBLANES, 128), F32),
                        pltpu.VMEM((tm + SUBLANES, 2 * qk_w), F32),
                        pltpu.VMEM((tm, hg_w), BF16),
                        pltpu.VMEM((tm, v_w), BF16)],
        compiler_params=_params("arbitrary"),
        name="mixer",
    )(p_main, p_small, lb_logits, hg_gain.reshape(1, hg_w), conv_w, conv_b.reshape(1, 2 * qk_w),
      ml_gain.reshape(1, v_w), p_gate, x2, w_ua, w_ub, w_out, g_post.reshape(1, d), g_pre.reshape(1, d))


def _ffn_up_kernel(h_ref, wg_ref, wu_ref, cw_ref, cb_ref, a_ref, gbuf_ref, wgb_ref, wub_ref, *, tm, tiles_per_seq):
    kw = cw_ref.shape[0]
    h = h_ref[...]

    @pl.when(pl.program_id(1) == 0)
    def _():
        wgb_ref[...] = wg_ref[...].astype(BF16)
        wub_ref[...] = wu_ref[...].astype(BF16)

    @pl.when(pl.program_id(1) % tiles_per_seq == 0)
    def _():
        gbuf_ref[0:SUBLANES, :] = jnp.zeros((SUBLANES, gbuf_ref.shape[1]), F32)

    gbuf_ref[SUBLANES:SUBLANES + tm, :] = _dot(h, wgb_ref[...])
    acc = cb_ref[...] + cw_ref[kw - 1:kw, :] * gbuf_ref[SUBLANES:SUBLANES + tm, :]
    for j in range(kw - 1):
        s = kw - 1 - j
        acc = acc + cw_ref[j:j + 1, :] * gbuf_ref[SUBLANES - s:SUBLANES - s + tm, :]
    gbuf_ref[0:SUBLANES, :] = gbuf_ref[tm:tm + SUBLANES, :]
    a_ref[...] = (jax.nn.gelu(acc, approximate=True) * _dot(h, wub_ref[...])).astype(a_ref.dtype)


def _ffn_up(h2, w_gate, w_up, conv_w, conv_b, *, layer, seq, tm=1024, tf=512):
    m, d = h2.shape
    f = w_gate.shape[2]
    kw = conv_w.shape[0]
    wspec = pl.BlockSpec((None, d, tf), lambda j, i: (layer, 0, j))
    return pl.pallas_call(
        functools.partial(_ffn_up_kernel, tm=tm, tiles_per_seq=seq // tm),
        grid=(f // tf, m // tm),
        in_specs=[pl.BlockSpec((tm, d), lambda j, i: (i, 0)),
                  wspec, wspec,
                  pl.BlockSpec((kw, tf), lambda j, i: (0, j)),
                  pl.BlockSpec((1, tf), lambda j, i: (0, j))],
        out_specs=pl.BlockSpec((tm, tf), lambda j, i: (i, j)),
        out_shape=jax.ShapeDtypeStruct((m, f), BF16),
        scratch_shapes=[pltpu.VMEM((tm + SUBLANES, tf), F32),
                        pltpu.VMEM((d, tf), BF16),
                        pltpu.VMEM((d, tf), BF16)],
        compiler_params=_params("arbitrary", "arbitrary"),
        name="ffn_up",
    )(h2, w_gate, w_up, conv_w, conv_b.reshape(1, f))


def _ffn_down_kernel(a_ref, wd_ref, x1_ref, g_ref, o_ref):
    kk = pl.program_id(1)

    @pl.when(kk == 0)
    def _():
        o_ref[...] = jnp.zeros_like(o_ref)

    o_ref[...] += _dot(a_ref[...], wd_ref[...])

    @pl.when(kk == pl.num_programs(1) - 1)
    def _():
        o_ref[...] = x1_ref[...] + _rms(o_ref[...], g_ref[...])


def _ffn_down(act, w_down, x1, gain, tm=512, tk=1408):
    m, f = act.shape
    d = w_down.shape[1]
    return pl.pallas_call(
        _ffn_down_kernel,
        grid=(m // tm, f // tk),
        in_specs=[pl.BlockSpec((tm, tk), lambda i, k: (i, k)),
                  pl.BlockSpec((tk, d), lambda i, k: (k, 0)),
                  pl.BlockSpec((tm, d), lambda i, k: (i, 0)),
                  pl.BlockSpec((1, d), lambda i, k: (0, 0))],
        out_specs=pl.BlockSpec((tm, d), lambda i, k: (i, 0)),
        out_shape=jax.ShapeDtypeStruct((m, d), F32),
        compiler_params=_params("parallel", "arbitrary"),
        name="ffn_down",
    )(act, w_down, x1, gain.reshape(1, d))


def kernel(x, norm_mix_pre, norm_mix_post, norm_ffn_pre, norm_ffn_post, w_in, b_in, hg_lb_logits, hg_norm,
           ml_conv_w, ml_conv_b, ml_norm, w_up_a, w_up_b, w_out, ffn_w_gate, ffn_w_up, ffn_conv_w, ffn_conv_b,
           ffn_w_down):
    batch, seq, d = x.shape
    depth = w_in.shape[0]
    hg_w = HG_HEADS * HG_HEAD_DIM
    qk_w = ML_HEADS * ML_QK_DIM
    v_w = ML_HEADS * ML_V_DIM
    main_w = 4 * hg_w + 2 * qk_w + 2 * v_w
    gate_off = main_w + 2 * ML_HEADS
    w_in_t = jnp.swapaxes(w_in, 1, 2)

    x2 = x.reshape(batch * seq, d)
    for l in range(depth):
        h1 = _rmsnorm_cast(x2, norm_mix_pre[l])
        p_main = _mm_bias_t(h1, w_in_t, b_in[l, :main_w], layer=l, row0=0, n=main_w, tm=1024, tn=1024,
                            out_dtype=F32, name="in_proj_main")
        p_gate = _mm_bias_t(h1, w_in_t, b_in[l, gate_off:], layer=l, row0=gate_off, n=2 * d, tm=1024, tn=1024,
                            out_dtype=BF16, sigmoid=True, name="in_proj_gate")
        p_small = _mm_bias_t(h1, w_in_t, b_in[l, main_w:gate_off], layer=l, row0=main_w, n=2 * ML_HEADS,
                             tm=1024, tn=2 * ML_HEADS, out_dtype=F32, name="in_proj_small")

        x2, h2 = _mixer(p_main, p_small, p_gate, x2, hg_lb_logits, hg_norm[l], ml_conv_w[l], ml_conv_b[l],
                        ml_norm[l], w_up_a[l].astype(BF16), w_up_b[l].astype(BF16), w_out[l].astype(BF16),
                        norm_mix_post[l], norm_ffn_pre[l], seq=seq, layer=l)

        act = _ffn_up(h2, ffn_w_gate, ffn_w_up, ffn_conv_w[l], ffn_conv_b[l], layer=l, seq=seq)
        x2 = _ffn_down(act, ffn_w_down[l].astype(BF16), x2, norm_ffn_post[l])
    return x2.reshape(batch, seq, d)
```

```python
import functools

import jax
import jax.numpy as jnp
from jax import lax
from jax.experimental import pallas as pl
from jax.experimental.pallas import tpu as pltpu

HG_HEADS = 8
HG_HEAD_DIM = 128
ML_HEADS = 4
ML_QK_DIM = 128
ML_V_DIM = 256
CHUNK = 64
EPS = 1e-6

F32 = jnp.float32
BF16 = jnp.bfloat16

VMEM_LIMIT_BYTES = 56 * 1024 * 1024
SUBLANES = 8


def _params(*sem):
    return pltpu.CompilerParams(dimension_semantics=sem, vmem_limit_bytes=VMEM_LIMIT_BYTES)


def _dot(a, b):
    return jnp.dot(a, b, preferred_element_type=F32)


def _dot_nt(a, b):
    return lax.dot_general(a, b, (((1,), (1,)), ((), ())), preferred_element_type=F32)


def _dot_tn(a, b):
    return lax.dot_general(a, b, (((0,), (0,)), ((), ())), preferred_element_type=F32)


def _rms(u, gain):
    return u * lax.rsqrt(jnp.mean(u * u, axis=-1, keepdims=True) + EPS) * gain


def _rmsnorm_kernel(x_ref, g_ref, o_ref):
    o_ref[...] = _rms(x_ref[...], g_ref[...]).astype(o_ref.dtype)


def _rmsnorm_cast(x2, gain, tm=512):
    m, d = x2.shape
    return pl.pallas_call(
        _rmsnorm_kernel,
        grid=(m // tm,),
        in_specs=[pl.BlockSpec((tm, d), lambda i: (i, 0)),
                  pl.BlockSpec((1, d), lambda i: (0, 0))],
        out_specs=pl.BlockSpec((tm, d), lambda i: (i, 0)),
        out_shape=jax.ShapeDtypeStruct((m, d), BF16),
        compiler_params=_params("parallel"),
        name="rmsnorm_cast",
    )(x2, gain.reshape(1, d))


def _mm_bias_kernel(h_ref, w_ref, b_ref, o_ref, wb_ref, *, sigmoid):
    @pl.when(pl.program_id(1) == 0)
    def _():
        wb_ref[...] = w_ref[...].astype(BF16)

    acc = _dot_nt(h_ref[...], wb_ref[...]) + b_ref[...]
    if sigmoid:
        acc = jax.nn.sigmoid(acc)
    o_ref[...] = acc.astype(o_ref.dtype)


def _mm_bias_t(h, wt, b, *, layer, row0, n, tm, tn, out_dtype, sigmoid=False, name):
    m, k = h.shape
    row0 += layer * wt.shape[1]
    wt = wt.reshape(-1, k)
    return pl.pallas_call(
        functools.partial(_mm_bias_kernel, sigmoid=sigmoid),
        grid=(n // tn, m // tm),
        in_specs=[pl.BlockSpec((tm, k), lambda j, i: (i, 0)),
                  pl.BlockSpec((pl.Element(tn), pl.Element(k)),
                               lambda j, i: (pl.multiple_of(row0 + j * tn, SUBLANES), 0)),
                  pl.BlockSpec((1, tn), lambda j, i: (0, j))],
        out_specs=pl.BlockSpec((tm, tn), lambda j, i: (i, j)),
        out_shape=jax.ShapeDtypeStruct((m, n), out_dtype),
        scratch_shapes=[pltpu.VMEM((tn, k), BF16)],
        compiler_params=_params("parallel", "arbitrary"),
        name=name,
    )(h, wt, b.reshape(1, n))


def _tril(n):
    r = lax.broadcasted_iota(jnp.int32, (n, n), 0)
    c = lax.broadcasted_iota(jnp.int32, (n, n), 1)
    return r >= c


def _chunk_cumsum(x):
    pos = lax.broadcasted_iota(jnp.int32, x.shape, 0) & (CHUNK - 1)
    s = 1
    while s < CHUNK:
        x = x + jnp.where(pos >= s, pltpu.roll(x, s, axis=0), 0.0)
        s *= 2
    return x


def _chunk_slices(rows):
    return [slice(c * CHUNK, (c + 1) * CHUNK) for c in range(rows // CHUNK)]


def _hgrn2_head(q_pre, f_pre, v_in, g_pre, lb, gain, st_ref):
    q = jax.nn.silu(q_pre) * (HG_HEAD_DIM ** -0.5)
    f = lb + (1.0 - lb) * jax.nn.sigmoid(f_pre)
    k = 1.0 - f
    b = _chunk_cumsum(jnp.log(f))
    v = v_in.astype(BF16)
    causal = _tril(CHUNK)
    q_dec = (q * jnp.exp(b)).astype(BF16)
    k_inv = k * jnp.exp(-b)
    k_inv_b = k_inv.astype(BF16)

    chunks = _chunk_slices(q_pre.shape[0])
    decay = [jnp.exp(b[sl.stop - 1:sl.stop, :]) for sl in chunks]
    k_end = [(k_inv[sl] * d).astype(BF16) for sl, d in zip(chunks, decay)]
    yield
    attn = [jnp.where(causal, _dot_nt(q_dec[sl], k_inv_b[sl]), 0.0).astype(BF16) for sl in chunks]
    kv = [_dot_tn(v[sl], ke) for sl, ke in zip(chunks, k_end)]
    yield
    o_intra = [_dot(a, v[sl]) for a, sl in zip(attn, chunks)]
    st = st_ref[...]
    outs = []
    for c, sl in enumerate(chunks):
        outs.append(o_intra[c] + _dot_nt(q_dec[sl], st.astype(BF16)))
        st = decay[c] * st + kv[c]
    st_ref[...] = st
    yield
    o = jnp.concatenate(outs, axis=0)
    return _rms(o, gain) * jax.nn.silu(g_pre)


def _mlstm_head(q, k, v, o_pre, ig_col, b_col, gain, ct_ref, m_prev):
    rows = q.shape[0]
    qb = q.astype(BF16)
    kb = k.astype(BF16)
    v_aug = jnp.concatenate([v, jnp.ones((rows, ML_QK_DIM), F32)], axis=1).astype(BF16)
    wide = (rows, ML_QK_DIM)
    b = jnp.broadcast_to(b_col, wide)
    igb = jnp.broadcast_to(ig_col - b_col, wide)
    causal = _tril(CHUNK)
    chunks = _chunk_slices(rows)
    n_chunks = len(chunks)

    b_last = [b[sl.stop - 1:sl.stop, 0:1] for sl in chunks]
    log_d = [jnp.where(causal, b[sl, :CHUNK] + igb[sl].T[:CHUNK, :], -jnp.inf) for sl in chunks]
    m_intra = [jnp.max(ld, axis=1, keepdims=True) for ld in log_d]
    logw = [bl + igb[sl, 0:1] for bl, sl in zip(b_last, chunks)]
    logw_max = [jnp.max(lw, axis=0, keepdims=True) for lw in logw]
    qk = [_dot_nt(qb[sl], kb[sl]) for sl in chunks]
    yield

    m_in, m_out = [], []
    for c in range(n_chunks):
        m_in.append(m_prev)
        m_prev = jnp.maximum(b_last[c] + m_prev, logw_max[c])
        m_out.append(m_prev)

    m_inter = [b[sl, 0:1] + mi for sl, mi in zip(chunks, m_in)]
    m_tot = [jnp.maximum(a, bb) for a, bb in zip(m_inter, m_intra)]
    scores = [(qk[c] * jnp.exp(log_d[c] - m_tot[c])).astype(BF16) for c in range(n_chunks)]
    k_w = [(k[sl] * jnp.exp(logw[c] - m_out[c])).astype(BF16) for c, sl in enumerate(chunks)]
    yield
    intra = [_dot(scores[c], v_aug[sl]) for c, sl in enumerate(chunks)]
    kv = [_dot_tn(v_aug[sl], k_w[c]) for c, sl in enumerate(chunks)]
    yield
    ct = ct_ref[...]
    outs = []
    for c, sl in enumerate(chunks):
        tot = intra[c] + jnp.exp(m_inter[c] - m_tot[c]) * _dot_nt(qb[sl], ct.astype(BF16))
        num = tot[:, :ML_V_DIM]
        den = tot[:, ML_V_DIM:ML_V_DIM + 1]
        outs.append(num / jnp.maximum(jnp.abs(den), jnp.exp(-m_tot[c])))
        ct = jnp.exp(b_last[c] + m_in[c] - m_out[c]) * ct + kv[c]
    ct_ref[...] = ct
    yield
    hh = jnp.concatenate(outs, axis=0)
    return jax.nn.sigmoid(o_pre) * _rms(hh, gain), m_prev


def _interleave(*lanes):
    lanes = [iter(lane) for lane in lanes]
    while lanes:
        for lane in list(lanes):
            try:
                next(lane)
            except StopIteration:
                lanes.remove(lane)


def _idle(n):
    for _ in range(n):
        yield


def _chain(*gens):
    for g in gens:
        yield from g


def _causal_conv(buf_ref, x, w_ref, b_ref):
    kw = w_ref.shape[0]
    rows = x.shape[0]
    buf_ref[SUBLANES:SUBLANES + rows, :] = x
    acc = b_ref[...] + w_ref[kw - 1:kw, :] * x
    for j in range(kw - 1):
        s = kw - 1 - j
        acc = acc + w_ref[j:j + 1, :] * buf_ref[SUBLANES - s:SUBLANES - s + rows, :]
    buf_ref[0:SUBLANES, :] = buf_ref[rows:rows + SUBLANES, :]
    return acc


def _mixer_kernel(pm_ref, ps_ref, lbl_ref, hgn_ref, cw_ref, cb_ref, mln_ref,
                  pg_ref, x_ref, wua_ref, wub_ref, wout_ref, gpost_ref, gpre_ref,
                  x1_ref, h2_ref,
                  st_ref, ct_ref, m_ref, qkbuf_ref, ya_ref, yb_ref, *, layer, tiles_per_seq, n_tiles):
    s = pl.program_id(0)
    r = jnp.minimum(s, n_tiles - 1)

    @pl.when(s == 0)
    def _():
        ya_ref[...] = jnp.zeros_like(ya_ref)
        yb_ref[...] = jnp.zeros_like(yb_ref)

    @pl.when(r % tiles_per_seq == 0)
    def _():
        st_ref[...] = jnp.zeros_like(st_ref)
        ct_ref[...] = jnp.zeros_like(ct_ref)
        m_ref[...] = jnp.zeros_like(m_ref)
        qkbuf_ref[0:SUBLANES, :] = jnp.zeros((SUBLANES, qkbuf_ref.shape[1]), F32)

    d = x_ref.shape[1]
    hg_w = HG_HEADS * HG_HEAD_DIM
    qk_w = ML_HEADS * ML_QK_DIM
    qk_off = 4 * hg_w
    v_off = qk_off + 2 * qk_w
    o_off = v_off + ML_HEADS * ML_V_DIM
    n_pieces = d // 256
    pw = d // n_pieces
    merged = [None] * n_pieces
    mix = [None] * n_pieces
    ya_new = [None] * HG_HEADS
    yb_new = [None] * ML_HEADS

    def lift_lane():
        for j in range(n_pieces):
            cs = slice(j * pw, (j + 1) * pw)
            part_a = pg_ref[:, cs] * _dot(ya_ref[...], wua_ref[:, cs]).astype(BF16)
            yield
            merged[j] = part_a + pg_ref[:, d + j * pw:d + (j + 1) * pw] * _dot(yb_ref[...], wub_ref[:, cs]).astype(BF16)
            yield

    def outproj_lane():
        lhs = jnp.concatenate(merged, axis=1)
        for j in range(n_pieces):
            mix[j] = _dot(lhs, wout_ref[:, j * pw:(j + 1) * pw])
            yield
            if j % 2:
                yield

    def hgrn2_piece(h):
        cols = [slice(j * hg_w + h * HG_HEAD_DIM, j * hg_w + (h + 1) * HG_HEAD_DIM) for j in range(4)]
        hs = slice(h * HG_HEAD_DIM, (h + 1) * HG_HEAD_DIM)
        y = yield from _hgrn2_head(pm_ref[:, cols[0]], pm_ref[:, cols[1]], pm_ref[:, cols[2]],
                                   pm_ref[:, cols[3]], lb[:, hs], hgn_ref[:, hs], st_ref.at[h])
        ya_new[h] = y.astype(ya_ref.dtype)

    def mlstm_piece(h):
        qs = slice(h * ML_QK_DIM, (h + 1) * ML_QK_DIM)
        ks = slice(qk_w + h * ML_QK_DIM, qk_w + (h + 1) * ML_QK_DIM)
        vs = slice(h * ML_V_DIM, (h + 1) * ML_V_DIM)
        y, m_new = yield from _mlstm_head(qk[:, qs], qk[:, ks] * (ML_QK_DIM ** -0.5),
                                          pm_ref[:, v_off + vs.start:v_off + vs.stop],
                                          pm_ref[:, o_off + vs.start:o_off + vs.stop],
                                          gates[:, h:h + 1], gates_b[:, ML_HEADS + h:ML_HEADS + h + 1],
                                          mln_ref[:, vs], ct_ref.at[h], m_ref[h, 0:1, 0:1])
        m_ref[h] = jnp.broadcast_to(m_new, m_ref.shape[1:])
        yb_new[h] = y.astype(yb_ref.dtype)

    logits = lbl_ref[...]
    e = jnp.exp(logits - jnp.max(logits, axis=0, keepdims=True))
    sm = e / jnp.sum(e, axis=0, keepdims=True)
    lb = jnp.sum(sm[:layer + 1], axis=0, keepdims=True)
    qk = jax.nn.silu(_causal_conv(qkbuf_ref, pm_ref[:, qk_off:qk_off + 2 * qk_w], cw_ref, cb_ref))
    gates = ps_ref[...]
    gates_b = _chunk_cumsum(jax.nn.log_sigmoid(gates))

    _interleave(lift_lane(),
                _chain(*[hgrn2_piece(h) for h in range(0, HG_HEADS, 2)]),
                _chain(_idle(2), *[hgrn2_piece(h) for h in range(1, HG_HEADS, 2)]))
    _interleave(outproj_lane(),
                _chain(*[mlstm_piece(h) for h in range(0, ML_HEADS, 2)]),
                _chain(_idle(2), *[mlstm_piece(h) for h in range(1, ML_HEADS, 2)]))
    x1 = x_ref[...] + _rms(jnp.concatenate(mix, axis=1), gpost_ref[...])
    x1_ref[...] = x1
    h2_ref[...] = _rms(x1, gpre_ref[...]).astype(h2_ref.dtype)
    ya_ref[...] = jnp.concatenate(ya_new, axis=1)
    yb_ref[...] = jnp.concatenate(yb_new, axis=1)


def _mixer(p_main, p_small, p_gate, x2, lb_logits, hg_gain, conv_w, conv_b, ml_gain, w_ua, w_ub, w_out,
           g_post, g_pre, *, seq, layer, tm=256):
    m, d = x2.shape
    n_tiles = m // tm
    hg_w = HG_HEADS * HG_HEAD_DIM
    qk_w = ML_HEADS * ML_QK_DIM
    v_w = ML_HEADS * ML_V_DIM
    const = lambda s: (0, 0)
    cur = lambda s: (jnp.minimum(s, n_tiles - 1), 0)
    prev = lambda s: (jnp.maximum(s - 1, 0), 0)
    single = pl.Buffered(1)
    return pl.pallas_call(
        functools.partial(_mixer_kernel, layer=layer, tiles_per_seq=seq // tm, n_tiles=n_tiles),
        grid=(n_tiles + 1,),
        in_specs=[pl.BlockSpec((tm, p_main.shape[1]), cur),
                  pl.BlockSpec((tm, p_small.shape[1]), cur),
                  pl.BlockSpec(lb_logits.shape, const),
                  pl.BlockSpec((1, hg_w), const),
                  pl.BlockSpec(conv_w.shape, const),
                  pl.BlockSpec((1, 2 * qk_w), const),
                  pl.BlockSpec((1, v_w), const),
                  pl.BlockSpec((tm, 2 * d), prev),
                  pl.BlockSpec((tm, d), prev),
                  pl.BlockSpec((hg_w, d), const, pipeline_mode=single),
                  pl.BlockSpec((v_w, d), const, pipeline_mode=single),
                  pl.BlockSpec((d, d), const, pipeline_mode=single),
                  pl.BlockSpec((1, d), const),
                  pl.BlockSpec((1, d), const)],
        out_specs=[pl.BlockSpec((tm, d), prev),
                   pl.BlockSpec((tm, d), prev)],
        out_shape=[jax.ShapeDtypeStruct((m, d), F32),
                   jax.ShapeDtypeStruct((m, d), BF16)],
        scratch_shapes=[pltpu.VMEM((HG_HEADS, HG_HEAD_DIM, HG_HEAD_DIM), F32),
                        pltpu.VMEM((ML_HEADS, ML_V_DIM + ML_QK_DIM, ML_QK_DIM), F32),
                        pltpu.VMEM((ML_HEADS, SUBLANES, 128), F32),
                        pltpu.VMEM((tm + SUBLANES, 2 * qk_w), F32),
                        pltpu.VMEM((tm, hg_w), BF16),
                        pltpu.VMEM((tm, v_w), BF16)],
        compiler_params=_params("arbitrary"),
        name="mixer",
    )(p_main, p_small, lb_logits, hg_gain.reshape(1, hg_w), conv_w, conv_b.reshape(1, 2 * qk_w),
      ml_gain.reshape(1, v_w), p_gate, x2, w_ua, w_ub, w_out, g_post.reshape(1, d), g_pre.reshape(1, d))


def _ffn_up_kernel(h_ref, wg_ref, wu_ref, cw_ref, cb_ref, a_ref, gbuf_ref, wgb_ref, wub_ref, *, tm, tiles_per_seq):
    kw = cw_ref.shape[0]

    @pl.when(pl.program_id(1) == 0)
    def _():
        wgb_ref[...] = wg_ref[...].astype(BF16)
        wub_ref[...] = wu_ref[...].astype(BF16)

    @pl.when(pl.program_id(1) % tiles_per_seq == 0)
    def _():
        gbuf_ref[0:SUBLANES, :] = jnp.zeros((SUBLANES, gbuf_ref.shape[1]), F32)

    gbuf_ref[SUBLANES:SUBLANES + tm, :] = _dot(h_ref[...], wgb_ref[...])
    acc = cb_ref[...] + cw_ref[kw - 1:kw, :] * gbuf_ref[SUBLANES:SUBLANES + tm, :]
    for j in range(kw - 1):
        s = kw - 1 - j
        acc = acc + cw_ref[j:j + 1, :] * gbuf_ref[SUBLANES - s:SUBLANES - s + tm, :]
    gbuf_ref[0:SUBLANES, :] = gbuf_ref[tm:tm + SUBLANES, :]
    a_ref[...] = (jax.nn.gelu(acc, approximate=True) * _dot(h_ref[...], wub_ref[...])).astype(a_ref.dtype)


def _ffn_up(h2, w_gate, w_up, conv_w, conv_b, *, layer, seq, tm=1024, tf=512):
    m, d = h2.shape
    f = w_gate.shape[2]
    kw = conv_w.shape[0]
    wspec = pl.BlockSpec((None, d, tf), lambda j, i: (layer, 0, j))
    return pl.pallas_call(
        functools.partial(_ffn_up_kernel, tm=tm, tiles_per_seq=seq // tm),
        grid=(f // tf, m // tm),
        in_specs=[pl.BlockSpec((tm, d), lambda j, i: (i, 0)),
                  wspec, wspec,
                  pl.BlockSpec((kw, tf), lambda j, i: (0, j)),
                  pl.BlockSpec((1, tf), lambda j, i: (0, j))],
        out_specs=pl.BlockSpec((tm, tf), lambda j, i: (i, j)),
        out_shape=jax.ShapeDtypeStruct((m, f), BF16),
        scratch_shapes=[pltpu.VMEM((tm + SUBLANES, tf), F32),
                        pltpu.VMEM((d, tf), BF16),
                        pltpu.VMEM((d, tf), BF16)],
        compiler_params=_params("arbitrary", "arbitrary"),
        name="ffn_up",
    )(h2, w_gate, w_up, conv_w, conv_b.reshape(1, f))


def _ffn_down_kernel(a_ref, wd_ref, x1_ref, g_ref, o_ref):
    kk = pl.program_id(1)

    @pl.when(kk == 0)
    def _():
        o_ref[...] = jnp.zeros_like(o_ref)

    o_ref[...] += _dot(a_ref[...], wd_ref[...])

    @pl.when(kk == pl.num_programs(1) - 1)
    def _():
        o_ref[...] = x1_ref[...] + _rms(o_ref[...], g_ref[...])


def _ffn_down(act, w_down, x1, gain, tm=512, tk=1408):
    m, f = act.shape
    d = w_down.shape[1]
    return pl.pallas_call(
        _ffn_down_kernel,
        grid=(m // tm, f // tk),
        in_specs=[pl.BlockSpec((tm, tk), lambda i, k: (i, k)),
                  pl.BlockSpec((tk, d), lambda i, k: (k, 0)),
                  pl.BlockSpec((tm, d), lambda i, k: (i, 0)),
                  pl.BlockSpec((1, d), lambda i, k: (0, 0))],
        out_specs=pl.BlockSpec((tm, d), lambda i, k: (i, 0)),
        out_shape=jax.ShapeDtypeStruct((m, d), F32),
        compiler_params=_params("parallel", "arbitrary"),
        name="ffn_down",
    )(act, w_down, x1, gain.reshape(1, d))


def kernel(x, norm_mix_pre, norm_mix_post, norm_ffn_pre, norm_ffn_post, w_in, b_in, hg_lb_logits, hg_norm,
           ml_conv_w, ml_conv_b, ml_norm, w_up_a, w_up_b, w_out, ffn_w_gate, ffn_w_up, ffn_conv_w, ffn_conv_b,
           ffn_w_down):
    batch, seq, d = x.shape
    depth = w_in.shape[0]
    hg_w = HG_HEADS * HG_HEAD_DIM
    qk_w = ML_HEADS * ML_QK_DIM
    v_w = ML_HEADS * ML_V_DIM
    main_w = 4 * hg_w + 2 * qk_w + 2 * v_w
    gate_off = main_w + 2 * ML_HEADS
    w_in_t = jnp.swapaxes(w_in, 1, 2)

    x2 = x.reshape(batch * seq, d)
    for l in range(depth):
        h1 = _rmsnorm_cast(x2, norm_mix_pre[l])
        p_main = _mm_bias_t(h1, w_in_t, b_in[l, :main_w], layer=l, row0=0, n=main_w, tm=1024, tn=1024,
                            out_dtype=F32, name="in_proj_main")
        p_gate = _mm_bias_t(h1, w_in_t, b_in[l, gate_off:], layer=l, row0=gate_off, n=2 * d, tm=2048, tn=1024,
                            out_dtype=BF16, sigmoid=True, name="in_proj_gate")
        p_small = _mm_bias_t(h1, w_in_t, b_in[l, main_w:gate_off], layer=l, row0=main_w, n=2 * ML_HEADS,
                             tm=1024, tn=2 * ML_HEADS, out_dtype=F32, name="in_proj_small")

        x2, h2 = _mixer(p_main, p_small, p_gate, x2, hg_lb_logits, hg_norm[l], ml_conv_w[l], ml_conv_b[l],
                        ml_norm[l], w_up_a[l].astype(BF16), w_up_b[l].astype(BF16), w_out[l].astype(BF16),
                        norm_mix_post[l], norm_ffn_pre[l], seq=seq, layer=l)

        act = _ffn_up(h2, ffn_w_gate, ffn_w_up, ffn_conv_w[l], ffn_conv_b[l], layer=l, seq=seq)
        x2 = _ffn_down(act, ffn_w_down[l].astype(BF16), x2, norm_ffn_post[l])
    return x2.reshape(batch, seq, d)
```

```python
import functools

import jax
import jax.numpy as jnp
from jax import lax
from jax.experimental import pallas as pl
from jax.experimental.pallas import tpu as pltpu

HG_HEADS = 8
HG_HEAD_DIM = 128
ML_HEADS = 4
ML_QK_DIM = 128
ML_V_DIM = 256
CHUNK = 64
EPS = 1e-6

F32 = jnp.float32
BF16 = jnp.bfloat16

VMEM_LIMIT_BYTES = 56 * 1024 * 1024
SUBLANES = 8
BF16_SUBLANES = 16


def _params(*sem):
    return pltpu.CompilerParams(dimension_semantics=sem, vmem_limit_bytes=VMEM_LIMIT_BYTES)


def _dot(a, b):
    return jnp.dot(a, b, preferred_element_type=F32)


def _dot_nt(a, b):
    return lax.dot_general(a, b, (((1,), (1,)), ((), ())), preferred_element_type=F32)


def _dot_tn(a, b):
    return lax.dot_general(a, b, (((0,), (0,)), ((), ())), preferred_element_type=F32)


def _rms(u, gain):
    return u * lax.rsqrt(jnp.mean(u * u, axis=-1, keepdims=True) + EPS) * gain


def _rmsnorm_kernel(x_ref, g_ref, o_ref):
    o_ref[...] = _rms(x_ref[...], g_ref[...]).astype(o_ref.dtype)


def _rmsnorm_cast(x2, gain, tm=512):
    m, d = x2.shape
    return pl.pallas_call(
        _rmsnorm_kernel,
        grid=(m // tm,),
        in_specs=[pl.BlockSpec((tm, d), lambda i: (i, 0)),
                  pl.BlockSpec((1, d), lambda i: (0, 0))],
        out_specs=pl.BlockSpec((tm, d), lambda i: (i, 0)),
        out_shape=jax.ShapeDtypeStruct((m, d), BF16),
        compiler_params=_params("parallel"),
        name="rmsnorm_cast",
    )(x2, gain.reshape(1, d))


def _mm_bias_kernel(h_ref, w_ref, b_ref, *rest, sigmoid, n_side):
    side_in, o_ref, side_out, wb_ref = rest[:n_side], rest[n_side], rest[n_side + 1:2 * n_side + 1], rest[-1]

    @pl.when(pl.program_id(1) == 0)
    def _():
        wb_ref[...] = w_ref[...].astype(BF16)

    for src, dst in zip(side_in, side_out):
        dst[...] = src[...].astype(dst.dtype)

    acc = _dot_nt(h_ref[...], wb_ref[...]) + b_ref[...]
    if sigmoid:
        acc = jax.nn.sigmoid(acc)
    o_ref[...] = acc.astype(o_ref.dtype)


def _mm_bias_t(h, wt, b, *, layer, row0, n, tm, tn, out_dtype, sigmoid=False, side_casts=(), name):
    m, k = h.shape
    row0 += layer * wt.shape[1]
    wt = wt.reshape(-1, k)
    n_i = m // tm
    steps = (n // tn) * n_i
    side_specs, side_out_specs, side_shapes, side_args = [], [], [], []
    for a in side_casts:
        rows, cols = a.shape[1:]
        rb = next(r for r in range(BF16_SUBLANES, rows + 1, BF16_SUBLANES) if rows % r == 0 and rows // r <= steps)
        nb = rows // rb
        side_specs.append(pl.BlockSpec((None, rb, cols),
                                       lambda j, i, nb=nb: (layer, jnp.minimum(j * n_i + i, nb - 1), 0)))
        side_out_specs.append(pl.BlockSpec((rb, cols), lambda j, i, nb=nb: (jnp.minimum(j * n_i + i, nb - 1), 0)))
        side_shapes.append(jax.ShapeDtypeStruct((rows, cols), BF16))
        side_args.append(a)
    outs = pl.pallas_call(
        functools.partial(_mm_bias_kernel, sigmoid=sigmoid, n_side=len(side_casts)),
        grid=(n // tn, n_i),
        in_specs=[pl.BlockSpec((tm, k), lambda j, i: (i, 0)),
                  pl.BlockSpec((pl.Element(tn), pl.Element(k)),
                               lambda j, i: (pl.multiple_of(row0 + j * tn, SUBLANES), 0)),
                  pl.BlockSpec((1, tn), lambda j, i: (0, j))] + side_specs,
        out_specs=[pl.BlockSpec((tm, tn), lambda j, i: (i, j))] + side_out_specs,
        out_shape=[jax.ShapeDtypeStruct((m, n), out_dtype)] + side_shapes,
        scratch_shapes=[pltpu.VMEM((tn, k), BF16)],
        compiler_params=_params("arbitrary", "arbitrary"),
        name=name,
    )(h, wt, b.reshape(1, n), *side_args)
    return outs if side_casts else outs[0]


def _tril(n):
    r = lax.broadcasted_iota(jnp.int32, (n, n), 0)
    c = lax.broadcasted_iota(jnp.int32, (n, n), 1)
    return r >= c


def _chunk_cumsum(x):
    pos = lax.broadcasted_iota(jnp.int32, x.shape, 0) & (CHUNK - 1)
    s = 1
    while s < CHUNK:
        x = x + jnp.where(pos >= s, pltpu.roll(x, s, axis=0), 0.0)
        s *= 2
    return x


def _chunk_slices(rows):
    return [slice(c * CHUNK, (c + 1) * CHUNK) for c in range(rows // CHUNK)]


def _hgrn2_head(q_pre, f_pre, v_in, g_pre, lb, gain, st_ref):
    q = jax.nn.silu(q_pre) * (HG_HEAD_DIM ** -0.5)
    f = lb + (1.0 - lb) * jax.nn.sigmoid(f_pre)
    k = 1.0 - f
    b = _chunk_cumsum(jnp.log(f))
    v = v_in.astype(BF16)
    causal = _tril(CHUNK)
    q_dec = (q * jnp.exp(b)).astype(BF16)
    k_inv = k * jnp.exp(-b)
    k_inv_b = k_inv.astype(BF16)

    chunks = _chunk_slices(q_pre.shape[0])
    decay = [jnp.exp(b[sl.stop - 1:sl.stop, :]) for sl in chunks]
    k_end = [(k_inv[sl] * d).astype(BF16) for sl, d in zip(chunks, decay)]
    yield
    attn = [jnp.where(causal, _dot_nt(q_dec[sl], k_inv_b[sl]), 0.0).astype(BF16) for sl in chunks]
    kv = [_dot_tn(v[sl], ke) for sl, ke in zip(chunks, k_end)]
    yield
    o_intra = [_dot(a, v[sl]) for a, sl in zip(attn, chunks)]
    st = st_ref[...]
    outs = []
    for c, sl in enumerate(chunks):
        outs.append(o_intra[c] + _dot_nt(q_dec[sl], st.astype(BF16)))
        st = decay[c] * st + kv[c]
    st_ref[...] = st
    yield
    o = jnp.concatenate(outs, axis=0)
    return _rms(o, gain) * jax.nn.silu(g_pre)


def _mlstm_head(q, k, v, o_pre, ig_col, b_col, gain, ct_ref, m_prev):
    rows = q.shape[0]
    qb = q.astype(BF16)
    kb = k.astype(BF16)
    v_aug = jnp.concatenate([v, jnp.ones((rows, ML_QK_DIM), F32)], axis=1).astype(BF16)
    wide = (rows, ML_QK_DIM)
    b = jnp.broadcast_to(b_col, wide)
    igb = jnp.broadcast_to(ig_col - b_col, wide)
    causal = _tril(CHUNK)
    chunks = _chunk_slices(rows)
    n_chunks = len(chunks)

    b_last = [b[sl.stop - 1:sl.stop, 0:1] for sl in chunks]
    log_d = [jnp.where(causal, b[sl, :CHUNK] + igb[sl].T[:CHUNK, :], -jnp.inf) for sl in chunks]
    m_intra = [jnp.max(ld, axis=1, keepdims=True) for ld in log_d]
    logw = [bl + igb[sl, 0:1] for bl, sl in zip(b_last, chunks)]
    logw_max = [jnp.max(lw, axis=0, keepdims=True) for lw in logw]
    qk = [_dot_nt(qb[sl], kb[sl]) for sl in chunks]
    yield

    m_in, m_out = [], []
    for c in range(n_chunks):
        m_in.append(m_prev)
        m_prev = jnp.maximum(b_last[c] + m_prev, logw_max[c])
        m_out.append(m_prev)

    m_inter = [b[sl, 0:1] + mi for sl, mi in zip(chunks, m_in)]
    m_tot = [jnp.maximum(a, bb) for a, bb in zip(m_inter, m_intra)]
    scores = [(qk[c] * jnp.exp(log_d[c] - m_tot[c])).astype(BF16) for c in range(n_chunks)]
    k_w = [(k[sl] * jnp.exp(logw[c] - m_out[c])).astype(BF16) for c, sl in enumerate(chunks)]
    yield
    intra = [_dot(scores[c], v_aug[sl]) for c, sl in enumerate(chunks)]
    kv = [_dot_tn(v_aug[sl], k_w[c]) for c, sl in enumerate(chunks)]
    yield
    ct = ct_ref[...]
    outs = []
    for c, sl in enumerate(chunks):
        tot = intra[c] + jnp.exp(m_inter[c] - m_tot[c]) * _dot_nt(qb[sl], ct.astype(BF16))
        num = tot[:, :ML_V_DIM]
        den = tot[:, ML_V_DIM:ML_V_DIM + 1]
        outs.append(num / jnp.maximum(jnp.abs(den), jnp.exp(-m_tot[c])))
        ct = jnp.exp(b_last[c] + m_in[c] - m_out[c]) * ct + kv[c]
    ct_ref[...] = ct
    yield
    hh = jnp.concatenate(outs, axis=0)
    return jax.nn.sigmoid(o_pre) * _rms(hh, gain), m_prev


def _interleave(*lanes):
    lanes = [iter(lane) for lane in lanes]
    while lanes:
        for lane in list(lanes):
            try:
                next(lane)
            except StopIteration:
                lanes.remove(lane)


def _idle(n):
    for _ in range(n):
        yield


def _chain(*gens):
    for g in gens:
        yield from g


def _causal_conv(buf_ref, x, w_ref, b_ref):
    kw = w_ref.shape[0]
    rows = x.shape[0]
    buf_ref[SUBLANES:SUBLANES + rows, :] = x
    acc = b_ref[...] + w_ref[kw - 1:kw, :] * x
    for j in range(kw - 1):
        s = kw - 1 - j
        acc = acc + w_ref[j:j + 1, :] * buf_ref[SUBLANES - s:SUBLANES - s + rows, :]
    buf_ref[0:SUBLANES, :] = buf_ref[rows:rows + SUBLANES, :]
    return acc


def _mixer_kernel(pm_ref, ps_ref, lbl_ref, hgn_ref, cw_ref, cb_ref, mln_ref,
                  pg_ref, x_ref, wua_ref, wub_ref, wout_ref, gpost_ref, gpre_ref,
                  x1_ref, h2_ref,
                  st_ref, ct_ref, m_ref, qkbuf_ref, ya_ref, yb_ref, *, layer, tiles_per_seq, n_tiles):
    s = pl.program_id(0)
    r = jnp.minimum(s, n_tiles - 1)

    @pl.when(s == 0)
    def _():
        ya_ref[...] = jnp.zeros_like(ya_ref)
        yb_ref[...] = jnp.zeros_like(yb_ref)

    @pl.when(r % tiles_per_seq == 0)
    def _():
        st_ref[...] = jnp.zeros_like(st_ref)
        ct_ref[...] = jnp.zeros_like(ct_ref)
        m_ref[...] = jnp.zeros_like(m_ref)
        qkbuf_ref[0:SUBLANES, :] = jnp.zeros((SUBLANES, qkbuf_ref.shape[1]), F32)

    d = x_ref.shape[1]
    hg_w = HG_HEADS * HG_HEAD_DIM
    qk_w = ML_HEADS * ML_QK_DIM
    qk_off = 4 * hg_w
    v_off = qk_off + 2 * qk_w
    o_off = v_off + ML_HEADS * ML_V_DIM
    n_pieces = d // 256
    pw = d // n_pieces
    merged = [None] * n_pieces
    mix = [None] * n_pieces
    ya_new = [None] * HG_HEADS
    yb_new = [None] * ML_HEADS

    def lift_lane():
        for j in range(n_pieces):
            cs = slice(j * pw, (j + 1) * pw)
            part_a = pg_ref[:, cs] * _dot(ya_ref[...], wua_ref[:, cs]).astype(BF16)
            yield
            merged[j] = part_a + pg_ref[:, d + j * pw:d + (j + 1) * pw] * _dot(yb_ref[...], wub_ref[:, cs]).astype(BF16)
            yield

    def outproj_lane():
        lhs = jnp.concatenate(merged, axis=1)
        for j in range(n_pieces):
            mix[j] = _dot(lhs, wout_ref[:, j * pw:(j + 1) * pw])
            yield
            if j % 2:
                yield

    def hgrn2_piece(h):
        cols = [slice(j * hg_w + h * HG_HEAD_DIM, j * hg_w + (h + 1) * HG_HEAD_DIM) for j in range(4)]
        hs = slice(h * HG_HEAD_DIM, (h + 1) * HG_HEAD_DIM)
        y = yield from _hgrn2_head(pm_ref[:, cols[0]], pm_ref[:, cols[1]], pm_ref[:, cols[2]],
                                   pm_ref[:, cols[3]], lb[:, hs], hgn_ref[:, hs], st_ref.at[h])
        ya_new[h] = y.astype(ya_ref.dtype)

    def mlstm_piece(h):
        qs = slice(h * ML_QK_DIM, (h + 1) * ML_QK_DIM)
        ks = slice(qk_w + h * ML_QK_DIM, qk_w + (h + 1) * ML_QK_DIM)
        vs = slice(h * ML_V_DIM, (h + 1) * ML_V_DIM)
        y, m_new = yield from _mlstm_head(qk[:, qs], qk[:, ks] * (ML_QK_DIM ** -0.5),
                                          pm_ref[:, v_off + vs.start:v_off + vs.stop],
                                          pm_ref[:, o_off + vs.start:o_off + vs.stop],
                                          gates[:, h:h + 1], gates_b[:, ML_HEADS + h:ML_HEADS + h + 1],
                                          mln_ref[:, vs], ct_ref.at[h], m_ref[h, 0:1, 0:1])
        m_ref[h] = jnp.broadcast_to(m_new, m_ref.shape[1:])
        yb_new[h] = y.astype(yb_ref.dtype)

    logits = lbl_ref[...]
    e = jnp.exp(logits - jnp.max(logits, axis=0, keepdims=True))
    sm = e / jnp.sum(e, axis=0, keepdims=True)
    lb = jnp.sum(sm[:layer + 1], axis=0, keepdims=True)
    qk = jax.nn.silu(_causal_conv(qkbuf_ref, pm_ref[:, qk_off:qk_off + 2 * qk_w], cw_ref, cb_ref))
    gates = ps_ref[...]
    gates_b = _chunk_cumsum(jax.nn.log_sigmoid(gates))

    _interleave(lift_lane(),
                _chain(*[hgrn2_piece(h) for h in range(0, HG_HEADS, 2)]),
                _chain(_idle(2), *[hgrn2_piece(h) for h in range(1, HG_HEADS, 2)]))
    _interleave(outproj_lane(),
                _chain(*[mlstm_piece(h) for h in range(0, ML_HEADS, 2)]),
                _chain(_idle(2), *[mlstm_piece(h) for h in range(1, ML_HEADS, 2)]))
    x1 = x_ref[...] + _rms(jnp.concatenate(mix, axis=1), gpost_ref[...])
    x1_ref[...] = x1
    h2_ref[...] = _rms(x1, gpre_ref[...]).astype(h2_ref.dtype)
    ya_ref[...] = jnp.concatenate(ya_new, axis=1)
    yb_ref[...] = jnp.concatenate(yb_new, axis=1)


def _mixer(p_main, p_small, p_gate, x2, lb_logits, hg_gain, conv_w, conv_b, ml_gain, w_ua, w_ub, w_out,
           g_post, g_pre, *, seq, layer, tm=256):
    m, d = x2.shape
    n_tiles = m // tm
    hg_w = HG_HEADS * HG_HEAD_DIM
    qk_w = ML_HEADS * ML_QK_DIM
    v_w = ML_HEADS * ML_V_DIM
    const = lambda s: (0, 0)
    cur = lambda s: (jnp.minimum(s, n_tiles - 1), 0)
    prev = lambda s: (jnp.maximum(s - 1, 0), 0)
    single = pl.Buffered(1)
    return pl.pallas_call(
        functools.partial(_mixer_kernel, layer=layer, tiles_per_seq=seq // tm, n_tiles=n_tiles),
        grid=(n_tiles + 1,),
        in_specs=[pl.BlockSpec((tm, p_main.shape[1]), cur),
                  pl.BlockSpec((tm, p_small.shape[1]), cur),
                  pl.BlockSpec(lb_logits.shape, const),
                  pl.BlockSpec((1, hg_w), const),
                  pl.BlockSpec(conv_w.shape, const),
                  pl.BlockSpec((1, 2 * qk_w), const),
                  pl.BlockSpec((1, v_w), const),
                  pl.BlockSpec((tm, 2 * d), prev),
                  pl.BlockSpec((tm, d), prev),
                  pl.BlockSpec((hg_w, d), const, pipeline_mode=single),
                  pl.BlockSpec((v_w, d), const, pipeline_mode=single),
                  pl.BlockSpec((d, d), const, pipeline_mode=single),
                  pl.BlockSpec((1, d), const),
                  pl.BlockSpec((1, d), const)],
        out_specs=[pl.BlockSpec((tm, d), prev),
                   pl.BlockSpec((tm, d), prev)],
        out_shape=[jax.ShapeDtypeStruct((m, d), F32),
                   jax.ShapeDtypeStruct((m, d), BF16)],
        scratch_shapes=[pltpu.VMEM((HG_HEADS, HG_HEAD_DIM, HG_HEAD_DIM), F32),
                        pltpu.VMEM((ML_HEADS, ML_V_DIM + ML_QK_DIM, ML_QK_DIM), F32),
                        pltpu.VMEM((ML_HEADS, SUBLANES, 128), F32),
                        pltpu.VMEM((tm + SUBLANES, 2 * qk_w), F32),
                        pltpu.VMEM((tm, hg_w), BF16),
                        pltpu.VMEM((tm, v_w), BF16)],
        compiler_params=_params("arbitrary"),
        name="mixer",
    )(p_main, p_small, lb_logits, hg_gain.reshape(1, hg_w), conv_w, conv_b.reshape(1, 2 * qk_w),
      ml_gain.reshape(1, v_w), p_gate, x2, w_ua, w_ub, w_out, g_post.reshape(1, d), g_pre.reshape(1, d))


def _ffn_up_kernel(h_ref, wg_ref, wu_ref, cw_ref, cb_ref, a_ref, gbuf_ref, wgb_ref, wub_ref, *, tm, tiles_per_seq):
    kw = cw_ref.shape[0]

    @pl.when(pl.program_id(1) == 0)
    def _():
        wgb_ref[...] = wg_ref[...].astype(BF16)
        wub_ref[...] = wu_ref[...].astype(BF16)

    @pl.when(pl.program_id(1) % tiles_per_seq == 0)
    def _():
        gbuf_ref[0:SUBLANES, :] = jnp.zeros((SUBLANES, gbuf_ref.shape[1]), F32)

    gbuf_ref[SUBLANES:SUBLANES + tm, :] = _dot(h_ref[...], wgb_ref[...])
    acc = cb_ref[...] + cw_ref[kw - 1:kw, :] * gbuf_ref[SUBLANES:SUBLANES + tm, :]
    for j in range(kw - 1):
        s = kw - 1 - j
        acc = acc + cw_ref[j:j + 1, :] * gbuf_ref[SUBLANES - s:SUBLANES - s + tm, :]
    gbuf_ref[0:SUBLANES, :] = gbuf_ref[tm:tm + SUBLANES, :]
    a_ref[...] = (jax.nn.gelu(acc, approximate=True) * _dot(h_ref[...], wub_ref[...])).astype(a_ref.dtype)


def _ffn_up(h2, w_gate, w_up, conv_w, conv_b, *, layer, seq, tm=1024, tf=512):
    m, d = h2.shape
    f = w_gate.shape[2]
    kw = conv_w.shape[0]
    wspec = pl.BlockSpec((None, d, tf), lambda j, i: (layer, 0, j))
    return pl.pallas_call(
        functools.partial(_ffn_up_kernel, tm=tm, tiles_per_seq=seq // tm),
        grid=(f // tf, m // tm),
        in_specs=[pl.BlockSpec((tm, d), lambda j, i: (i, 0)),
                  wspec, wspec,
                  pl.BlockSpec((kw, tf), lambda j, i: (0, j)),
                  pl.BlockSpec((1, tf), lambda j, i: (0, j))],
        out_specs=pl.BlockSpec((tm, tf), lambda j, i: (i, j)),
        out_shape=jax.ShapeDtypeStruct((m, f), BF16),
        scratch_shapes=[pltpu.VMEM((tm + SUBLANES, tf), F32),
                        pltpu.VMEM((d, tf), BF16),
                        pltpu.VMEM((d, tf), BF16)],
        compiler_params=_params("arbitrary", "arbitrary"),
        name="ffn_up",
    )(h2, w_gate, w_up, conv_w, conv_b.reshape(1, f))


def _ffn_down_kernel(a_ref, wd_ref, x1_ref, g_ref, o_ref):
    o_ref[...] = x1_ref[...] + _rms(_dot(a_ref[...], wd_ref[...]), g_ref[...])


def _ffn_down(act, w_down, x1, gain, tm=256):
    m, f = act.shape
    d = w_down.shape[1]
    return pl.pallas_call(
        _ffn_down_kernel,
        grid=(m // tm,),
        in_specs=[pl.BlockSpec((tm, f), lambda i: (i, 0)),
                  pl.BlockSpec((f, d), lambda i: (0, 0), pipeline_mode=pl.Buffered(1)),
                  pl.BlockSpec((tm, d), lambda i: (i, 0)),
                  pl.BlockSpec((1, d), lambda i: (0, 0))],
        out_specs=pl.BlockSpec((tm, d), lambda i: (i, 0)),
        out_shape=jax.ShapeDtypeStruct((m, d), F32),
        compiler_params=_params("parallel"),
        name="ffn_down",
    )(act, w_down, x1, gain.reshape(1, d))


def kernel(x, norm_mix_pre, norm_mix_post, norm_ffn_pre, norm_ffn_post, w_in, b_in, hg_lb_logits, hg_norm,
           ml_conv_w, ml_conv_b, ml_norm, w_up_a, w_up_b, w_out, ffn_w_gate, ffn_w_up, ffn_conv_w, ffn_conv_b,
           ffn_w_down):
    batch, seq, d = x.shape
    depth = w_in.shape[0]
    hg_w = HG_HEADS * HG_HEAD_DIM
    qk_w = ML_HEADS * ML_QK_DIM
    v_w = ML_HEADS * ML_V_DIM
    main_w = 4 * hg_w + 2 * qk_w + 2 * v_w
    gate_off = main_w + 2 * ML_HEADS
    w_in_t = jnp.swapaxes(w_in, 1, 2)

    x2 = x.reshape(batch * seq, d)
    for l in range(depth):
        h1 = _rmsnorm_cast(x2, norm_mix_pre[l])
        p_main, w_down = _mm_bias_t(h1, w_in_t, b_in[l, :main_w], layer=l, row0=0, n=main_w, tm=1024, tn=1024,
                                    out_dtype=F32, side_casts=(ffn_w_down,), name="in_proj_main")
        p_gate, w_ua, w_ub, w_o = _mm_bias_t(h1, w_in_t, b_in[l, gate_off:], layer=l, row0=gate_off, n=2 * d,
                                             tm=1024, tn=1024, out_dtype=BF16, sigmoid=True,
                                             side_casts=(w_up_a, w_up_b, w_out), name="in_proj_gate")
        p_small = _mm_bias_t(h1, w_in_t, b_in[l, main_w:gate_off], layer=l, row0=main_w, n=2 * ML_HEADS,
                             tm=1024, tn=2 * ML_HEADS, out_dtype=F32, name="in_proj_small")

        x2, h2 = _mixer(p_main, p_small, p_gate, x2, hg_lb_logits, hg_norm[l], ml_conv_w[l], ml_conv_b[l],
                        ml_norm[l], w_ua, w_ub, w_o, norm_mix_post[l], norm_ffn_pre[l], seq=seq, layer=l)

        act = _ffn_up(h2, ffn_w_gate, ffn_w_up, ffn_conv_w[l], ffn_conv_b[l], layer=l, seq=seq)
        x2 = _ffn_down(act, w_down, x2, norm_ffn_post[l])
    return x2.reshape(batch, seq, d)
```

```python
import functools

import jax
import jax.numpy as jnp
from jax import lax
from jax.experimental import pallas as pl
from jax.experimental.pallas import tpu as pltpu

HG_HEADS = 8
HG_HEAD_DIM = 128
ML_HEADS = 4
ML_QK_DIM = 128
ML_V_DIM = 256
CHUNK = 64
EPS = 1e-6

F32 = jnp.float32
BF16 = jnp.bfloat16

VMEM_LIMIT_BYTES = 56 * 1024 * 1024
MIXER_VMEM_LIMIT_BYTES = 63 * 1024 * 1024 + 512 * 1024
SUBLANES = 8
LANES = 128


def _params(*sem, vmem=VMEM_LIMIT_BYTES):
    return pltpu.CompilerParams(dimension_semantics=sem, vmem_limit_bytes=vmem)


def _dot(a, b):
    return jnp.dot(a, b, preferred_element_type=F32)


def _dot_nt(a, b):
    return lax.dot_general(a, b, (((1,), (1,)), ((), ())), preferred_element_type=F32)


def _dot_tn(a, b):
    return lax.dot_general(a, b, (((0,), (0,)), ((), ())), preferred_element_type=F32)


def _rms(u, gain):
    return u * lax.rsqrt(jnp.mean(u * u, axis=-1, keepdims=True) + EPS) * gain


def _rmsnorm_kernel(x_ref, g_ref, ws_ref, bs_ref, o_ref, ps_ref):
    h = _rms(x_ref[...], g_ref[...]).astype(o_ref.dtype)
    o_ref[...] = h
    ps_ref[...] = _dot_nt(h, ws_ref[...].astype(BF16)) + bs_ref[...]


def _rmsnorm_cast(x2, gain, wt, b_small, *, layer, row0, tm=512):
    m, d = x2.shape
    n = b_small.shape[0]
    row0 += layer * wt.shape[1]
    return pl.pallas_call(
        _rmsnorm_kernel,
        grid=(m // tm,),
        in_specs=[pl.BlockSpec((tm, d), lambda i: (i, 0)),
                  pl.BlockSpec((1, d), lambda i: (0, 0)),
                  pl.BlockSpec((pl.Element(n), pl.Element(d)), lambda i: (row0, 0)),
                  pl.BlockSpec((1, n), lambda i: (0, 0))],
        out_specs=[pl.BlockSpec((tm, d), lambda i: (i, 0)),
                   pl.BlockSpec((tm, n), lambda i: (i, 0))],
        out_shape=[jax.ShapeDtypeStruct((m, d), BF16),
                   jax.ShapeDtypeStruct((m, n), F32)],
        compiler_params=_params("parallel"),
        name="rmsnorm_cast",
    )(x2, gain.reshape(1, d), wt.reshape(-1, d), b_small.reshape(1, n))


def _mm_bias_kernel(h_ref, w_ref, b_ref, *rest, sigmoid, side_transpose):
    n_side = len(side_transpose)
    side_in, o_ref, side_out, wb_ref = rest[:n_side], rest[n_side], rest[n_side + 1:2 * n_side + 1], rest[-1]

    @pl.when(pl.program_id(1) == 0)
    def _():
        wb_ref[...] = w_ref[...].astype(BF16)

    for src, dst, transpose in zip(side_in, side_out, side_transpose):
        blk = src[...]
        dst[...] = (blk.T if transpose else blk).astype(dst.dtype)

    acc = _dot_nt(h_ref[...], wb_ref[...]) + b_ref[...]
    if sigmoid:
        acc = jax.nn.sigmoid(acc)
    o_ref[...] = acc.astype(o_ref.dtype)


def _mm_bias_t(h, wt, b, *, layer, row0, n, tm, tn, out_dtype, sigmoid=False, side_casts=(), name):
    m, k = h.shape
    row0 += layer * wt.shape[1]
    wt = wt.reshape(-1, k)
    n_i = m // tm
    steps = (n // tn) * n_i
    side_specs, side_out_specs, side_shapes, side_args = [], [], [], []
    for a, start, rows, transpose in side_casts:
        cols = a.shape[2]
        start += layer * a.shape[1]
        rb = next(r for r in range(LANES, rows + 1, LANES) if rows % r == 0 and rows // r <= steps)
        nb = rows // rb
        blk = lambda j, i, nb=nb: jnp.minimum(j * n_i + i, nb - 1)
        side_specs.append(pl.BlockSpec(
            (pl.Element(rb), pl.Element(cols)),
            lambda j, i, blk=blk, start=start, rb=rb: (pl.multiple_of(start + blk(j, i) * rb, SUBLANES), 0)))
        if transpose:
            side_out_specs.append(pl.BlockSpec((cols, rb), lambda j, i, blk=blk: (0, blk(j, i))))
            side_shapes.append(jax.ShapeDtypeStruct((cols, rows), BF16))
        else:
            side_out_specs.append(pl.BlockSpec((rb, cols), lambda j, i, blk=blk: (blk(j, i), 0)))
            side_shapes.append(jax.ShapeDtypeStruct((rows, cols), BF16))
        side_args.append(a.reshape(-1, cols))
    outs = pl.pallas_call(
        functools.partial(_mm_bias_kernel, sigmoid=sigmoid, side_transpose=tuple(c[3] for c in side_casts)),
        grid=(n // tn, n_i),
        in_specs=[pl.BlockSpec((tm, k), lambda j, i: (i, 0)),
                  pl.BlockSpec((pl.Element(tn), pl.Element(k)),
                               lambda j, i: (pl.multiple_of(row0 + j * tn, SUBLANES), 0)),
                  pl.BlockSpec((1, tn), lambda j, i: (0, j))] + side_specs,
        out_specs=[pl.BlockSpec((tm, tn), lambda j, i: (i, j))] + side_out_specs,
        out_shape=[jax.ShapeDtypeStruct((m, n), out_dtype)] + side_shapes,
        scratch_shapes=[pltpu.VMEM((tn, k), BF16)],
        compiler_params=_params("arbitrary", "arbitrary"),
        name=name,
    )(h, wt, b.reshape(1, n), *side_args)
    return outs if side_casts else outs[0]


def _tril(n):
    r = lax.broadcasted_iota(jnp.int32, (n, n), 0)
    c = lax.broadcasted_iota(jnp.int32, (n, n), 1)
    return r >= c


def _chunk_cumsum(x):
    pos = lax.broadcasted_iota(jnp.int32, x.shape, 0) & (CHUNK - 1)
    s = 1
    while s < CHUNK:
        x = x + jnp.where(pos >= s, pltpu.roll(x, s, axis=0), 0.0)
        s *= 2
    return x


def _chunk_slices(rows):
    return [slice(c * CHUNK, (c + 1) * CHUNK) for c in range(rows // CHUNK)]


def _hgrn2_head(q_pre, f_pre, v_in, g_pre, lb, gain, st_ref):
    q = jax.nn.silu(q_pre.astype(F32)) * (HG_HEAD_DIM ** -0.5)
    f = lb + (1.0 - lb) * jax.nn.sigmoid(f_pre.astype(F32))
    k = 1.0 - f
    b = _chunk_cumsum(jnp.log(f))
    v = v_in.astype(BF16)
    causal = _tril(CHUNK)
    q_dec = (q * jnp.exp(b)).astype(BF16)
    k_inv = k * jnp.exp(-b)
    k_inv_b = k_inv.astype(BF16)

    chunks = _chunk_slices(q_pre.shape[0])
    decay = [jnp.exp(b[sl.stop - 1:sl.stop, :]) for sl in chunks]
    k_end = [(k_inv[sl] * d).astype(BF16) for sl, d in zip(chunks, decay)]
    yield
    attn = [jnp.where(causal, _dot_nt(q_dec[sl], k_inv_b[sl]), 0.0).astype(BF16) for sl in chunks]
    kv = [_dot_tn(v[sl], ke) for sl, ke in zip(chunks, k_end)]
    yield
    o_intra = [_dot(a, v[sl]) for a, sl in zip(attn, chunks)]
    st = st_ref[...]
    outs = []
    for c, sl in enumerate(chunks):
        outs.append(o_intra[c] + _dot_nt(q_dec[sl], st.astype(BF16)))
        st = decay[c] * st + kv[c]
    st_ref[...] = st
    yield
    o = jnp.concatenate(outs, axis=0)
    return _rms(o, gain) * jax.nn.silu(g_pre.astype(F32))


def _mlstm_head(q, k, v, o_pre, ig_col, b_col, gain, ct_ref, m_prev):
    rows = q.shape[0]
    qb = q.astype(BF16)
    kb = k.astype(BF16)
    v_aug = jnp.concatenate([v.astype(BF16), jnp.ones((rows, ML_QK_DIM), BF16)], axis=1)
    wide = (rows, ML_QK_DIM)
    b = jnp.broadcast_to(b_col, wide)
    igb = jnp.broadcast_to(ig_col - b_col, wide)
    causal = _tril(CHUNK)
    chunks = _chunk_slices(rows)
    n_chunks = len(chunks)

    b_last = [b[sl.stop - 1:sl.stop, 0:1] for sl in chunks]
    log_d = [jnp.where(causal, b[sl, :CHUNK] + igb[sl].T[:CHUNK, :], -jnp.inf) for sl in chunks]
    m_intra = [jnp.max(ld, axis=1, keepdims=True) for ld in log_d]
    logw = [bl + igb[sl, 0:1] for bl, sl in zip(b_last, chunks)]
    logw_max = [jnp.max(lw, axis=0, keepdims=True) for lw in logw]
    qk = [_dot_nt(qb[sl], kb[sl]) for sl in chunks]
    yield

    m_in, m_out = [], []
    for c in range(n_chunks):
        m_in.append(m_prev)
        m_prev = jnp.maximum(b_last[c] + m_prev, logw_max[c])
        m_out.append(m_prev)

    m_inter = [b[sl, 0:1] + mi for sl, mi in zip(chunks, m_in)]
    m_tot = [jnp.maximum(a, bb) for a, bb in zip(m_inter, m_intra)]
    scores = [(qk[c] * jnp.exp(log_d[c] - m_tot[c])).astype(BF16) for c in range(n_chunks)]
    k_w = [(k[sl] * jnp.exp(logw[c] - m_out[c])).astype(BF16) for c, sl in enumerate(chunks)]
    yield
    intra = [_dot(scores[c], v_aug[sl]) for c, sl in enumerate(chunks)]
    kv = [_dot_tn(v_aug[sl], k_w[c]) for c, sl in enumerate(chunks)]
    yield
    ct = ct_ref[...]
    outs = []
    for c, sl in enumerate(chunks):
        tot = intra[c] + jnp.exp(m_inter[c] - m_tot[c]) * _dot_nt(qb[sl], ct.astype(BF16))
        num = tot[:, :ML_V_DIM]
        den = tot[:, ML_V_DIM:ML_V_DIM + 1]
        outs.append(num / jnp.maximum(jnp.abs(den), jnp.exp(-m_tot[c])))
        ct = jnp.exp(b_last[c] + m_in[c] - m_out[c]) * ct + kv[c]
    ct_ref[...] = ct
    yield
    hh = jnp.concatenate(outs, axis=0)
    return jax.nn.sigmoid(o_pre.astype(F32)) * _rms(hh, gain), m_prev


def _interleave(*lanes):
    lanes = [iter(lane) for lane in lanes]
    while lanes:
        for lane in list(lanes):
            try:
                next(lane)
            except StopIteration:
                lanes.remove(lane)


def _idle(n):
    for _ in range(n):
        yield


def _chain(*gens):
    for g in gens:
        yield from g


def _causal_conv(buf_ref, x, w_ref, b_ref):
    kw = w_ref.shape[0]
    rows = x.shape[0]
    buf_ref[SUBLANES:SUBLANES + rows, :] = x
    acc = b_ref[...] + w_ref[kw - 1:kw, :] * x
    for j in range(kw - 1):
        s = kw - 1 - j
        acc = acc + w_ref[j:j + 1, :] * buf_ref[SUBLANES - s:SUBLANES - s + rows, :]
    buf_ref[0:SUBLANES, :] = buf_ref[rows:rows + SUBLANES, :]
    return acc


def _mixer_kernel(pm_ref, ps_ref, lbl_ref, hgn_ref, cw_ref, cb_ref, mln_ref,
                  h1_ref, wg_ref, bg_ref, x_ref, wua_ref, wub_ref, wout_ref, gpost_ref, gpre_ref,
                  x1_ref, h2_ref,
                  st_ref, ct_ref, m_ref, qkbuf_ref, ya_ref, yb_ref, *, layer, tiles_per_seq, n_tiles):
    s = pl.program_id(0)
    r = jnp.minimum(s, n_tiles - 1)

    @pl.when(s == 0)
    def _():
        ya_ref[...] = jnp.zeros_like(ya_ref)
        yb_ref[...] = jnp.zeros_like(yb_ref)

    @pl.when(r % tiles_per_seq == 0)
    def _():
        st_ref[...] = jnp.zeros_like(st_ref)
        ct_ref[...] = jnp.zeros_like(ct_ref)
        m_ref[...] = jnp.zeros_like(m_ref)
        qkbuf_ref[0:SUBLANES, :] = jnp.zeros((SUBLANES, qkbuf_ref.shape[1]), F32)

    d = x_ref.shape[1]
    hg_w = HG_HEADS * HG_HEAD_DIM
    qk_w = ML_HEADS * ML_QK_DIM
    qk_off = 4 * hg_w
    v_off = qk_off + 2 * qk_w
    o_off = v_off + ML_HEADS * ML_V_DIM
    n_pieces = d // 256
    pw = d // n_pieces
    merged = [None] * n_pieces
    mix = [None] * n_pieces
    ya_new = [None] * HG_HEADS
    yb_new = [None] * ML_HEADS

    def gate(cs):
        return jax.nn.sigmoid(_dot(h1_ref[...], wg_ref[:, cs]) + bg_ref[:, cs]).astype(BF16)

    def lift_lane():
        for j in range(n_pieces):
            cs = slice(j * pw, (j + 1) * pw)
            gate_a = gate(cs)
            yield
            part_a = gate_a * _dot(ya_ref[...], wua_ref[:, cs]).astype(BF16)
            yield
            gate_b = gate(slice(d + j * pw, d + (j + 1) * pw))
            yield
            merged[j] = part_a + gate_b * _dot(yb_ref[...], wub_ref[:, cs]).astype(BF16)
            yield

    def outproj_lane():
        lhs = jnp.concatenate(merged, axis=1)
        for j in range(n_pieces):
            mix[j] = _dot(lhs, wout_ref[:, j * pw:(j + 1) * pw])
            yield

    def hgrn2_piece(h):
        cols = [slice(j * hg_w + h * HG_HEAD_DIM, j * hg_w + (h + 1) * HG_HEAD_DIM) for j in range(4)]
        hs = slice(h * HG_HEAD_DIM, (h + 1) * HG_HEAD_DIM)
        y = yield from _hgrn2_head(pm_ref[:, cols[0]], pm_ref[:, cols[1]], pm_ref[:, cols[2]],
                                   pm_ref[:, cols[3]], lb[:, hs], hgn_ref[:, hs], st_ref.at[h])
        ya_new[h] = y.astype(ya_ref.dtype)

    def mlstm_piece(h):
        qs = slice(h * ML_QK_DIM, (h + 1) * ML_QK_DIM)
        ks = slice(qk_w + h * ML_QK_DIM, qk_w + (h + 1) * ML_QK_DIM)
        vs = slice(h * ML_V_DIM, (h + 1) * ML_V_DIM)
        y, m_new = yield from _mlstm_head(qk[:, qs], qk[:, ks] * (ML_QK_DIM ** -0.5),
                                          pm_ref[:, v_off + vs.start:v_off + vs.stop],
                                          pm_ref[:, o_off + vs.start:o_off + vs.stop],
                                          gates[:, h:h + 1], gates_b[:, ML_HEADS + h:ML_HEADS + h + 1],
                                          mln_ref[:, vs], ct_ref.at[h], m_ref[h, 0:1, 0:1])
        m_ref[h] = jnp.broadcast_to(m_new, m_ref.shape[1:])
        yb_new[h] = y.astype(yb_ref.dtype)

    logits = lbl_ref[...]
    e = jnp.exp(logits - jnp.max(logits, axis=0, keepdims=True))
    sm = e / jnp.sum(e, axis=0, keepdims=True)
    lb = jnp.sum(sm[:layer + 1], axis=0, keepdims=True)
    qk = jax.nn.silu(_causal_conv(qkbuf_ref, pm_ref[:, qk_off:qk_off + 2 * qk_w].astype(F32), cw_ref, cb_ref))
    gates = ps_ref[...]
    gates_b = _chunk_cumsum(jax.nn.log_sigmoid(gates))

    _interleave(lift_lane(),
                _chain(*[hgrn2_piece(h) for h in range(0, HG_HEADS, 2)], mlstm_piece(0)),
                _chain(_idle(2), *[hgrn2_piece(h) for h in range(1, HG_HEADS, 2)], mlstm_piece(1)))
    _interleave(outproj_lane(),
                mlstm_piece(2),
                _chain(_idle(2), mlstm_piece(3)))
    x1 = x_ref[...] + _rms(jnp.concatenate(mix, axis=1), gpost_ref[...])
    x1_ref[...] = x1
    h2_ref[...] = _rms(x1, gpre_ref[...]).astype(h2_ref.dtype)
    ya_ref[...] = jnp.concatenate(ya_new, axis=1)
    yb_ref[...] = jnp.concatenate(yb_new, axis=1)


def _mixer(p_main, p_small, h1, w_gate, b_gate, x2, lb_logits, hg_gain, conv_w, conv_b, ml_gain, w_ua, w_ub, w_out,
           g_post, g_pre, *, seq, layer, tm=256):
    m, d = x2.shape
    n_tiles = m // tm
    hg_w = HG_HEADS * HG_HEAD_DIM
    qk_w = ML_HEADS * ML_QK_DIM
    v_w = ML_HEADS * ML_V_DIM
    const = lambda s: (0, 0)
    cur = lambda s: (jnp.minimum(s, n_tiles - 1), 0)
    prev = lambda s: (jnp.maximum(s - 1, 0), 0)
    single = pl.Buffered(1)
    return pl.pallas_call(
        functools.partial(_mixer_kernel, layer=layer, tiles_per_seq=seq // tm, n_tiles=n_tiles),
        grid=(n_tiles + 1,),
        in_specs=[pl.BlockSpec((tm, p_main.shape[1]), cur),
                  pl.BlockSpec((tm, p_small.shape[1]), cur),
                  pl.BlockSpec(lb_logits.shape, const),
                  pl.BlockSpec((1, hg_w), const),
                  pl.BlockSpec(conv_w.shape, const),
                  pl.BlockSpec((1, 2 * qk_w), const),
                  pl.BlockSpec((1, v_w), const),
                  pl.BlockSpec((tm, d), prev),
                  pl.BlockSpec((d, 2 * d), const, pipeline_mode=single),
                  pl.BlockSpec((1, 2 * d), const),
                  pl.BlockSpec((tm, d), prev),
                  pl.BlockSpec((hg_w, d), const, pipeline_mode=single),
                  pl.BlockSpec((v_w, d), const, pipeline_mode=single),
                  pl.BlockSpec((d, d), const, pipeline_mode=single),
                  pl.BlockSpec((1, d), const),
                  pl.BlockSpec((1, d), const)],
        out_specs=[pl.BlockSpec((tm, d), prev),
                   pl.BlockSpec((tm, d), prev)],
        out_shape=[jax.ShapeDtypeStruct((m, d), F32),
                   jax.ShapeDtypeStruct((m, d), BF16)],
        scratch_shapes=[pltpu.VMEM((HG_HEADS, HG_HEAD_DIM, HG_HEAD_DIM), F32),
                        pltpu.VMEM((ML_HEADS, ML_V_DIM + ML_QK_DIM, ML_QK_DIM), F32),
                        pltpu.VMEM((ML_HEADS, SUBLANES, 128), F32),
                        pltpu.VMEM((tm + SUBLANES, 2 * qk_w), F32),
                        pltpu.VMEM((tm, hg_w), BF16),
                        pltpu.VMEM((tm, v_w), BF16)],
        compiler_params=_params("arbitrary", vmem=MIXER_VMEM_LIMIT_BYTES),
        name="mixer",
    )(p_main, p_small, lb_logits, hg_gain.reshape(1, hg_w), conv_w, conv_b.reshape(1, 2 * qk_w),
      ml_gain.reshape(1, v_w), h1, w_gate, b_gate.reshape(1, 2 * d), x2, w_ua, w_ub, w_out,
      g_post.reshape(1, d), g_pre.reshape(1, d))


def _ffn_up_kernel(h_ref, wg_ref, wu_ref, cw_ref, cb_ref, a_ref, gbuf_ref, wgb_ref, wub_ref, *, tm, tiles_per_seq):
    kw = cw_ref.shape[0]

    @pl.when(pl.program_id(1) == 0)
    def _():
        wgb_ref[...] = wg_ref[...].astype(BF16)
        wub_ref[...] = wu_ref[...].astype(BF16)

    @pl.when(pl.program_id(1) % tiles_per_seq == 0)
    def _():
        gbuf_ref[0:SUBLANES, :] = jnp.zeros((SUBLANES, gbuf_ref.shape[1]), F32)

    gbuf_ref[SUBLANES:SUBLANES + tm, :] = _dot(h_ref[...], wgb_ref[...])
    acc = cb_ref[...] + cw_ref[kw - 1:kw, :] * gbuf_ref[SUBLANES:SUBLANES + tm, :]
    for j in range(kw - 1):
        s = kw - 1 - j
        acc = acc + cw_ref[j:j + 1, :] * gbuf_ref[SUBLANES - s:SUBLANES - s + tm, :]
    gbuf_ref[0:SUBLANES, :] = gbuf_ref[tm:tm + SUBLANES, :]
    a_ref[...] = (jax.nn.gelu(acc, approximate=True) * _dot(h_ref[...], wub_ref[...])).astype(a_ref.dtype)


def _ffn_up(h2, w_gate, w_up, conv_w, conv_b, *, layer, seq, tm=1024, tf=512):
    m, d = h2.shape
    f = w_gate.shape[2]
    kw = conv_w.shape[0]
    wspec = pl.BlockSpec((None, d, tf), lambda j, i: (layer, 0, j))
    return pl.pallas_call(
        functools.partial(_ffn_up_kernel, tm=tm, tiles_per_seq=seq // tm),
        grid=(f // tf, m // tm),
        in_specs=[pl.BlockSpec((tm, d), lambda j, i: (i, 0)),
                  wspec, wspec,
                  pl.BlockSpec((kw, tf), lambda j, i: (0, j)),
                  pl.BlockSpec((1, tf), lambda j, i: (0, j))],
        out_specs=pl.BlockSpec((tm, tf), lambda j, i: (i, j)),
        out_shape=jax.ShapeDtypeStruct((m, f), BF16),
        scratch_shapes=[pltpu.VMEM((tm + SUBLANES, tf), F32),
                        pltpu.VMEM((d, tf), BF16),
                        pltpu.VMEM((d, tf), BF16)],
        compiler_params=_params("arbitrary", "arbitrary"),
        name="ffn_up",
    )(h2, w_gate, w_up, conv_w, conv_b.reshape(1, f))


def _ffn_down_kernel(a_ref, wd_ref, x1_ref, g_ref, o_ref):
    o_ref[...] = x1_ref[...] + _rms(_dot(a_ref[...], wd_ref[...]), g_ref[...])


def _ffn_down(act, w_down, x1, gain, tm=256):
    m, f = act.shape
    d = w_down.shape[1]
    return pl.pallas_call(
        _ffn_down_kernel,
        grid=(m // tm,),
        in_specs=[pl.BlockSpec((tm, f), lambda i: (i, 0)),
                  pl.BlockSpec((f, d), lambda i: (0, 0), pipeline_mode=pl.Buffered(1)),
                  pl.BlockSpec((tm, d), lambda i: (i, 0)),
                  pl.BlockSpec((1, d), lambda i: (0, 0))],
        out_specs=pl.BlockSpec((tm, d), lambda i: (i, 0)),
        out_shape=jax.ShapeDtypeStruct((m, d), F32),
        compiler_params=_params("parallel"),
        name="ffn_down",
    )(act, w_down, x1, gain.reshape(1, d))


def kernel(x, norm_mix_pre, norm_mix_post, norm_ffn_pre, norm_ffn_post, w_in, b_in, hg_lb_logits, hg_norm,
           ml_conv_w, ml_conv_b, ml_norm, w_up_a, w_up_b, w_out, ffn_w_gate, ffn_w_up, ffn_conv_w, ffn_conv_b,
           ffn_w_down):
    batch, seq, d = x.shape
    depth = w_in.shape[0]
    hg_w = HG_HEADS * HG_HEAD_DIM
    qk_w = ML_HEADS * ML_QK_DIM
    v_w = ML_HEADS * ML_V_DIM
    main_w = 4 * hg_w + 2 * qk_w + 2 * v_w
    gate_off = main_w + 2 * ML_HEADS
    w_in_t = jnp.swapaxes(w_in, 1, 2)

    x2 = x.reshape(batch * seq, d)
    for l in range(depth):
        h1, p_small = _rmsnorm_cast(x2, norm_mix_pre[l], w_in_t, b_in[l, main_w:gate_off], layer=l, row0=main_w)
        p_main, w_down, w_g, w_ua, w_ub, w_o = _mm_bias_t(
            h1, w_in_t, b_in[l, :main_w], layer=l, row0=0, n=main_w, tm=1024, tn=1024, out_dtype=BF16,
            side_casts=((ffn_w_down, 0, ffn_w_down.shape[1], False), (w_in_t, gate_off, 2 * d, True),
                        (w_up_a, 0, hg_w, False), (w_up_b, 0, v_w, False), (w_out, 0, d, False)),
            name="in_proj_main")
        x2, h2 = _mixer(p_main, p_small, h1, w_g, b_in[l, gate_off:], x2, hg_lb_logits, hg_norm[l], ml_conv_w[l],
                        ml_conv_b[l], ml_norm[l], w_ua, w_ub, w_o, norm_mix_post[l], norm_ffn_pre[l],
                        seq=seq, layer=l)

        act = _ffn_up(h2, ffn_w_gate, ffn_w_up, ffn_conv_w[l], ffn_conv_b[l], layer=l, seq=seq)
        x2 = _ffn_down(act, w_down, x2, norm_ffn_post[l])
    return x2.reshape(batch, seq, d)
```

```python
import functools

import jax
import jax.numpy as jnp
from jax import lax
from jax.experimental import pallas as pl
from jax.experimental.pallas import tpu as pltpu

HG_HEADS = 8
HG_HEAD_DIM = 128
ML_HEADS = 4
ML_QK_DIM = 128
ML_V_DIM = 256
CHUNK = 64
EPS = 1e-6

F32 = jnp.float32
BF16 = jnp.bfloat16

VMEM_LIMIT_BYTES = 56 * 1024 * 1024
MIXER_VMEM_LIMIT_BYTES = 63 * 1024 * 1024 + 512 * 1024
SUBLANES = 8
LANES = 128


def _params(*sem, vmem=VMEM_LIMIT_BYTES):
    return pltpu.CompilerParams(dimension_semantics=sem, vmem_limit_bytes=vmem)


def _dot(a, b):
    return jnp.dot(a, b, preferred_element_type=F32)


def _dot_nt(a, b):
    return lax.dot_general(a, b, (((1,), (1,)), ((), ())), preferred_element_type=F32)


def _dot_tn(a, b):
    return lax.dot_general(a, b, (((0,), (0,)), ((), ())), preferred_element_type=F32)


def _rms(u, gain):
    return u * lax.rsqrt(jnp.mean(u * u, axis=-1, keepdims=True) + EPS) * gain


def _rmsnorm_kernel(x_ref, g_ref, ws_ref, bs_ref, o_ref, ps_ref):
    h = _rms(x_ref[...], g_ref[...]).astype(o_ref.dtype)
    o_ref[...] = h
    ps_ref[...] = _dot_nt(h, ws_ref[...].astype(BF16)) + bs_ref[...]


def _rmsnorm_cast(x2, gain, wt, b_small, *, layer, row0, tm=512):
    m, d = x2.shape
    n = b_small.shape[0]
    row0 += layer * wt.shape[1]
    return pl.pallas_call(
        _rmsnorm_kernel,
        grid=(m // tm,),
        in_specs=[pl.BlockSpec((tm, d), lambda i: (i, 0)),
                  pl.BlockSpec((1, d), lambda i: (0, 0)),
                  pl.BlockSpec((pl.Element(n), pl.Element(d)), lambda i: (row0, 0)),
                  pl.BlockSpec((1, n), lambda i: (0, 0))],
        out_specs=[pl.BlockSpec((tm, d), lambda i: (i, 0)),
                   pl.BlockSpec((tm, n), lambda i: (i, 0))],
        out_shape=[jax.ShapeDtypeStruct((m, d), BF16),
                   jax.ShapeDtypeStruct((m, n), F32)],
        compiler_params=_params("parallel"),
        name="rmsnorm_cast",
    )(x2, gain.reshape(1, d), wt.reshape(-1, d), b_small.reshape(1, n))


def _mm_bias_kernel(h_ref, w_ref, b_ref, *rest, sigmoid, side_transpose):
    n_side = len(side_transpose)
    side_in, o_ref, side_out, wb_ref = rest[:n_side], rest[n_side], rest[n_side + 1:2 * n_side + 1], rest[-1]

    @pl.when(pl.program_id(1) == 0)
    def _():
        wb_ref[...] = w_ref[...].astype(BF16)

    for src, dst, transpose in zip(side_in, side_out, side_transpose):
        blk = src[...]
        dst[...] = (blk.T if transpose else blk).astype(dst.dtype)

    acc = _dot_nt(h_ref[...], wb_ref[...]) + b_ref[...]
    if sigmoid:
        acc = jax.nn.sigmoid(acc)
    o_ref[...] = acc.astype(o_ref.dtype)


def _mm_bias_t(h, wt, b, *, layer, row0, n, tm, tn, out_dtype, sigmoid=False, side_casts=(), name):
    m, k = h.shape
    row0 += layer * wt.shape[1]
    wt = wt.reshape(-1, k)
    n_i = m // tm
    steps = (n // tn) * n_i
    side_specs, side_out_specs, side_shapes, side_args = [], [], [], []
    for a, start, rows, transpose in side_casts:
        cols = a.shape[2]
        start += layer * a.shape[1]
        rb = next(r for r in range(LANES, rows + 1, LANES) if rows % r == 0 and rows // r <= steps)
        nb = rows // rb
        blk = lambda j, i, nb=nb: jnp.minimum(j * n_i + i, nb - 1)
        side_specs.append(pl.BlockSpec(
            (pl.Element(rb), pl.Element(cols)),
            lambda j, i, blk=blk, start=start, rb=rb: (pl.multiple_of(start + blk(j, i) * rb, SUBLANES), 0)))
        if transpose:
            side_out_specs.append(pl.BlockSpec((cols, rb), lambda j, i, blk=blk: (0, blk(j, i))))
            side_shapes.append(jax.ShapeDtypeStruct((cols, rows), BF16))
        else:
            side_out_specs.append(pl.BlockSpec((rb, cols), lambda j, i, blk=blk: (blk(j, i), 0)))
            side_shapes.append(jax.ShapeDtypeStruct((rows, cols), BF16))
        side_args.append(a.reshape(-1, cols))
    outs = pl.pallas_call(
        functools.partial(_mm_bias_kernel, sigmoid=sigmoid, side_transpose=tuple(c[3] for c in side_casts)),
        grid=(n // tn, n_i),
        in_specs=[pl.BlockSpec((tm, k), lambda j, i: (i, 0)),
                  pl.BlockSpec((pl.Element(tn), pl.Element(k)),
                               lambda j, i: (pl.multiple_of(row0 + j * tn, SUBLANES), 0)),
                  pl.BlockSpec((1, tn), lambda j, i: (0, j))] + side_specs,
        out_specs=[pl.BlockSpec((tm, tn), lambda j, i: (i, j))] + side_out_specs,
        out_shape=[jax.ShapeDtypeStruct((m, n), out_dtype)] + side_shapes,
        scratch_shapes=[pltpu.VMEM((tn, k), BF16)],
        compiler_params=_params("arbitrary", "arbitrary"),
        name=name,
    )(h, wt, b.reshape(1, n), *side_args)
    return outs if side_casts else outs[0]


def _tril(n):
    r = lax.broadcasted_iota(jnp.int32, (n, n), 0)
    c = lax.broadcasted_iota(jnp.int32, (n, n), 1)
    return r >= c


def _chunk_cumsum(x):
    pos = lax.broadcasted_iota(jnp.int32, x.shape, 0) & (CHUNK - 1)
    s = 1
    while s < CHUNK:
        x = x + jnp.where(pos >= s, pltpu.roll(x, s, axis=0), 0.0)
        s *= 2
    return x


def _chunk_slices(rows):
    return [slice(c * CHUNK, (c + 1) * CHUNK) for c in range(rows // CHUNK)]


def _hgrn2_head(q_pre, f_pre, v_in, g_pre, lb, gain, st_ref):
    q = jax.nn.silu(q_pre.astype(F32)) * (HG_HEAD_DIM ** -0.5)
    f = lb + (1.0 - lb) * jax.nn.sigmoid(f_pre.astype(F32))
    k = 1.0 - f
    b = _chunk_cumsum(jnp.log(f))
    v = v_in.astype(BF16)
    causal = _tril(CHUNK)
    q_dec = (q * jnp.exp(b)).astype(BF16)
    k_inv = k * jnp.exp(-b)
    k_inv_b = k_inv.astype(BF16)

    chunks = _chunk_slices(q_pre.shape[0])
    decay = [jnp.exp(b[sl.stop - 1:sl.stop, :]) for sl in chunks]
    k_end = [(k_inv[sl] * d).astype(BF16) for sl, d in zip(chunks, decay)]
    yield
    attn = [jnp.where(causal, _dot_nt(q_dec[sl], k_inv_b[sl]), 0.0).astype(BF16) for sl in chunks]
    kv = [_dot_tn(v[sl], ke) for sl, ke in zip(chunks, k_end)]
    yield
    o_intra = [_dot(a, v[sl]) for a, sl in zip(attn, chunks)]
    st = st_ref[...]
    outs = []
    for c, sl in enumerate(chunks):
        outs.append(o_intra[c] + _dot_nt(q_dec[sl], st.astype(BF16)))
        st = decay[c] * st + kv[c]
    st_ref[...] = st
    yield
    o = jnp.concatenate(outs, axis=0)
    return _rms(o, gain) * jax.nn.silu(g_pre.astype(F32))


def _mlstm_head(q, k, v, o_pre, ig_col, b_col, gain, ct_ref, m_prev):
    rows = q.shape[0]
    qb = q.astype(BF16)
    kb = k.astype(BF16)
    v_aug = jnp.concatenate([v.astype(BF16), jnp.ones((rows, ML_QK_DIM), BF16)], axis=1)
    wide = (rows, ML_QK_DIM)
    b = jnp.broadcast_to(b_col, wide)
    igb = jnp.broadcast_to(ig_col - b_col, wide)
    causal = _tril(CHUNK)
    chunks = _chunk_slices(rows)
    n_chunks = len(chunks)

    b_last = [b[sl.stop - 1:sl.stop, 0:1] for sl in chunks]
    log_d = [jnp.where(causal, b[sl, :CHUNK] + igb[sl].T[:CHUNK, :], -jnp.inf) for sl in chunks]
    m_intra = [jnp.max(ld, axis=1, keepdims=True) for ld in log_d]
    logw = [bl + igb[sl, 0:1] for bl, sl in zip(b_last, chunks)]
    logw_max = [jnp.max(lw, axis=0, keepdims=True) for lw in logw]
    qk = [_dot_nt(qb[sl], kb[sl]) for sl in chunks]
    yield

    m_in, m_out = [], []
    for c in range(n_chunks):
        m_in.append(m_prev)
        m_prev = jnp.maximum(b_last[c] + m_prev, logw_max[c])
        m_out.append(m_prev)

    m_inter = [b[sl, 0:1] + mi for sl, mi in zip(chunks, m_in)]
    m_tot = [jnp.maximum(a, bb) for a, bb in zip(m_inter, m_intra)]
    scores = [(qk[c] * jnp.exp(log_d[c] - m_tot[c])).astype(BF16) for c in range(n_chunks)]
    k_w = [(k[sl] * jnp.exp(logw[c] - m_out[c])).astype(BF16) for c, sl in enumerate(chunks)]
    yield
    intra = [_dot(scores[c], v_aug[sl]) for c, sl in enumerate(chunks)]
    kv = [_dot_tn(v_aug[sl], k_w[c]) for c, sl in enumerate(chunks)]
    yield
    ct = ct_ref[...]
    outs = []
    for c, sl in enumerate(chunks):
        tot = intra[c] + jnp.exp(m_inter[c] - m_tot[c]) * _dot_nt(qb[sl], ct.astype(BF16))
        num = tot[:, :ML_V_DIM]
        den = tot[:, ML_V_DIM:ML_V_DIM + 1]
        outs.append(num / jnp.maximum(jnp.abs(den), jnp.exp(-m_tot[c])))
        ct = jnp.exp(b_last[c] + m_in[c] - m_out[c]) * ct + kv[c]
    ct_ref[...] = ct
    yield
    hh = jnp.concatenate(outs, axis=0)
    return jax.nn.sigmoid(o_pre.astype(F32)) * _rms(hh, gain), m_prev


def _interleave(*lanes):
    lanes = [iter(lane) for lane in lanes]
    while lanes:
        for lane in list(lanes):
            try:
                next(lane)
            except StopIteration:
                lanes.remove(lane)


def _idle(n):
    for _ in range(n):
        yield


def _chain(*gens):
    for g in gens:
        yield from g


def _causal_conv(buf_ref, x, w_ref, b_ref):
    kw = w_ref.shape[0]
    rows = x.shape[0]
    buf_ref[SUBLANES:SUBLANES + rows, :] = x
    acc = b_ref[...] + w_ref[kw - 1:kw, :] * x
    for j in range(kw - 1):
        s = kw - 1 - j
        acc = acc + w_ref[j:j + 1, :] * buf_ref[SUBLANES - s:SUBLANES - s + rows, :]
    buf_ref[0:SUBLANES, :] = buf_ref[rows:rows + SUBLANES, :]
    return acc


def _mixer_kernel(pm_ref, ps_ref, lbl_ref, hgn_ref, cw_ref, cb_ref, mln_ref,
                  h1_ref, wg_ref, bg_ref, x_ref, wua_ref, wub_ref, wout_ref, gpost_ref, gpre_ref,
                  x1_ref, h2_ref,
                  st_ref, ct_ref, m_ref, qkbuf_ref, ya_ref, yb_ref, *, layer, tiles_per_seq, n_tiles):
    s = pl.program_id(0)
    r = jnp.minimum(s, n_tiles - 1)

    @pl.when(s == 0)
    def _():
        ya_ref[...] = jnp.zeros_like(ya_ref)
        yb_ref[...] = jnp.zeros_like(yb_ref)

    @pl.when(r % tiles_per_seq == 0)
    def _():
        st_ref[...] = jnp.zeros_like(st_ref)
        ct_ref[...] = jnp.zeros_like(ct_ref)
        m_ref[...] = jnp.zeros_like(m_ref)
        qkbuf_ref[0:SUBLANES, :] = jnp.zeros((SUBLANES, qkbuf_ref.shape[1]), F32)

    d = x_ref.shape[1]
    hg_w = HG_HEADS * HG_HEAD_DIM
    qk_w = ML_HEADS * ML_QK_DIM
    qk_off = 4 * hg_w
    v_off = qk_off + 2 * qk_w
    o_off = v_off + ML_HEADS * ML_V_DIM
    n_pieces = d // 256
    pw = d // n_pieces
    merged = [None] * n_pieces
    mix = [None] * n_pieces
    ya_new = [None] * HG_HEADS
    yb_new = [None] * ML_HEADS

    def gate(cs):
        return jax.nn.sigmoid(_dot(h1_ref[...], wg_ref[:, cs]) + bg_ref[:, cs])

    def lift_lane():
        for j in range(n_pieces):
            cs = slice(j * pw, (j + 1) * pw)
            part_a = gate(cs) * _dot(ya_ref[...], wua_ref[:, cs])
            yield
            part_b = gate(slice(d + j * pw, d + (j + 1) * pw)) * _dot(yb_ref[...], wub_ref[:, cs])
            merged[j] = (part_a + part_b).astype(BF16)
            yield

    def outproj_lane():
        lhs = jnp.concatenate(merged, axis=1)
        for j in range(n_pieces):
            mix[j] = _dot(lhs, wout_ref[:, j * pw:(j + 1) * pw])
            yield

    def hgrn2_piece(h):
        cols = [slice(j * hg_w + h * HG_HEAD_DIM, j * hg_w + (h + 1) * HG_HEAD_DIM) for j in range(4)]
        hs = slice(h * HG_HEAD_DIM, (h + 1) * HG_HEAD_DIM)
        y = yield from _hgrn2_head(pm_ref[:, cols[0]], pm_ref[:, cols[1]], pm_ref[:, cols[2]],
                                   pm_ref[:, cols[3]], lb[:, hs], hgn_ref[:, hs], st_ref.at[h])
        ya_new[h] = y.astype(ya_ref.dtype)

    def mlstm_piece(h):
        qs = slice(h * ML_QK_DIM, (h + 1) * ML_QK_DIM)
        ks = slice(qk_w + h * ML_QK_DIM, qk_w + (h + 1) * ML_QK_DIM)
        vs = slice(h * ML_V_DIM, (h + 1) * ML_V_DIM)
        y, m_new = yield from _mlstm_head(qk[:, qs], qk[:, ks] * (ML_QK_DIM ** -0.5),
                                          pm_ref[:, v_off + vs.start:v_off + vs.stop],
                                          pm_ref[:, o_off + vs.start:o_off + vs.stop],
                                          gates[:, h:h + 1], gates_b[:, ML_HEADS + h:ML_HEADS + h + 1],
                                          mln_ref[:, vs], ct_ref.at[h], m_ref[h, 0:1, 0:1])
        m_ref[h] = jnp.broadcast_to(m_new, m_ref.shape[1:])
        yb_new[h] = y.astype(yb_ref.dtype)

    logits = lbl_ref[...]
    e = jnp.exp(logits - jnp.max(logits, axis=0, keepdims=True))
    sm = e / jnp.sum(e, axis=0, keepdims=True)
    lb = jnp.sum(sm[:layer + 1], axis=0, keepdims=True)
    qk = jax.nn.silu(_causal_conv(qkbuf_ref, pm_ref[:, qk_off:qk_off + 2 * qk_w].astype(F32), cw_ref, cb_ref))
    gates = ps_ref[...]
    gates_b = _chunk_cumsum(jax.nn.log_sigmoid(gates))

    _interleave(lift_lane(),
                _chain(*[hgrn2_piece(h) for h in range(0, HG_HEADS, 2)], mlstm_piece(0)),
                _chain(_idle(2), *[hgrn2_piece(h) for h in range(1, HG_HEADS, 2)], mlstm_piece(1)))
    _interleave(outproj_lane(),
                mlstm_piece(2),
                _chain(_idle(2), mlstm_piece(3)))
    x1 = x_ref[...] + _rms(jnp.concatenate(mix, axis=1), gpost_ref[...])
    x1_ref[...] = x1
    h2_ref[...] = _rms(x1, gpre_ref[...]).astype(h2_ref.dtype)
    ya_ref[...] = jnp.concatenate(ya_new, axis=1)
    yb_ref[...] = jnp.concatenate(yb_new, axis=1)


def _mixer(p_main, p_small, h1, w_gate, b_gate, x2, lb_logits, hg_gain, conv_w, conv_b, ml_gain, w_ua, w_ub, w_out,
           g_post, g_pre, *, seq, layer, tm=256):
    m, d = x2.shape
    n_tiles = m // tm
    hg_w = HG_HEADS * HG_HEAD_DIM
    qk_w = ML_HEADS * ML_QK_DIM
    v_w = ML_HEADS * ML_V_DIM
    const = lambda s: (0, 0)
    cur = lambda s: (jnp.minimum(s, n_tiles - 1), 0)
    prev = lambda s: (jnp.maximum(s - 1, 0), 0)
    single = pl.Buffered(1)
    return pl.pallas_call(
        functools.partial(_mixer_kernel, layer=layer, tiles_per_seq=seq // tm, n_tiles=n_tiles),
        grid=(n_tiles + 1,),
        in_specs=[pl.BlockSpec((tm, p_main.shape[1]), cur),
                  pl.BlockSpec((tm, p_small.shape[1]), cur),
                  pl.BlockSpec(lb_logits.shape, const),
                  pl.BlockSpec((1, hg_w), const),
                  pl.BlockSpec(conv_w.shape, const),
                  pl.BlockSpec((1, 2 * qk_w), const),
                  pl.BlockSpec((1, v_w), const),
                  pl.BlockSpec((tm, d), prev),
                  pl.BlockSpec((d, 2 * d), const, pipeline_mode=single),
                  pl.BlockSpec((1, 2 * d), const),
                  pl.BlockSpec((tm, d), prev),
                  pl.BlockSpec((hg_w, d), const, pipeline_mode=single),
                  pl.BlockSpec((v_w, d), const, pipeline_mode=single),
                  pl.BlockSpec((d, d), const, pipeline_mode=single),
                  pl.BlockSpec((1, d), const),
                  pl.BlockSpec((1, d), const)],
        out_specs=[pl.BlockSpec((tm, d), prev),
                   pl.BlockSpec((tm, d), prev)],
        out_shape=[jax.ShapeDtypeStruct((m, d), F32),
                   jax.ShapeDtypeStruct((m, d), BF16)],
        scratch_shapes=[pltpu.VMEM((HG_HEADS, HG_HEAD_DIM, HG_HEAD_DIM), F32),
                        pltpu.VMEM((ML_HEADS, ML_V_DIM + ML_QK_DIM, ML_QK_DIM), F32),
                        pltpu.VMEM((ML_HEADS, SUBLANES, 128), F32),
                        pltpu.VMEM((tm + SUBLANES, 2 * qk_w), F32),
                        pltpu.VMEM((tm, hg_w), BF16),
                        pltpu.VMEM((tm, v_w), BF16)],
        compiler_params=_params("arbitrary", vmem=MIXER_VMEM_LIMIT_BYTES),
        name="mixer",
    )(p_main, p_small, lb_logits, hg_gain.reshape(1, hg_w), conv_w, conv_b.reshape(1, 2 * qk_w),
      ml_gain.reshape(1, v_w), h1, w_gate, b_gate.reshape(1, 2 * d), x2, w_ua, w_ub, w_out,
      g_post.reshape(1, d), g_pre.reshape(1, d))


def _ffn_up_kernel(h_ref, wg_ref, wu_ref, cw_ref, cb_ref, a_ref, gbuf_ref, wgb_ref, wub_ref, *, tm, tiles_per_seq):
    kw = cw_ref.shape[0]

    @pl.when(pl.program_id(1) == 0)
    def _():
        wgb_ref[...] = wg_ref[...].astype(BF16)
        wub_ref[...] = wu_ref[...].astype(BF16)

    @pl.when(pl.program_id(1) % tiles_per_seq == 0)
    def _():
        gbuf_ref[0:SUBLANES, :] = jnp.zeros((SUBLANES, gbuf_ref.shape[1]), F32)

    gbuf_ref[SUBLANES:SUBLANES + tm, :] = _dot(h_ref[...], wgb_ref[...])
    acc = cb_ref[...] + cw_ref[kw - 1:kw, :] * gbuf_ref[SUBLANES:SUBLANES + tm, :]
    for j in range(kw - 1):
        s = kw - 1 - j
        acc = acc + cw_ref[j:j + 1, :] * gbuf_ref[SUBLANES - s:SUBLANES - s + tm, :]
    gbuf_ref[0:SUBLANES, :] = gbuf_ref[tm:tm + SUBLANES, :]
    a_ref[...] = (jax.nn.gelu(acc, approximate=True) * _dot(h_ref[...], wub_ref[...])).astype(a_ref.dtype)


def _ffn_up(h2, w_gate, w_up, conv_w, conv_b, *, layer, seq, tm=1024, tf=512):
    m, d = h2.shape
    f = w_gate.shape[2]
    kw = conv_w.shape[0]
    wspec = pl.BlockSpec((None, d, tf), lambda j, i: (layer, 0, j))
    return pl.pallas_call(
        functools.partial(_ffn_up_kernel, tm=tm, tiles_per_seq=seq // tm),
        grid=(f // tf, m // tm),
        in_specs=[pl.BlockSpec((tm, d), lambda j, i: (i, 0)),
                  wspec, wspec,
                  pl.BlockSpec((kw, tf), lambda j, i: (0, j)),
                  pl.BlockSpec((1, tf), lambda j, i: (0, j))],
        out_specs=pl.BlockSpec((tm, tf), lambda j, i: (i, j)),
        out_shape=jax.ShapeDtypeStruct((m, f), BF16),
        scratch_shapes=[pltpu.VMEM((tm + SUBLANES, tf), F32),
                        pltpu.VMEM((d, tf), BF16),
                        pltpu.VMEM((d, tf), BF16)],
        compiler_params=_params("arbitrary", "arbitrary"),
        name="ffn_up",
    )(h2, w_gate, w_up, conv_w, conv_b.reshape(1, f))


def _ffn_down_kernel(a_ref, wd_ref, x1_ref, g_ref, o_ref):
    o_ref[...] = x1_ref[...] + _rms(_dot(a_ref[...], wd_ref[...]), g_ref[...])


def _ffn_down(act, w_down, x1, gain, tm=256):
    m, f = act.shape
    d = w_down.shape[1]
    return pl.pallas_call(
        _ffn_down_kernel,
        grid=(m // tm,),
        in_specs=[pl.BlockSpec((tm, f), lambda i: (i, 0)),
                  pl.BlockSpec((f, d), lambda i: (0, 0), pipeline_mode=pl.Buffered(1)),
                  pl.BlockSpec((tm, d), lambda i: (i, 0)),
                  pl.BlockSpec((1, d), lambda i: (0, 0))],
        out_specs=pl.BlockSpec((tm, d), lambda i: (i, 0)),
        out_shape=jax.ShapeDtypeStruct((m, d), F32),
        compiler_params=_params("parallel"),
        name="ffn_down",
    )(act, w_down, x1, gain.reshape(1, d))


def kernel(x, norm_mix_pre, norm_mix_post, norm_ffn_pre, norm_ffn_post, w_in, b_in, hg_lb_logits, hg_norm,
           ml_conv_w, ml_conv_b, ml_norm, w_up_a, w_up_b, w_out, ffn_w_gate, ffn_w_up, ffn_conv_w, ffn_conv_b,
           ffn_w_down):
    batch, seq, d = x.shape
    depth = w_in.shape[0]
    hg_w = HG_HEADS * HG_HEAD_DIM
    qk_w = ML_HEADS * ML_QK_DIM
    v_w = ML_HEADS * ML_V_DIM
    main_w = 4 * hg_w + 2 * qk_w + 2 * v_w
    gate_off = main_w + 2 * ML_HEADS
    w_in_t = jnp.swapaxes(w_in, 1, 2)

    x2 = x.reshape(batch * seq, d)
    for l in range(depth):
        h1, p_small = _rmsnorm_cast(x2, norm_mix_pre[l], w_in_t, b_in[l, main_w:gate_off], layer=l, row0=main_w)
        p_main, w_down, w_g, w_ua, w_ub, w_o = _mm_bias_t(
            h1, w_in_t, b_in[l, :main_w], layer=l, row0=0, n=main_w, tm=1024, tn=1024, out_dtype=BF16,
            side_casts=((ffn_w_down, 0, ffn_w_down.shape[1], False), (w_in_t, gate_off, 2 * d, True),
                        (w_up_a, 0, hg_w, False), (w_up_b, 0, v_w, False), (w_out, 0, d, False)),
            name="in_proj_main")
        x2, h2 = _mixer(p_main, p_small, h1, w_g, b_in[l, gate_off:], x2, hg_lb_logits, hg_norm[l], ml_conv_w[l],
                        ml_conv_b[l], ml_norm[l], w_ua, w_ub, w_o, norm_mix_post[l], norm_ffn_pre[l],
                        seq=seq, layer=l)

        act = _ffn_up(h2, ffn_w_gate, ffn_w_up, ffn_conv_w[l], ffn_conv_b[l], layer=l, seq=seq)
        x2 = _ffn_down(act, w_down, x2, norm_ffn_post[l])
    return x2.reshape(batch, seq, d)
```

```python
import functools

import jax
import jax.numpy as jnp
from jax import lax
from jax.experimental import pallas as pl
from jax.experimental.pallas import tpu as pltpu

HG_HEADS = 8
HG_HEAD_DIM = 128
ML_HEADS = 4
ML_QK_DIM = 128
ML_V_DIM = 256
CHUNK = 64
EPS = 1e-6

F32 = jnp.float32
BF16 = jnp.bfloat16

SUBLANES = 8
LANES = 128
MXU_WIDTH = 256
MIB = 1024 * 1024
ROW_TILE = 1024
COL_TILE = 1024
FFN_COL_TILE = 512
SMALL_ROW_TILE = 256
VMEM_LIMIT_BYTES = 56 * MIB
MIXER_VMEM_LIMIT_BYTES = 63 * MIB + MIB // 2


def _params(*sem, vmem=VMEM_LIMIT_BYTES):
    return pltpu.CompilerParams(dimension_semantics=sem, vmem_limit_bytes=vmem)


def _dot(a, b):
    return jnp.dot(a, b, preferred_element_type=F32)


def _dot_nt(a, b):
    return lax.dot_general(a, b, (((1,), (1,)), ((), ())), preferred_element_type=F32)


def _dot_tn(a, b):
    return lax.dot_general(a, b, (((0,), (0,)), ((), ())), preferred_element_type=F32)


def _rms(u, gain):
    return u * lax.rsqrt(jnp.mean(u * u, axis=-1, keepdims=True) + EPS) * gain


def _rmsnorm_kernel(x_ref, g_ref, ws_ref, bs_ref, o_ref, ps_ref):
    h = _rms(x_ref[...], g_ref[...]).astype(o_ref.dtype)
    o_ref[...] = h
    ps_ref[...] = _dot_nt(h, ws_ref[...].astype(BF16)) + bs_ref[...]


def _rmsnorm_cast(x2, gain, wt, b_small, *, layer, row0, tm=ROW_TILE):
    m, d = x2.shape
    n = b_small.shape[0]
    row0 += layer * wt.shape[1]
    return pl.pallas_call(
        _rmsnorm_kernel,
        grid=(m // tm,),
        in_specs=[pl.BlockSpec((tm, d), lambda i: (i, 0)),
                  pl.BlockSpec((1, d), lambda i: (0, 0)),
                  pl.BlockSpec((pl.Element(n), pl.Element(d)), lambda i: (row0, 0)),
                  pl.BlockSpec((1, n), lambda i: (0, 0))],
        out_specs=[pl.BlockSpec((tm, d), lambda i: (i, 0)),
                   pl.BlockSpec((tm, n), lambda i: (i, 0))],
        out_shape=[jax.ShapeDtypeStruct((m, d), BF16),
                   jax.ShapeDtypeStruct((m, n), F32)],
        compiler_params=_params("parallel"),
        name="rmsnorm_cast",
    )(x2, gain.reshape(1, d), wt.reshape(-1, d), b_small.reshape(1, n))


def _mm_bias_kernel(h_ref, w_ref, b_ref, *rest, sigmoid, side_transpose):
    n_side = len(side_transpose)
    side_in, o_ref, side_out, wb_ref = rest[:n_side], rest[n_side], rest[n_side + 1:2 * n_side + 1], rest[-1]

    @pl.when(pl.program_id(1) == 0)
    def _():
        wb_ref[...] = w_ref[...].astype(BF16)

    for src, dst, transpose in zip(side_in, side_out, side_transpose):
        blk = src[...]
        dst[...] = (blk.T if transpose else blk).astype(dst.dtype)

    acc = _dot_nt(h_ref[...], wb_ref[...]) + b_ref[...]
    if sigmoid:
        acc = jax.nn.sigmoid(acc)
    o_ref[...] = acc.astype(o_ref.dtype)


def _mm_bias_t(h, wt, b, *, layer, row0, n, tm, tn, out_dtype, sigmoid=False, side_casts=(), name):
    m, k = h.shape
    row0 += layer * wt.shape[1]
    wt = wt.reshape(-1, k)
    n_i = m // tm
    steps = (n // tn) * n_i
    side_specs, side_out_specs, side_shapes, side_args = [], [], [], []
    for a, start, rows, transpose in side_casts:
        cols = a.shape[2]
        start += layer * a.shape[1]
        rb = next(r for r in range(LANES, rows + 1, LANES) if rows % r == 0 and rows // r <= steps)
        nb = rows // rb
        blk = lambda j, i, nb=nb: jnp.minimum(j * n_i + i, nb - 1)
        side_specs.append(pl.BlockSpec(
            (pl.Element(rb), pl.Element(cols)),
            lambda j, i, blk=blk, start=start, rb=rb: (pl.multiple_of(start + blk(j, i) * rb, SUBLANES), 0)))
        if transpose:
            side_out_specs.append(pl.BlockSpec((cols, rb), lambda j, i, blk=blk: (0, blk(j, i))))
            side_shapes.append(jax.ShapeDtypeStruct((cols, rows), BF16))
        else:
            side_out_specs.append(pl.BlockSpec((rb, cols), lambda j, i, blk=blk: (blk(j, i), 0)))
            side_shapes.append(jax.ShapeDtypeStruct((rows, cols), BF16))
        side_args.append(a.reshape(-1, cols))
    outs = pl.pallas_call(
        functools.partial(_mm_bias_kernel, sigmoid=sigmoid, side_transpose=tuple(c[3] for c in side_casts)),
        grid=(n // tn, n_i),
        in_specs=[pl.BlockSpec((tm, k), lambda j, i: (i, 0)),
                  pl.BlockSpec((pl.Element(tn), pl.Element(k)),
                               lambda j, i: (pl.multiple_of(row0 + j * tn, SUBLANES), 0)),
                  pl.BlockSpec((1, tn), lambda j, i: (0, j))] + side_specs,
        out_specs=[pl.BlockSpec((tm, tn), lambda j, i: (i, j))] + side_out_specs,
        out_shape=[jax.ShapeDtypeStruct((m, n), out_dtype)] + side_shapes,
        scratch_shapes=[pltpu.VMEM((tn, k), BF16)],
        compiler_params=_params("arbitrary", "arbitrary"),
        name=name,
    )(h, wt, b.reshape(1, n), *side_args)
    return outs if side_casts else outs[0]


def _tril(n):
    r = lax.broadcasted_iota(jnp.int32, (n, n), 0)
    c = lax.broadcasted_iota(jnp.int32, (n, n), 1)
    return r >= c


def _chunk_cumsum(x):
    pos = lax.broadcasted_iota(jnp.int32, x.shape, 0) & (CHUNK - 1)
    s = 1
    while s < CHUNK:
        x = x + jnp.where(pos >= s, pltpu.roll(x, s, axis=0), 0.0)
        s *= 2
    return x


def _chunk_slices(rows):
    return [slice(c * CHUNK, (c + 1) * CHUNK) for c in range(rows // CHUNK)]


def _hgrn2_head(q_pre, f_pre, v_in, g_pre, lb, gain, st_ref):
    q = jax.nn.silu(q_pre.astype(F32)) * (HG_HEAD_DIM ** -0.5)
    f = lb + (1.0 - lb) * jax.nn.sigmoid(f_pre.astype(F32))
    k = 1.0 - f
    b = _chunk_cumsum(jnp.log(f))
    v = v_in.astype(BF16)
    causal = _tril(CHUNK)
    q_dec = (q * jnp.exp(b)).astype(BF16)
    k_inv = k * jnp.exp(-b)
    k_inv_b = k_inv.astype(BF16)

    chunks = _chunk_slices(q_pre.shape[0])
    decay = [jnp.exp(b[sl.stop - 1:sl.stop, :]) for sl in chunks]
    k_end = [(k_inv[sl] * d).astype(BF16) for sl, d in zip(chunks, decay)]
    yield
    attn = [jnp.where(causal, _dot_nt(q_dec[sl], k_inv_b[sl]), 0.0).astype(BF16) for sl in chunks]
    kv = [_dot_tn(v[sl], ke) for sl, ke in zip(chunks, k_end)]
    yield
    o_intra = [_dot(a, v[sl]) for a, sl in zip(attn, chunks)]
    st = st_ref[...]
    outs = []
    for c, sl in enumerate(chunks):
        outs.append(o_intra[c] + _dot_nt(q_dec[sl], st.astype(BF16)))
        st = decay[c] * st + kv[c]
    st_ref[...] = st
    yield
    o = jnp.concatenate(outs, axis=0)
    return _rms(o, gain) * jax.nn.silu(g_pre.astype(F32))


def _mlstm_head(q, k, v, o_pre, ig_col, b_col, gain, ct_ref, m_prev):
    rows = q.shape[0]
    qb = q.astype(BF16)
    kb = k.astype(BF16)
    v_aug = jnp.concatenate([v.astype(BF16), jnp.ones((rows, ML_QK_DIM), BF16)], axis=1)
    wide = (rows, ML_QK_DIM)
    b = jnp.broadcast_to(b_col, wide)
    igb = jnp.broadcast_to(ig_col - b_col, wide)
    causal = _tril(CHUNK)
    chunks = _chunk_slices(rows)
    n_chunks = len(chunks)

    b_last = [b[sl.stop - 1:sl.stop, 0:1] for sl in chunks]
    log_d = [jnp.where(causal, b[sl, :CHUNK] + igb[sl].T[:CHUNK, :], -jnp.inf) for sl in chunks]
    m_intra = [jnp.max(ld, axis=1, keepdims=True) for ld in log_d]
    logw = [bl + igb[sl, 0:1] for bl, sl in zip(b_last, chunks)]
    logw_max = [jnp.max(lw, axis=0, keepdims=True) for lw in logw]
    qk = [_dot_nt(qb[sl], kb[sl]) for sl in chunks]
    yield

    m_in, m_out = [], []
    for c in range(n_chunks):
        m_in.append(m_prev)
        m_prev = jnp.maximum(b_last[c] + m_prev, logw_max[c])
        m_out.append(m_prev)

    m_inter = [b[sl, 0:1] + mi for sl, mi in zip(chunks, m_in)]
    m_tot = [jnp.maximum(a, bb) for a, bb in zip(m_inter, m_intra)]
    scores = [(qk[c] * jnp.exp(log_d[c] - m_tot[c])).astype(BF16) for c in range(n_chunks)]
    k_w = [(k[sl] * jnp.exp(logw[c] - m_out[c])).astype(BF16) for c, sl in enumerate(chunks)]
    yield
    intra = [_dot(scores[c], v_aug[sl]) for c, sl in enumerate(chunks)]
    kv = [_dot_tn(v_aug[sl], k_w[c]) for c, sl in enumerate(chunks)]
    yield
    ct = ct_ref[...]
    outs = []
    for c, sl in enumerate(chunks):
        tot = intra[c] + jnp.exp(m_inter[c] - m_tot[c]) * _dot_nt(qb[sl], ct.astype(BF16))
        num = tot[:, :ML_V_DIM]
        den = tot[:, ML_V_DIM:ML_V_DIM + 1]
        outs.append(num / jnp.maximum(jnp.abs(den), jnp.exp(-m_tot[c])))
        ct = jnp.exp(b_last[c] + m_in[c] - m_out[c]) * ct + kv[c]
    ct_ref[...] = ct
    yield
    hh = jnp.concatenate(outs, axis=0)
    return jax.nn.sigmoid(o_pre.astype(F32)) * _rms(hh, gain), m_prev


def _interleave(*lanes):
    lanes = [iter(lane) for lane in lanes]
    while lanes:
        for lane in list(lanes):
            try:
                next(lane)
            except StopIteration:
                lanes.remove(lane)


def _idle(n):
    for _ in range(n):
        yield


def _chain(*gens):
    for g in gens:
        yield from g


def _causal_conv(buf_ref, x, w_ref, b_ref):
    kw = w_ref.shape[0]
    rows = x.shape[0]
    buf_ref[SUBLANES:SUBLANES + rows, :] = x
    acc = b_ref[...] + w_ref[kw - 1:kw, :] * x
    for j in range(kw - 1):
        s = kw - 1 - j
        acc = acc + w_ref[j:j + 1, :] * buf_ref[SUBLANES - s:SUBLANES - s + rows, :]
    buf_ref[0:SUBLANES, :] = buf_ref[rows:rows + SUBLANES, :]
    return acc


def _mixer_kernel(pm_ref, ps_ref, lbl_ref, hgn_ref, cw_ref, cb_ref, mln_ref,
                  h1_ref, wg_ref, bg_ref, x_ref, wua_ref, wub_ref, wout_ref, gpost_ref, gpre_ref,
                  x1_ref, h2_ref,
                  st_ref, ct_ref, m_ref, qkbuf_ref, ya_ref, yb_ref, *, layer, tiles_per_seq, n_tiles):
    s = pl.program_id(0)
    r = jnp.minimum(s, n_tiles - 1)

    @pl.when(s == 0)
    def _():
        ya_ref[...] = jnp.zeros_like(ya_ref)
        yb_ref[...] = jnp.zeros_like(yb_ref)

    @pl.when(r % tiles_per_seq == 0)
    def _():
        st_ref[...] = jnp.zeros_like(st_ref)
        ct_ref[...] = jnp.zeros_like(ct_ref)
        m_ref[...] = jnp.zeros_like(m_ref)
        qkbuf_ref[0:SUBLANES, :] = jnp.zeros((SUBLANES, qkbuf_ref.shape[1]), F32)

    d = x_ref.shape[1]
    hg_w = HG_HEADS * HG_HEAD_DIM
    qk_w = ML_HEADS * ML_QK_DIM
    qk_off = 4 * hg_w
    v_off = qk_off + 2 * qk_w
    o_off = v_off + ML_HEADS * ML_V_DIM
    n_pieces = d // MXU_WIDTH
    pw = d // n_pieces
    merged = [None] * n_pieces
    mix = [None] * n_pieces
    ya_new = [None] * HG_HEADS
    yb_new = [None] * ML_HEADS

    def gate(cs):
        return jax.nn.sigmoid(_dot(h1_ref[...], wg_ref[:, cs]) + bg_ref[:, cs])

    def lift_lane():
        for j in range(n_pieces):
            cs = slice(j * pw, (j + 1) * pw)
            part_a = gate(cs) * _dot(ya_ref[...], wua_ref[:, cs])
            yield
            part_b = gate(slice(d + j * pw, d + (j + 1) * pw)) * _dot(yb_ref[...], wub_ref[:, cs])
            merged[j] = (part_a + part_b).astype(BF16)
            yield

    def outproj_lane():
        lhs = jnp.concatenate(merged, axis=1)
        for j in range(n_pieces):
            mix[j] = _dot(lhs, wout_ref[:, j * pw:(j + 1) * pw])
            yield

    def hgrn2_piece(h):
        cols = [slice(j * hg_w + h * HG_HEAD_DIM, j * hg_w + (h + 1) * HG_HEAD_DIM) for j in range(4)]
        hs = slice(h * HG_HEAD_DIM, (h + 1) * HG_HEAD_DIM)
        y = yield from _hgrn2_head(pm_ref[:, cols[0]], pm_ref[:, cols[1]], pm_ref[:, cols[2]],
                                   pm_ref[:, cols[3]], lb[:, hs], hgn_ref[:, hs], st_ref.at[h])
        ya_new[h] = y.astype(ya_ref.dtype)

    def mlstm_piece(h):
        qs = slice(h * ML_QK_DIM, (h + 1) * ML_QK_DIM)
        ks = slice(qk_w + h * ML_QK_DIM, qk_w + (h + 1) * ML_QK_DIM)
        vs = slice(h * ML_V_DIM, (h + 1) * ML_V_DIM)
        y, m_new = yield from _mlstm_head(qk[:, qs], qk[:, ks] * (ML_QK_DIM ** -0.5),
                                          pm_ref[:, v_off + vs.start:v_off + vs.stop],
                                          pm_ref[:, o_off + vs.start:o_off + vs.stop],
                                          gates[:, h:h + 1], gates_b[:, ML_HEADS + h:ML_HEADS + h + 1],
                                          mln_ref[:, vs], ct_ref.at[h], m_ref[h, 0:1, 0:1])
        m_ref[h] = jnp.broadcast_to(m_new, m_ref.shape[1:])
        yb_new[h] = y.astype(yb_ref.dtype)

    logits = lbl_ref[...]
    e = jnp.exp(logits - jnp.max(logits, axis=0, keepdims=True))
    sm = e / jnp.sum(e, axis=0, keepdims=True)
    lb = jnp.sum(sm[:layer + 1], axis=0, keepdims=True)
    qk = jax.nn.silu(_causal_conv(qkbuf_ref, pm_ref[:, qk_off:qk_off + 2 * qk_w].astype(F32), cw_ref, cb_ref))
    gates = ps_ref[...]
    gates_b = _chunk_cumsum(jax.nn.log_sigmoid(gates))

    _interleave(lift_lane(),
                _chain(*[hgrn2_piece(h) for h in range(0, HG_HEADS, 2)], mlstm_piece(0)),
                _chain(_idle(2), *[hgrn2_piece(h) for h in range(1, HG_HEADS, 2)], mlstm_piece(1)))
    _interleave(outproj_lane(),
                mlstm_piece(2),
                _chain(_idle(2), mlstm_piece(3)))
    x1 = x_ref[...] + _rms(jnp.concatenate(mix, axis=1), gpost_ref[...])
    x1_ref[...] = x1
    h2_ref[...] = _rms(x1, gpre_ref[...]).astype(h2_ref.dtype)
    ya_ref[...] = jnp.concatenate(ya_new, axis=1)
    yb_ref[...] = jnp.concatenate(yb_new, axis=1)


def _mixer(p_main, p_small, h1, w_gate, b_gate, x2, lb_logits, hg_gain, conv_w, conv_b, ml_gain, w_ua, w_ub, w_out,
           g_post, g_pre, *, seq, layer, tm=SMALL_ROW_TILE):
    m, d = x2.shape
    n_tiles = m // tm
    hg_w = HG_HEADS * HG_HEAD_DIM
    qk_w = ML_HEADS * ML_QK_DIM
    v_w = ML_HEADS * ML_V_DIM
    const = lambda s: (0, 0)
    cur = lambda s: (jnp.minimum(s, n_tiles - 1), 0)
    prev = lambda s: (jnp.maximum(s - 1, 0), 0)
    single = pl.Buffered(1)
    return pl.pallas_call(
        functools.partial(_mixer_kernel, layer=layer, tiles_per_seq=seq // tm, n_tiles=n_tiles),
        grid=(n_tiles + 1,),
        in_specs=[pl.BlockSpec((tm, p_main.shape[1]), cur),
                  pl.BlockSpec((tm, p_small.shape[1]), cur),
                  pl.BlockSpec(lb_logits.shape, const),
                  pl.BlockSpec((1, hg_w), const),
                  pl.BlockSpec(conv_w.shape, const),
                  pl.BlockSpec((1, 2 * qk_w), const),
                  pl.BlockSpec((1, v_w), const),
                  pl.BlockSpec((tm, d), prev),
                  pl.BlockSpec((d, 2 * d), const, pipeline_mode=single),
                  pl.BlockSpec((1, 2 * d), const),
                  pl.BlockSpec((tm, d), prev),
                  pl.BlockSpec((hg_w, d), const, pipeline_mode=single),
                  pl.BlockSpec((v_w, d), const, pipeline_mode=single),
                  pl.BlockSpec((d, d), const, pipeline_mode=single),
                  pl.BlockSpec((1, d), const),
                  pl.BlockSpec((1, d), const)],
        out_specs=[pl.BlockSpec((tm, d), prev),
                   pl.BlockSpec((tm, d), prev)],
        out_shape=[jax.ShapeDtypeStruct((m, d), F32),
                   jax.ShapeDtypeStruct((m, d), BF16)],
        scratch_shapes=[pltpu.VMEM((HG_HEADS, HG_HEAD_DIM, HG_HEAD_DIM), F32),
                        pltpu.VMEM((ML_HEADS, ML_V_DIM + ML_QK_DIM, ML_QK_DIM), F32),
                        pltpu.VMEM((ML_HEADS, SUBLANES, LANES), F32),
                        pltpu.VMEM((tm + SUBLANES, 2 * qk_w), F32),
                        pltpu.VMEM((tm, hg_w), BF16),
                        pltpu.VMEM((tm, v_w), BF16)],
        compiler_params=_params("arbitrary", vmem=MIXER_VMEM_LIMIT_BYTES),
        name="mixer",
    )(p_main, p_small, lb_logits, hg_gain.reshape(1, hg_w), conv_w, conv_b.reshape(1, 2 * qk_w),
      ml_gain.reshape(1, v_w), h1, w_gate, b_gate.reshape(1, 2 * d), x2, w_ua, w_ub, w_out,
      g_post.reshape(1, d), g_pre.reshape(1, d))


def _ffn_up_kernel(h_ref, wg_ref, wu_ref, cw_ref, cb_ref, a_ref, gbuf_ref, wgb_ref, wub_ref, *, tm, tiles_per_seq):
    kw = cw_ref.shape[0]

    @pl.when(pl.program_id(1) == 0)
    def _():
        wgb_ref[...] = wg_ref[...].astype(BF16)
        wub_ref[...] = wu_ref[...].astype(BF16)

    @pl.when(pl.program_id(1) % tiles_per_seq == 0)
    def _():
        gbuf_ref[0:SUBLANES, :] = jnp.zeros((SUBLANES, gbuf_ref.shape[1]), F32)

    gbuf_ref[SUBLANES:SUBLANES + tm, :] = _dot(h_ref[...], wgb_ref[...])
    acc = cb_ref[...] + cw_ref[kw - 1:kw, :] * gbuf_ref[SUBLANES:SUBLANES + tm, :]
    for j in range(kw - 1):
        s = kw - 1 - j
        acc = acc + cw_ref[j:j + 1, :] * gbuf_ref[SUBLANES - s:SUBLANES - s + tm, :]
    gbuf_ref[0:SUBLANES, :] = gbuf_ref[tm:tm + SUBLANES, :]
    a_ref[...] = (jax.nn.gelu(acc, approximate=True) * _dot(h_ref[...], wub_ref[...])).astype(a_ref.dtype)


def _ffn_up(h2, w_gate, w_up, conv_w, conv_b, *, layer, seq, tm=ROW_TILE, tf=FFN_COL_TILE):
    m, d = h2.shape
    f = w_gate.shape[2]
    kw = conv_w.shape[0]
    wspec = pl.BlockSpec((None, d, tf), lambda j, i: (layer, 0, j))
    return pl.pallas_call(
        functools.partial(_ffn_up_kernel, tm=tm, tiles_per_seq=seq // tm),
        grid=(f // tf, m // tm),
        in_specs=[pl.BlockSpec((tm, d), lambda j, i: (i, 0)),
                  wspec, wspec,
                  pl.BlockSpec((kw, tf), lambda j, i: (0, j)),
                  pl.BlockSpec((1, tf), lambda j, i: (0, j))],
        out_specs=pl.BlockSpec((tm, tf), lambda j, i: (i, j)),
        out_shape=jax.ShapeDtypeStruct((m, f), BF16),
        scratch_shapes=[pltpu.VMEM((tm + SUBLANES, tf), F32),
                        pltpu.VMEM((d, tf), BF16),
                        pltpu.VMEM((d, tf), BF16)],
        compiler_params=_params("arbitrary", "arbitrary"),
        name="ffn_up",
    )(h2, w_gate, w_up, conv_w, conv_b.reshape(1, f))


def _ffn_down_kernel(a_ref, wd_ref, x1_ref, g_ref, o_ref):
    o_ref[...] = x1_ref[...] + _rms(_dot(a_ref[...], wd_ref[...]), g_ref[...])


def _ffn_down(act, w_down, x1, gain, tm=SMALL_ROW_TILE):
    m, f = act.shape
    d = w_down.shape[1]
    return pl.pallas_call(
        _ffn_down_kernel,
        grid=(m // tm,),
        in_specs=[pl.BlockSpec((tm, f), lambda i: (i, 0)),
                  pl.BlockSpec((f, d), lambda i: (0, 0), pipeline_mode=pl.Buffered(1)),
                  pl.BlockSpec((tm, d), lambda i: (i, 0)),
                  pl.BlockSpec((1, d), lambda i: (0, 0))],
        out_specs=pl.BlockSpec((tm, d), lambda i: (i, 0)),
        out_shape=jax.ShapeDtypeStruct((m, d), F32),
        compiler_params=_params("parallel"),
        name="ffn_down",
    )(act, w_down, x1, gain.reshape(1, d))


def kernel(x, norm_mix_pre, norm_mix_post, norm_ffn_pre, norm_ffn_post, w_in, b_in, hg_lb_logits, hg_norm,
           ml_conv_w, ml_conv_b, ml_norm, w_up_a, w_up_b, w_out, ffn_w_gate, ffn_w_up, ffn_conv_w, ffn_conv_b,
           ffn_w_down):
    batch, seq, d = x.shape
    depth = w_in.shape[0]
    hg_w = HG_HEADS * HG_HEAD_DIM
    qk_w = ML_HEADS * ML_QK_DIM
    v_w = ML_HEADS * ML_V_DIM
    main_w = 4 * hg_w + 2 * qk_w + 2 * v_w
    gate_off = main_w + 2 * ML_HEADS
    assert w_in.shape[2] == gate_off + 2 * d and w_up_a.shape[1] == hg_w and w_up_b.shape[1] == v_w
    assert seq % ROW_TILE == 0 and main_w % COL_TILE == 0 and ffn_w_gate.shape[2] % FFN_COL_TILE == 0
    w_in_t = jnp.swapaxes(w_in, 1, 2)

    x2 = x.reshape(batch * seq, d)
    for l in range(depth):
        h1, p_small = _rmsnorm_cast(x2, norm_mix_pre[l], w_in_t, b_in[l, main_w:gate_off], layer=l, row0=main_w)
        p_main, w_down, w_g, w_ua, w_ub, w_o = _mm_bias_t(
            h1, w_in_t, b_in[l, :main_w], layer=l, row0=0, n=main_w, tm=ROW_TILE, tn=COL_TILE, out_dtype=BF16,
            side_casts=((ffn_w_down, 0, ffn_w_down.shape[1], False), (w_in_t, gate_off, 2 * d, True),
                        (w_up_a, 0, hg_w, False), (w_up_b, 0, v_w, False), (w_out, 0, d, False)),
            name="in_proj_main")
        x2, h2 = _mixer(p_main, p_small, h1, w_g, b_in[l, gate_off:], x2, hg_lb_logits, hg_norm[l], ml_conv_w[l],
                        ml_conv_b[l], ml_norm[l], w_ua, w_ub, w_o, norm_mix_post[l], norm_ffn_pre[l],
                        seq=seq, layer=l)

        act = _ffn_up(h2, ffn_w_gate, ffn_w_up, ffn_conv_w[l], ffn_conv_b[l], layer=l, seq=seq)
        x2 = _ffn_down(act, w_down, x2, norm_ffn_post[l])
    return x2.reshape(batch, seq, d)
```

```python
import functools

import jax
import jax.numpy as jnp
from jax import lax
from jax.experimental import pallas as pl
from jax.experimental.pallas import tpu as pltpu

HG_HEADS = 8
HG_HEAD_DIM = 128
ML_HEADS = 4
ML_QK_DIM = 128
ML_V_DIM = 256
CHUNK = 64
EPS = 1e-6

F32 = jnp.float32
BF16 = jnp.bfloat16

SUBLANES = 8
LANES = 128
MXU_WIDTH = 256
MIB = 1024 * 1024
ROW_TILE = 1024
COL_TILE = 1024
FFN_COL_TILE = 512
SMALL_ROW_TILE = 256
VMEM_LIMIT_BYTES = 56 * MIB
MIXER_VMEM_LIMIT_BYTES = 63 * MIB + MIB // 2


def _params(*sem, vmem=VMEM_LIMIT_BYTES):
    return pltpu.CompilerParams(dimension_semantics=sem, vmem_limit_bytes=vmem)


def _dot(a, b):
    return jnp.dot(a, b, preferred_element_type=F32)


def _dot_nt(a, b):
    return lax.dot_general(a, b, (((1,), (1,)), ((), ())), preferred_element_type=F32)


def _dot_tn(a, b):
    return lax.dot_general(a, b, (((0,), (0,)), ((), ())), preferred_element_type=F32)


def _rms(u, gain):
    return u * lax.rsqrt(jnp.mean(u * u, axis=-1, keepdims=True) + EPS) * gain


def _rmsnorm_kernel(x_ref, g_ref, ws_ref, bs_ref, o_ref, ps_ref):
    h = _rms(x_ref[...], g_ref[...]).astype(o_ref.dtype)
    o_ref[...] = h
    ps_ref[...] = _dot_nt(h, ws_ref[...].astype(BF16)) + bs_ref[...]


def _rmsnorm_cast(x2, gain, wt, b_small, *, layer, row0, tm=ROW_TILE):
    m, d = x2.shape
    n = b_small.shape[0]
    row0 += layer * wt.shape[1]
    return pl.pallas_call(
        _rmsnorm_kernel,
        grid=(m // tm,),
        in_specs=[pl.BlockSpec((tm, d), lambda i: (i, 0)),
                  pl.BlockSpec((1, d), lambda i: (0, 0)),
                  pl.BlockSpec((pl.Element(n), pl.Element(d)), lambda i: (row0, 0)),
                  pl.BlockSpec((1, n), lambda i: (0, 0))],
        out_specs=[pl.BlockSpec((tm, d), lambda i: (i, 0)),
                   pl.BlockSpec((tm, n), lambda i: (i, 0))],
        out_shape=[jax.ShapeDtypeStruct((m, d), BF16),
                   jax.ShapeDtypeStruct((m, n), F32)],
        compiler_params=_params("parallel"),
        name="rmsnorm_cast",
    )(x2, gain.reshape(1, d), wt.reshape(-1, d), b_small.reshape(1, n))


def _mm_bias_kernel(h_ref, w_ref, b_ref, *rest, sigmoid, side_transpose):
    n_side = len(side_transpose)
    side_in, o_ref, side_out, wb_ref = rest[:n_side], rest[n_side], rest[n_side + 1:2 * n_side + 1], rest[-1]

    @pl.when(pl.program_id(1) == 0)
    def _():
        wb_ref[...] = w_ref[...].astype(BF16)

    for src, dst, transpose in zip(side_in, side_out, side_transpose):
        blk = src[...]
        dst[...] = (blk.T if transpose else blk).astype(dst.dtype)

    acc = _dot_nt(h_ref[...], wb_ref[...]) + b_ref[...]
    if sigmoid:
        acc = jax.nn.sigmoid(acc)
    o_ref[...] = acc.astype(o_ref.dtype)


def _mm_bias_t(h, wt, b, *, layer, row0, n, tm, tn, out_dtype, sigmoid=False, side_casts=(), name):
    m, k = h.shape
    row0 += layer * wt.shape[1]
    wt = wt.reshape(-1, k)
    n_i = m // tm
    steps = (n // tn) * n_i
    side_specs, side_out_specs, side_shapes, side_args = [], [], [], []
    for a, start, rows, transpose in side_casts:
        cols = a.shape[2]
        start += layer * a.shape[1]
        rb = next(r for r in range(LANES, rows + 1, LANES) if rows % r == 0 and rows // r <= steps)
        nb = rows // rb
        blk = lambda j, i, nb=nb: jnp.minimum(j * n_i + i, nb - 1)
        side_specs.append(pl.BlockSpec(
            (pl.Element(rb), pl.Element(cols)),
            lambda j, i, blk=blk, start=start, rb=rb: (pl.multiple_of(start + blk(j, i) * rb, SUBLANES), 0)))
        if transpose:
            side_out_specs.append(pl.BlockSpec((cols, rb), lambda j, i, blk=blk: (0, blk(j, i))))
            side_shapes.append(jax.ShapeDtypeStruct((cols, rows), BF16))
        else:
            side_out_specs.append(pl.BlockSpec((rb, cols), lambda j, i, blk=blk: (blk(j, i), 0)))
            side_shapes.append(jax.ShapeDtypeStruct((rows, cols), BF16))
        side_args.append(a.reshape(-1, cols))
    outs = pl.pallas_call(
        functools.partial(_mm_bias_kernel, sigmoid=sigmoid, side_transpose=tuple(c[3] for c in side_casts)),
        grid=(n // tn, n_i),
        in_specs=[pl.BlockSpec((tm, k), lambda j, i: (i, 0)),
                  pl.BlockSpec((pl.Element(tn), pl.Element(k)),
                               lambda j, i: (pl.multiple_of(row0 + j * tn, SUBLANES), 0)),
                  pl.BlockSpec((1, tn), lambda j, i: (0, j))] + side_specs,
        out_specs=[pl.BlockSpec((tm, tn), lambda j, i: (i, j))] + side_out_specs,
        out_shape=[jax.ShapeDtypeStruct((m, n), out_dtype)] + side_shapes,
        scratch_shapes=[pltpu.VMEM((tn, k), BF16)],
        compiler_params=_params("arbitrary", "arbitrary"),
        name=name,
    )(h, wt, b.reshape(1, n), *side_args)
    return outs if side_casts else outs[0]


def _tril(n):
    r = lax.broadcasted_iota(jnp.int32, (n, n), 0)
    c = lax.broadcasted_iota(jnp.int32, (n, n), 1)
    return r >= c


def _chunk_cumsum(x):
    pos = lax.broadcasted_iota(jnp.int32, x.shape, 0) & (CHUNK - 1)
    s = 1
    while s < CHUNK:
        x = x + jnp.where(pos >= s, pltpu.roll(x, s, axis=0), 0.0)
        s *= 2
    return x


def _chunk_slices(rows):
    return [slice(c * CHUNK, (c + 1) * CHUNK) for c in range(rows // CHUNK)]


def _hgrn2_head(q_pre, f_pre, v_in, g_pre, lb, gain, st_ref):
    q = jax.nn.silu(q_pre.astype(F32)) * (HG_HEAD_DIM ** -0.5)
    f = lb + (1.0 - lb) * jax.nn.sigmoid(f_pre.astype(F32))
    k = 1.0 - f
    b = _chunk_cumsum(jnp.log(f))
    v = v_in.astype(BF16)
    causal = _tril(CHUNK)
    q_dec = (q * jnp.exp(b)).astype(BF16)
    k_inv = k * jnp.exp(-b)
    k_inv_b = k_inv.astype(BF16)

    chunks = _chunk_slices(q_pre.shape[0])
    decay = [jnp.exp(b[sl.stop - 1:sl.stop, :]) for sl in chunks]
    k_end = [(k_inv[sl] * d).astype(BF16) for sl, d in zip(chunks, decay)]
    yield
    attn = [jnp.where(causal, _dot_nt(q_dec[sl], k_inv_b[sl]), 0.0).astype(BF16) for sl in chunks]
    kv = [_dot_tn(v[sl], ke) for sl, ke in zip(chunks, k_end)]
    yield
    o_intra = [_dot(a, v[sl]) for a, sl in zip(attn, chunks)]
    st = st_ref[...]
    outs = []
    for c, sl in enumerate(chunks):
        outs.append(o_intra[c] + _dot_nt(q_dec[sl], st.astype(BF16)))
        st = decay[c] * st + kv[c]
    st_ref[...] = st
    yield
    o = jnp.concatenate(outs, axis=0)
    return _rms(o, gain) * jax.nn.silu(g_pre.astype(F32))


def _mlstm_head(q, k, v, o_pre, ig_col, b_col, gain, c_ref, n_ref, m_prev):
    rows = q.shape[0]
    qb = q.astype(BF16)
    kb = k.astype(BF16)
    vb = v.astype(BF16)
    wide = (rows, ML_QK_DIM)
    b = jnp.broadcast_to(b_col, wide)
    igb = jnp.broadcast_to(ig_col - b_col, wide)
    causal = _tril(CHUNK)
    chunks = _chunk_slices(rows)
    n_chunks = len(chunks)

    b_last = [b[sl.stop - 1:sl.stop, 0:1] for sl in chunks]
    log_d = [jnp.where(causal, b[sl, :CHUNK] + igb[sl].T[:CHUNK, :], -jnp.inf) for sl in chunks]
    m_intra = [jnp.max(ld, axis=1, keepdims=True) for ld in log_d]
    logw = [bl + igb[sl, 0:1] for bl, sl in zip(b_last, chunks)]
    logw_max = [jnp.max(lw, axis=0, keepdims=True) for lw in logw]
    qk = [_dot_nt(qb[sl], kb[sl]) for sl in chunks]
    yield

    m_in, m_out = [], []
    for c in range(n_chunks):
        m_in.append(m_prev)
        m_prev = jnp.maximum(b_last[c] + m_prev, logw_max[c])
        m_out.append(m_prev)

    m_inter = [b[sl, 0:1] + mi for sl, mi in zip(chunks, m_in)]
    m_tot = [jnp.maximum(a, bb) for a, bb in zip(m_inter, m_intra)]
    scores = [qk[c] * jnp.exp(log_d[c] - m_tot[c]) for c in range(n_chunks)]
    den_intra = [jnp.sum(sc, axis=1, keepdims=True) for sc in scores]
    k_w = [k[sl] * jnp.exp(logw[c] - m_out[c]) for c, sl in enumerate(chunks)]
    k_sum = [jnp.sum(kw, axis=0, keepdims=True) for kw in k_w]
    yield
    intra = [_dot(scores[c].astype(BF16), vb[sl]) for c, sl in enumerate(chunks)]
    kv = [_dot_tn(k_w[c].astype(BF16), vb[sl]) for c, sl in enumerate(chunks)]
    yield
    cm = c_ref[...]
    n = n_ref[0:1, :]
    outs = []
    for c, sl in enumerate(chunks):
        inter_scale = jnp.exp(m_inter[c] - m_tot[c])
        num = intra[c] + inter_scale * _dot(qb[sl], cm.astype(BF16))
        den = den_intra[c] + inter_scale * jnp.sum(q[sl] * n, axis=1, keepdims=True)
        outs.append(num / jnp.maximum(jnp.abs(den), jnp.exp(-m_tot[c])))
        decay = jnp.exp(b_last[c] + m_in[c] - m_out[c])
        cm = decay * cm + kv[c]
        n = decay * n + k_sum[c]
    c_ref[...] = cm
    n_ref[...] = jnp.broadcast_to(n, n_ref.shape)
    yield
    hh = jnp.concatenate(outs, axis=0)
    return jax.nn.sigmoid(o_pre.astype(F32)) * _rms(hh, gain), m_prev


def _interleave(*lanes):
    lanes = [iter(lane) for lane in lanes]
    while lanes:
        for lane in list(lanes):
            try:
                next(lane)
            except StopIteration:
                lanes.remove(lane)


def _idle(n):
    for _ in range(n):
        yield


def _chain(*gens):
    for g in gens:
        yield from g


def _causal_conv(buf_ref, x, w_ref, b_ref):
    kw = w_ref.shape[0]
    rows = x.shape[0]
    buf_ref[SUBLANES:SUBLANES + rows, :] = x
    acc = b_ref[...] + w_ref[kw - 1:kw, :] * x
    for j in range(kw - 1):
        s = kw - 1 - j
        acc = acc + w_ref[j:j + 1, :] * buf_ref[SUBLANES - s:SUBLANES - s + rows, :]
    buf_ref[0:SUBLANES, :] = buf_ref[rows:rows + SUBLANES, :]
    return acc


def _mixer_kernel(pm_ref, ps_ref, lbl_ref, hgn_ref, cw_ref, cb_ref, mln_ref,
                  h1_ref, wg_ref, bg_ref, x_ref, wua_ref, wub_ref, wout_ref, gpost_ref, gpre_ref,
                  x1_ref, h2_ref,
                  st_ref, c_ref, n_ref, m_ref, qkbuf_ref, ya_ref, yb_ref, *, layer, tiles_per_seq, n_tiles):
    s = pl.program_id(0)
    r = jnp.minimum(s, n_tiles - 1)

    @pl.when(s == 0)
    def _():
        ya_ref[...] = jnp.zeros_like(ya_ref)
        yb_ref[...] = jnp.zeros_like(yb_ref)

    @pl.when(r % tiles_per_seq == 0)
    def _():
        st_ref[...] = jnp.zeros_like(st_ref)
        c_ref[...] = jnp.zeros_like(c_ref)
        n_ref[...] = jnp.zeros_like(n_ref)
        m_ref[...] = jnp.zeros_like(m_ref)
        qkbuf_ref[0:SUBLANES, :] = jnp.zeros((SUBLANES, qkbuf_ref.shape[1]), F32)

    d = x_ref.shape[1]
    hg_w = HG_HEADS * HG_HEAD_DIM
    qk_w = ML_HEADS * ML_QK_DIM
    qk_off = 4 * hg_w
    v_off = qk_off + 2 * qk_w
    o_off = v_off + ML_HEADS * ML_V_DIM
    n_pieces = d // MXU_WIDTH
    pw = d // n_pieces
    merged = [None] * n_pieces
    mix = [None] * n_pieces
    ya_new = [None] * HG_HEADS
    yb_new = [None] * ML_HEADS

    def gate(cs):
        return jax.nn.sigmoid(_dot(h1_ref[...], wg_ref[:, cs]) + bg_ref[:, cs])

    def lift_lane():
        for j in range(n_pieces):
            cs = slice(j * pw, (j + 1) * pw)
            part_a = gate(cs) * _dot(ya_ref[...], wua_ref[:, cs])
            yield
            part_b = gate(slice(d + j * pw, d + (j + 1) * pw)) * _dot(yb_ref[...], wub_ref[:, cs])
            merged[j] = (part_a + part_b).astype(BF16)
            yield

    def outproj_lane():
        lhs = jnp.concatenate(merged, axis=1)
        for j in range(n_pieces):
            mix[j] = _dot(lhs, wout_ref[:, j * pw:(j + 1) * pw])
            yield

    def hgrn2_piece(h):
        cols = [slice(j * hg_w + h * HG_HEAD_DIM, j * hg_w + (h + 1) * HG_HEAD_DIM) for j in range(4)]
        hs = slice(h * HG_HEAD_DIM, (h + 1) * HG_HEAD_DIM)
        y = yield from _hgrn2_head(pm_ref[:, cols[0]], pm_ref[:, cols[1]], pm_ref[:, cols[2]],
                                   pm_ref[:, cols[3]], lb[:, hs], hgn_ref[:, hs], st_ref.at[h])
        ya_new[h] = y.astype(ya_ref.dtype)

    def mlstm_piece(h):
        qs = slice(h * ML_QK_DIM, (h + 1) * ML_QK_DIM)
        ks = slice(qk_w + h * ML_QK_DIM, qk_w + (h + 1) * ML_QK_DIM)
        vs = slice(h * ML_V_DIM, (h + 1) * ML_V_DIM)
        y, m_new = yield from _mlstm_head(qk[:, qs], qk[:, ks] * (ML_QK_DIM ** -0.5),
                                          pm_ref[:, v_off + vs.start:v_off + vs.stop],
                                          pm_ref[:, o_off + vs.start:o_off + vs.stop],
                                          gates[:, h:h + 1], gates_b[:, ML_HEADS + h:ML_HEADS + h + 1],
                                          mln_ref[:, vs], c_ref.at[h], n_ref.at[h], m_ref[h, 0:1, 0:1])
        m_ref[h] = jnp.broadcast_to(m_new, m_ref.shape[1:])
        yb_new[h] = y.astype(yb_ref.dtype)

    logits = lbl_ref[...]
    e = jnp.exp(logits - jnp.max(logits, axis=0, keepdims=True))
    sm = e / jnp.sum(e, axis=0, keepdims=True)
    lb = jnp.sum(sm[:layer + 1], axis=0, keepdims=True)
    qk = jax.nn.silu(_causal_conv(qkbuf_ref, pm_ref[:, qk_off:qk_off + 2 * qk_w].astype(F32), cw_ref, cb_ref))
    gates = ps_ref[...]
    gates_b = _chunk_cumsum(jax.nn.log_sigmoid(gates))

    _interleave(lift_lane(),
                _chain(*[hgrn2_piece(h) for h in range(0, HG_HEADS, 2)], mlstm_piece(0)),
                _chain(_idle(2), *[hgrn2_piece(h) for h in range(1, HG_HEADS, 2)], mlstm_piece(1)))
    _interleave(outproj_lane(),
                mlstm_piece(2),
                _chain(_idle(2), mlstm_piece(3)))
    x1 = x_ref[...] + _rms(jnp.concatenate(mix, axis=1), gpost_ref[...])
    x1_ref[...] = x1
    h2_ref[...] = _rms(x1, gpre_ref[...]).astype(h2_ref.dtype)
    ya_ref[...] = jnp.concatenate(ya_new, axis=1)
    yb_ref[...] = jnp.concatenate(yb_new, axis=1)


def _mixer(p_main, p_small, h1, w_gate, b_gate, x2, lb_logits, hg_gain, conv_w, conv_b, ml_gain, w_ua, w_ub, w_out,
           g_post, g_pre, *, seq, layer, tm=SMALL_ROW_TILE):
    m, d = x2.shape
    n_tiles = m // tm
    hg_w = HG_HEADS * HG_HEAD_DIM
    qk_w = ML_HEADS * ML_QK_DIM
    v_w = ML_HEADS * ML_V_DIM
    const = lambda s: (0, 0)
    cur = lambda s: (jnp.minimum(s, n_tiles - 1), 0)
    prev = lambda s: (jnp.maximum(s - 1, 0), 0)
    single = pl.Buffered(1)
    return pl.pallas_call(
        functools.partial(_mixer_kernel, layer=layer, tiles_per_seq=seq // tm, n_tiles=n_tiles),
        grid=(n_tiles + 1,),
        in_specs=[pl.BlockSpec((tm, p_main.shape[1]), cur),
                  pl.BlockSpec((tm, p_small.shape[1]), cur),
                  pl.BlockSpec(lb_logits.shape, const),
                  pl.BlockSpec((1, hg_w), const),
                  pl.BlockSpec(conv_w.shape, const),
                  pl.BlockSpec((1, 2 * qk_w), const),
                  pl.BlockSpec((1, v_w), const),
                  pl.BlockSpec((tm, d), prev),
                  pl.BlockSpec((d, 2 * d), const, pipeline_mode=single),
                  pl.BlockSpec((1, 2 * d), const),
                  pl.BlockSpec((tm, d), prev),
                  pl.BlockSpec((hg_w, d), const, pipeline_mode=single),
                  pl.BlockSpec((v_w, d), const, pipeline_mode=single),
                  pl.BlockSpec((d, d), const, pipeline_mode=single),
                  pl.BlockSpec((1, d), const),
                  pl.BlockSpec((1, d), const)],
        out_specs=[pl.BlockSpec((tm, d), prev),
                   pl.BlockSpec((tm, d), prev)],
        out_shape=[jax.ShapeDtypeStruct((m, d), F32),
                   jax.ShapeDtypeStruct((m, d), BF16)],
        scratch_shapes=[pltpu.VMEM((HG_HEADS, HG_HEAD_DIM, HG_HEAD_DIM), F32),
                        pltpu.VMEM((ML_HEADS, ML_QK_DIM, ML_V_DIM), F32),
                        pltpu.VMEM((ML_HEADS, SUBLANES, ML_QK_DIM), F32),
                        pltpu.VMEM((ML_HEADS, SUBLANES, LANES), F32),
                        pltpu.VMEM((tm + SUBLANES, 2 * qk_w), F32),
                        pltpu.VMEM((tm, hg_w), BF16),
                        pltpu.VMEM((tm, v_w), BF16)],
        compiler_params=_params("arbitrary", vmem=MIXER_VMEM_LIMIT_BYTES),
        name="mixer",
    )(p_main, p_small, lb_logits, hg_gain.reshape(1, hg_w), conv_w, conv_b.reshape(1, 2 * qk_w),
      ml_gain.reshape(1, v_w), h1, w_gate, b_gate.reshape(1, 2 * d), x2, w_ua, w_ub, w_out,
      g_post.reshape(1, d), g_pre.reshape(1, d))


def _ffn_up_kernel(h_ref, wg_ref, wu_ref, cw_ref, cb_ref, a_ref, gbuf_ref, wgb_ref, wub_ref, *, tm, tiles_per_seq):
    kw = cw_ref.shape[0]

    @pl.when(pl.program_id(1) == 0)
    def _():
        wgb_ref[...] = wg_ref[...].astype(BF16)
        wub_ref[...] = wu_ref[...].astype(BF16)

    @pl.when(pl.program_id(1) % tiles_per_seq == 0)
    def _():
        gbuf_ref[0:SUBLANES, :] = jnp.zeros((SUBLANES, gbuf_ref.shape[1]), F32)

    gbuf_ref[SUBLANES:SUBLANES + tm, :] = _dot(h_ref[...], wgb_ref[...])
    acc = cb_ref[...] + cw_ref[kw - 1:kw, :] * gbuf_ref[SUBLANES:SUBLANES + tm, :]
    for j in range(kw - 1):
        s = kw - 1 - j
        acc = acc + cw_ref[j:j + 1, :] * gbuf_ref[SUBLANES - s:SUBLANES - s + tm, :]
    gbuf_ref[0:SUBLANES, :] = gbuf_ref[tm:tm + SUBLANES, :]
    a_ref[...] = (jax.nn.gelu(acc, approximate=True) * _dot(h_ref[...], wub_ref[...])).astype(a_ref.dtype)


def _ffn_up(h2, w_gate, w_up, conv_w, conv_b, *, layer, seq, tm=ROW_TILE, tf=FFN_COL_TILE):
    m, d = h2.shape
    f = w_gate.shape[2]
    kw = conv_w.shape[0]
    wspec = pl.BlockSpec((None, d, tf), lambda j, i: (layer, 0, j))
    return pl.pallas_call(
        functools.partial(_ffn_up_kernel, tm=tm, tiles_per_seq=seq // tm),
        grid=(f // tf, m // tm),
        in_specs=[pl.BlockSpec((tm, d), lambda j, i: (i, 0)),
                  wspec, wspec,
                  pl.BlockSpec((kw, tf), lambda j, i: (0, j)),
                  pl.BlockSpec((1, tf), lambda j, i: (0, j))],
        out_specs=pl.BlockSpec((tm, tf), lambda j, i: (i, j)),
        out_shape=jax.ShapeDtypeStruct((m, f), BF16),
        scratch_shapes=[pltpu.VMEM((tm + SUBLANES, tf), F32),
                        pltpu.VMEM((d, tf), BF16),
                        pltpu.VMEM((d, tf), BF16)],
        compiler_params=_params("arbitrary", "arbitrary"),
        name="ffn_up",
    )(h2, w_gate, w_up, conv_w, conv_b.reshape(1, f))


def _ffn_down_kernel(a_ref, wd_ref, x1_ref, g_ref, o_ref):
    o_ref[...] = x1_ref[...] + _rms(_dot(a_ref[...], wd_ref[...]), g_ref[...])


def _ffn_down(act, w_down, x1, gain, tm=SMALL_ROW_TILE):
    m, f = act.shape
    d = w_down.shape[1]
    return pl.pallas_call(
        _ffn_down_kernel,
        grid=(m // tm,),
        in_specs=[pl.BlockSpec((tm, f), lambda i: (i, 0)),
                  pl.BlockSpec((f, d), lambda i: (0, 0), pipeline_mode=pl.Buffered(1)),
                  pl.BlockSpec((tm, d), lambda i: (i, 0)),
                  pl.BlockSpec((1, d), lambda i: (0, 0))],
        out_specs=pl.BlockSpec((tm, d), lambda i: (i, 0)),
        out_shape=jax.ShapeDtypeStruct((m, d), F32),
        compiler_params=_params("parallel"),
        name="ffn_down",
    )(act, w_down, x1, gain.reshape(1, d))


def kernel(x, norm_mix_pre, norm_mix_post, norm_ffn_pre, norm_ffn_post, w_in, b_in, hg_lb_logits, hg_norm,
           ml_conv_w, ml_conv_b, ml_norm, w_up_a, w_up_b, w_out, ffn_w_gate, ffn_w_up, ffn_conv_w, ffn_conv_b,
           ffn_w_down):
    batch, seq, d = x.shape
    depth = w_in.shape[0]
    hg_w = HG_HEADS * HG_HEAD_DIM
    qk_w = ML_HEADS * ML_QK_DIM
    v_w = ML_HEADS * ML_V_DIM
    main_w = 4 * hg_w + 2 * qk_w + 2 * v_w
    gate_off = main_w + 2 * ML_HEADS
    assert w_in.shape[2] == gate_off + 2 * d and w_up_a.shape[1] == hg_w and w_up_b.shape[1] == v_w
    assert seq % ROW_TILE == 0 and main_w % COL_TILE == 0 and ffn_w_gate.shape[2] % FFN_COL_TILE == 0
    w_in_t = jnp.swapaxes(w_in, 1, 2)

    x2 = x.reshape(batch * seq, d)
    for l in range(depth):
        h1, p_small = _rmsnorm_cast(x2, norm_mix_pre[l], w_in_t, b_in[l, main_w:gate_off], layer=l, row0=main_w)
        p_main, w_down, w_g, w_ua, w_ub, w_o = _mm_bias_t(
            h1, w_in_t, b_in[l, :main_w], layer=l, row0=0, n=main_w, tm=ROW_TILE, tn=COL_TILE, out_dtype=BF16,
            side_casts=((ffn_w_down, 0, ffn_w_down.shape[1], False), (w_in_t, gate_off, 2 * d, True),
                        (w_up_a, 0, hg_w, False), (w_up_b, 0, v_w, False), (w_out, 0, d, False)),
            name="in_proj_main")
        x2, h2 = _mixer(p_main, p_small, h1, w_g, b_in[l, gate_off:], x2, hg_lb_logits, hg_norm[l], ml_conv_w[l],
                        ml_conv_b[l], ml_norm[l], w_ua, w_ub, w_o, norm_mix_post[l], norm_ffn_pre[l],
                        seq=seq, layer=l)

        act = _ffn_up(h2, ffn_w_gate, ffn_w_up, ffn_conv_w[l], ffn_conv_b[l], layer=l, seq=seq)
        x2 = _ffn_down(act, w_down, x2, norm_ffn_post[l])
    return x2.reshape(batch, seq, d)
```

```python
import functools

import jax
import jax.numpy as jnp
from jax import lax
from jax.experimental import pallas as pl
from jax.experimental.pallas import tpu as pltpu

HG_HEADS = 8
HG_HEAD_DIM = 128
ML_HEADS = 4
ML_QK_DIM = 128
ML_V_DIM = 256
CHUNK = 64
EPS = 1e-6

F32 = jnp.float32
BF16 = jnp.bfloat16

SUBLANES = 8
LANES = 128
MXU_WIDTH = 256
MIB = 1024 * 1024
ROW_TILE = 1024
COL_TILE = 1024
FFN_COL_TILE = 512
SMALL_ROW_TILE = 256
VMEM_LIMIT_BYTES = 56 * MIB
MIXER_VMEM_LIMIT_BYTES = 63 * MIB + MIB // 2


def _params(*sem, vmem=VMEM_LIMIT_BYTES):
    return pltpu.CompilerParams(dimension_semantics=sem, vmem_limit_bytes=vmem)


def _dot(a, b):
    return jnp.dot(a, b, preferred_element_type=F32)


def _dot_nt(a, b):
    return lax.dot_general(a, b, (((1,), (1,)), ((), ())), preferred_element_type=F32)


def _dot_tn(a, b):
    return lax.dot_general(a, b, (((0,), (0,)), ((), ())), preferred_element_type=F32)


def _rms(u, gain):
    return u * lax.rsqrt(jnp.mean(u * u, axis=-1, keepdims=True) + EPS) * gain


def _rmsnorm_kernel(x_ref, g_ref, ws_ref, bs_ref, o_ref, ps_ref):
    h = _rms(x_ref[...], g_ref[...]).astype(o_ref.dtype)
    o_ref[...] = h
    ps_ref[...] = _dot_nt(h, ws_ref[...].astype(BF16)) + bs_ref[...]


def _rmsnorm_cast(x2, gain, wt, b_small, *, layer, row0, tm=ROW_TILE):
    m, d = x2.shape
    n = b_small.shape[0]
    row0 += layer * wt.shape[1]
    return pl.pallas_call(
        _rmsnorm_kernel,
        grid=(m // tm,),
        in_specs=[pl.BlockSpec((tm, d), lambda i: (i, 0)),
                  pl.BlockSpec((1, d), lambda i: (0, 0)),
                  pl.BlockSpec((pl.Element(n), pl.Element(d)), lambda i: (row0, 0)),
                  pl.BlockSpec((1, n), lambda i: (0, 0))],
        out_specs=[pl.BlockSpec((tm, d), lambda i: (i, 0)),
                   pl.BlockSpec((tm, n), lambda i: (i, 0))],
        out_shape=[jax.ShapeDtypeStruct((m, d), BF16),
                   jax.ShapeDtypeStruct((m, n), F32)],
        compiler_params=_params("parallel"),
        name="rmsnorm_cast",
    )(x2, gain.reshape(1, d), wt.reshape(-1, d), b_small.reshape(1, n))


def _mm_bias_kernel(h_ref, w_ref, b_ref, *rest, sigmoid, side_transpose):
    n_side = len(side_transpose)
    side_in, o_ref, side_out, wb_ref = rest[:n_side], rest[n_side], rest[n_side + 1:2 * n_side + 1], rest[-1]

    @pl.when(pl.program_id(1) == 0)
    def _():
        wb_ref[...] = w_ref[...].astype(BF16)

    for src, dst, transpose in zip(side_in, side_out, side_transpose):
        blk = src[...]
        dst[...] = (blk.T if transpose else blk).astype(dst.dtype)

    acc = _dot_nt(h_ref[...], wb_ref[...]) + b_ref[...]
    if sigmoid:
        acc = jax.nn.sigmoid(acc)
    o_ref[...] = acc.astype(o_ref.dtype)


def _mm_bias_t(h, wt, b, *, layer, row0, n, tm, tn, out_dtype, sigmoid=False, side_casts=(), name):
    m, k = h.shape
    row0 += layer * wt.shape[1]
    wt = wt.reshape(-1, k)
    n_i = m // tm
    steps = (n // tn) * n_i
    side_specs, side_out_specs, side_shapes, side_args = [], [], [], []
    for a, start, rows, transpose in side_casts:
        cols = a.shape[2]
        start += layer * a.shape[1]
        rb = next(r for r in range(LANES, rows + 1, LANES) if rows % r == 0 and rows // r <= steps)
        nb = rows // rb
        blk = lambda j, i, nb=nb: jnp.minimum(j * n_i + i, nb - 1)
        side_specs.append(pl.BlockSpec(
            (pl.Element(rb), pl.Element(cols)),
            lambda j, i, blk=blk, start=start, rb=rb: (pl.multiple_of(start + blk(j, i) * rb, SUBLANES), 0)))
        if transpose:
            side_out_specs.append(pl.BlockSpec((cols, rb), lambda j, i, blk=blk: (0, blk(j, i))))
            side_shapes.append(jax.ShapeDtypeStruct((cols, rows), BF16))
        else:
            side_out_specs.append(pl.BlockSpec((rb, cols), lambda j, i, blk=blk: (blk(j, i), 0)))
            side_shapes.append(jax.ShapeDtypeStruct((rows, cols), BF16))
        side_args.append(a.reshape(-1, cols))
    outs = pl.pallas_call(
        functools.partial(_mm_bias_kernel, sigmoid=sigmoid, side_transpose=tuple(c[3] for c in side_casts)),
        grid=(n // tn, n_i),
        in_specs=[pl.BlockSpec((tm, k), lambda j, i: (i, 0)),
                  pl.BlockSpec((pl.Element(tn), pl.Element(k)),
                               lambda j, i: (pl.multiple_of(row0 + j * tn, SUBLANES), 0)),
                  pl.BlockSpec((1, tn), lambda j, i: (0, j))] + side_specs,
        out_specs=[pl.BlockSpec((tm, tn), lambda j, i: (i, j))] + side_out_specs,
        out_shape=[jax.ShapeDtypeStruct((m, n), out_dtype)] + side_shapes,
        scratch_shapes=[pltpu.VMEM((tn, k), BF16)],
        compiler_params=_params("arbitrary", "arbitrary"),
        name=name,
    )(h, wt, b.reshape(1, n), *side_args)
    return outs if side_casts else outs[0]


def _tril(n):
    r = lax.broadcasted_iota(jnp.int32, (n, n), 0)
    c = lax.broadcasted_iota(jnp.int32, (n, n), 1)
    return r >= c


def _chunk_cumsum(x):
    pos = lax.broadcasted_iota(jnp.int32, x.shape, 0) & (CHUNK - 1)
    s = 1
    while s < CHUNK:
        x = x + jnp.where(pos >= s, pltpu.roll(x, s, axis=0), 0.0)
        s *= 2
    return x


def _chunk_slices(rows):
    return [slice(c * CHUNK, (c + 1) * CHUNK) for c in range(rows // CHUNK)]


def _hgrn2_head(q_pre, f_pre, v_in, g_pre, lb, gain, s_ref):
    q = jax.nn.silu(q_pre.astype(F32)) * (HG_HEAD_DIM ** -0.5)
    f = lb + (1.0 - lb) * jax.nn.sigmoid(f_pre.astype(F32))
    k = 1.0 - f
    b = _chunk_cumsum(jnp.log(f))
    v = v_in.astype(BF16)
    causal = _tril(CHUNK)
    q_dec = (q * jnp.exp(b)).astype(BF16)
    k_inv = k * jnp.exp(-b)
    k_inv_b = k_inv.astype(BF16)

    chunks = _chunk_slices(q_pre.shape[0])
    decay = [jnp.exp(b[sl.stop - 1:sl.stop, :]) for sl in chunks]
    k_end = [(k_inv[sl] * d).astype(BF16) for sl, d in zip(chunks, decay)]
    yield
    attn = [jnp.where(causal, _dot_nt(q_dec[sl], k_inv_b[sl]), 0.0).astype(BF16) for sl in chunks]
    kv = [_dot_tn(ke, v[sl]) for sl, ke in zip(chunks, k_end)]
    decay_col = [jnp.broadcast_to(d, (SUBLANES, d.shape[1])).T[:, 0:1] for d in decay]
    yield
    s = s_ref[...]
    outs = []
    for c, sl in enumerate(chunks):
        lhs = jnp.concatenate([q_dec[sl], attn[c]], axis=1)
        rhs = jnp.concatenate([s.astype(BF16), v[sl]], axis=0)
        outs.append(_dot(lhs, rhs))
        s = decay_col[c] * s + kv[c]
    s_ref[...] = s
    yield
    o = jnp.concatenate(outs, axis=0)
    return _rms(o, gain) * jax.nn.silu(g_pre.astype(F32))


def _mlstm_head(q, k, v, o_pre, ig_col, b_col, gain, c_ref, n_ref, m_prev):
    rows = q.shape[0]
    qb = q.astype(BF16)
    kb = k.astype(BF16)
    vb = v.astype(BF16)
    wide = (rows, ML_QK_DIM)
    b = jnp.broadcast_to(b_col, wide)
    igb = jnp.broadcast_to(ig_col - b_col, wide)
    causal = _tril(CHUNK)
    chunks = _chunk_slices(rows)
    n_chunks = len(chunks)

    b_last = [b[sl.stop - 1:sl.stop, 0:1] for sl in chunks]
    log_d = [jnp.where(causal, b[sl, :CHUNK] + igb[sl].T[:CHUNK, :], -jnp.inf) for sl in chunks]
    m_intra = [jnp.max(ld, axis=1, keepdims=True) for ld in log_d]
    logw = [bl + igb[sl, 0:1] for bl, sl in zip(b_last, chunks)]
    logw_max = [jnp.max(lw, axis=0, keepdims=True) for lw in logw]
    qk = [_dot_nt(qb[sl], kb[sl]) for sl in chunks]
    yield

    m_in, m_out = [], []
    for c in range(n_chunks):
        m_in.append(m_prev)
        m_prev = jnp.maximum(b_last[c] + m_prev, logw_max[c])
        m_out.append(m_prev)

    m_inter = [b[sl, 0:1] + mi for sl, mi in zip(chunks, m_in)]
    m_tot = [jnp.maximum(a, bb) for a, bb in zip(m_inter, m_intra)]
    scores = [qk[c] * jnp.exp(log_d[c] - m_tot[c]) for c in range(n_chunks)]
    den_intra = [jnp.sum(sc, axis=1, keepdims=True) for sc in scores]
    k_w = [k[sl] * jnp.exp(logw[c] - m_out[c]) for c, sl in enumerate(chunks)]
    k_sum = [jnp.sum(kw, axis=0, keepdims=True) for kw in k_w]
    inter_scale = [jnp.exp(m_inter[c] - m_tot[c]) for c in range(n_chunks)]
    q_scaled = [(q[sl] * inter_scale[c]).astype(BF16) for c, sl in enumerate(chunks)]
    yield
    kv = [_dot_tn(k_w[c].astype(BF16), vb[sl]) for c, sl in enumerate(chunks)]
    yield
    cm = c_ref[...]
    n = n_ref[0:1, :]
    outs = []
    for c, sl in enumerate(chunks):
        lhs = jnp.concatenate([q_scaled[c], scores[c].astype(BF16)], axis=1)
        rhs = jnp.concatenate([cm.astype(BF16), vb[sl]], axis=0)
        num = _dot(lhs, rhs)
        den = den_intra[c] + inter_scale[c] * jnp.sum(q[sl] * n, axis=1, keepdims=True)
        outs.append(num / jnp.maximum(jnp.abs(den), jnp.exp(-m_tot[c])))
        decay = jnp.exp(b_last[c] + m_in[c] - m_out[c])
        cm = decay * cm + kv[c]
        n = decay * n + k_sum[c]
    c_ref[...] = cm
    n_ref[...] = jnp.broadcast_to(n, n_ref.shape)
    yield
    hh = jnp.concatenate(outs, axis=0)
    return jax.nn.sigmoid(o_pre.astype(F32)) * _rms(hh, gain), m_prev


def _interleave(*lanes):
    lanes = [iter(lane) for lane in lanes]
    while lanes:
        for lane in list(lanes):
            try:
                next(lane)
            except StopIteration:
                lanes.remove(lane)


def _idle(n):
    for _ in range(n):
        yield


def _chain(*gens):
    for g in gens:
        yield from g


def _causal_conv(buf_ref, x, w_ref, b_ref):
    kw = w_ref.shape[0]
    rows = x.shape[0]
    buf_ref[SUBLANES:SUBLANES + rows, :] = x
    acc = b_ref[...] + w_ref[kw - 1:kw, :] * x
    for j in range(kw - 1):
        s = kw - 1 - j
        acc = acc + w_ref[j:j + 1, :] * buf_ref[SUBLANES - s:SUBLANES - s + rows, :]
    buf_ref[0:SUBLANES, :] = buf_ref[rows:rows + SUBLANES, :]
    return acc


def _mixer_kernel(pm_ref, ps_ref, lbl_ref, hgn_ref, cw_ref, cb_ref, mln_ref,
                  h1_ref, wg_ref, bg_ref, x_ref, wua_ref, wub_ref, wout_ref, gpost_ref, gpre_ref,
                  x1_ref, h2_ref,
                  st_ref, c_ref, n_ref, m_ref, qkbuf_ref, ya_ref, yb_ref, *, layer, tiles_per_seq, n_tiles):
    s = pl.program_id(0)
    r = jnp.minimum(s, n_tiles - 1)

    @pl.when(s == 0)
    def _():
        ya_ref[...] = jnp.zeros_like(ya_ref)
        yb_ref[...] = jnp.zeros_like(yb_ref)

    @pl.when(r % tiles_per_seq == 0)
    def _():
        st_ref[...] = jnp.zeros_like(st_ref)
        c_ref[...] = jnp.zeros_like(c_ref)
        n_ref[...] = jnp.zeros_like(n_ref)
        m_ref[...] = jnp.zeros_like(m_ref)
        qkbuf_ref[0:SUBLANES, :] = jnp.zeros((SUBLANES, qkbuf_ref.shape[1]), F32)

    d = x_ref.shape[1]
    hg_w = HG_HEADS * HG_HEAD_DIM
    qk_w = ML_HEADS * ML_QK_DIM
    qk_off = 4 * hg_w
    v_off = qk_off + 2 * qk_w
    o_off = v_off + ML_HEADS * ML_V_DIM
    n_pieces = d // MXU_WIDTH
    pw = d // n_pieces
    merged = [None] * n_pieces
    mix = [None] * n_pieces
    ya_new = [None] * HG_HEADS
    yb_new = [None] * ML_HEADS

    def gate(cs):
        return jax.nn.sigmoid(_dot(h1_ref[...], wg_ref[:, cs]) + bg_ref[:, cs])

    def lift_lane():
        for j in range(n_pieces):
            cs = slice(j * pw, (j + 1) * pw)
            part_a = gate(cs) * _dot(ya_ref[...], wua_ref[:, cs])
            yield
            part_b = gate(slice(d + j * pw, d + (j + 1) * pw)) * _dot(yb_ref[...], wub_ref[:, cs])
            merged[j] = (part_a + part_b).astype(BF16)
            yield

    def outproj_lane():
        lhs = jnp.concatenate(merged, axis=1)
        for j in range(n_pieces):
            mix[j] = _dot(lhs, wout_ref[:, j * pw:(j + 1) * pw])
            yield

    def hgrn2_piece(h):
        cols = [slice(j * hg_w + h * HG_HEAD_DIM, j * hg_w + (h + 1) * HG_HEAD_DIM) for j in range(4)]
        hs = slice(h * HG_HEAD_DIM, (h + 1) * HG_HEAD_DIM)
        y = yield from _hgrn2_head(pm_ref[:, cols[0]], pm_ref[:, cols[1]], pm_ref[:, cols[2]],
                                   pm_ref[:, cols[3]], lb[:, hs], hgn_ref[:, hs], st_ref.at[h])
        ya_new[h] = y.astype(ya_ref.dtype)

    def mlstm_piece(h):
        qs = slice(h * ML_QK_DIM, (h + 1) * ML_QK_DIM)
        ks = slice(qk_w + h * ML_QK_DIM, qk_w + (h + 1) * ML_QK_DIM)
        vs = slice(h * ML_V_DIM, (h + 1) * ML_V_DIM)
        y, m_new = yield from _mlstm_head(qk[:, qs], qk[:, ks] * (ML_QK_DIM ** -0.5),
                                          pm_ref[:, v_off + vs.start:v_off + vs.stop],
                                          pm_ref[:, o_off + vs.start:o_off + vs.stop],
                                          gates[:, h:h + 1], gates_b[:, ML_HEADS + h:ML_HEADS + h + 1],
                                          mln_ref[:, vs], c_ref.at[h], n_ref.at[h], m_ref[h, 0:1, 0:1])
        m_ref[h] = jnp.broadcast_to(m_new, m_ref.shape[1:])
        yb_new[h] = y.astype(yb_ref.dtype)

    logits = lbl_ref[...]
    e = jnp.exp(logits - jnp.max(logits, axis=0, keepdims=True))
    sm = e / jnp.sum(e, axis=0, keepdims=True)
    lb = jnp.sum(sm[:layer + 1], axis=0, keepdims=True)
    qk = jax.nn.silu(_causal_conv(qkbuf_ref, pm_ref[:, qk_off:qk_off + 2 * qk_w].astype(F32), cw_ref, cb_ref))
    gates = ps_ref[...]
    gates_b = _chunk_cumsum(jax.nn.log_sigmoid(gates))

    _interleave(lift_lane(),
                _chain(*[hgrn2_piece(h) for h in range(0, HG_HEADS, 2)], mlstm_piece(0)),
                _chain(_idle(2), *[hgrn2_piece(h) for h in range(1, HG_HEADS, 2)], mlstm_piece(1)))
    _interleave(outproj_lane(),
                mlstm_piece(2),
                _chain(_idle(2), mlstm_piece(3)))
    x1 = x_ref[...] + _rms(jnp.concatenate(mix, axis=1), gpost_ref[...])
    x1_ref[...] = x1
    h2_ref[...] = _rms(x1, gpre_ref[...]).astype(h2_ref.dtype)
    ya_ref[...] = jnp.concatenate(ya_new, axis=1)
    yb_ref[...] = jnp.concatenate(yb_new, axis=1)


def _mixer(p_main, p_small, h1, w_gate, b_gate, x2, lb_logits, hg_gain, conv_w, conv_b, ml_gain, w_ua, w_ub, w_out,
           g_post, g_pre, *, seq, layer, tm=SMALL_ROW_TILE):
    m, d = x2.shape
    n_tiles = m // tm
    hg_w = HG_HEADS * HG_HEAD_DIM
    qk_w = ML_HEADS * ML_QK_DIM
    v_w = ML_HEADS * ML_V_DIM
    const = lambda s: (0, 0)
    cur = lambda s: (jnp.minimum(s, n_tiles - 1), 0)
    prev = lambda s: (jnp.maximum(s - 1, 0), 0)
    single = pl.Buffered(1)
    return pl.pallas_call(
        functools.partial(_mixer_kernel, layer=layer, tiles_per_seq=seq // tm, n_tiles=n_tiles),
        grid=(n_tiles + 1,),
        in_specs=[pl.BlockSpec((tm, p_main.shape[1]), cur),
                  pl.BlockSpec((tm, p_small.shape[1]), cur),
                  pl.BlockSpec(lb_logits.shape, const),
                  pl.BlockSpec((1, hg_w), const),
                  pl.BlockSpec(conv_w.shape, const),
                  pl.BlockSpec((1, 2 * qk_w), const),
                  pl.BlockSpec((1, v_w), const),
                  pl.BlockSpec((tm, d), prev),
                  pl.BlockSpec((d, 2 * d), const, pipeline_mode=single),
                  pl.BlockSpec((1, 2 * d), const),
                  pl.BlockSpec((tm, d), prev),
                  pl.BlockSpec((hg_w, d), const, pipeline_mode=single),
                  pl.BlockSpec((v_w, d), const, pipeline_mode=single),
                  pl.BlockSpec((d, d), const, pipeline_mode=single),
                  pl.BlockSpec((1, d), const),
                  pl.BlockSpec((1, d), const)],
        out_specs=[pl.BlockSpec((tm, d), prev),
                   pl.BlockSpec((tm, d), prev)],
        out_shape=[jax.ShapeDtypeStruct((m, d), F32),
                   jax.ShapeDtypeStruct((m, d), BF16)],
        scratch_shapes=[pltpu.VMEM((HG_HEADS, HG_HEAD_DIM, HG_HEAD_DIM), F32),
                        pltpu.VMEM((ML_HEADS, ML_QK_DIM, ML_V_DIM), F32),
                        pltpu.VMEM((ML_HEADS, SUBLANES, ML_QK_DIM), F32),
                        pltpu.VMEM((ML_HEADS, SUBLANES, LANES), F32),
                        pltpu.VMEM((tm + SUBLANES, 2 * qk_w), F32),
                        pltpu.VMEM((tm, hg_w), BF16),
                        pltpu.VMEM((tm, v_w), BF16)],
        compiler_params=_params("arbitrary", vmem=MIXER_VMEM_LIMIT_BYTES),
        name="mixer",
    )(p_main, p_small, lb_logits, hg_gain.reshape(1, hg_w), conv_w, conv_b.reshape(1, 2 * qk_w),
      ml_gain.reshape(1, v_w), h1, w_gate, b_gate.reshape(1, 2 * d), x2, w_ua, w_ub, w_out,
      g_post.reshape(1, d), g_pre.reshape(1, d))


def _ffn_up_kernel(h_ref, wg_ref, wu_ref, cw_ref, cb_ref, a_ref, gbuf_ref, wgb_ref, wub_ref, *, tm, tiles_per_seq):
    kw = cw_ref.shape[0]

    @pl.when(pl.program_id(1) == 0)
    def _():
        wgb_ref[...] = wg_ref[...].astype(BF16)
        wub_ref[...] = wu_ref[...].astype(BF16)

    @pl.when(pl.program_id(1) % tiles_per_seq == 0)
    def _():
        gbuf_ref[0:SUBLANES, :] = jnp.zeros((SUBLANES, gbuf_ref.shape[1]), F32)

    gbuf_ref[SUBLANES:SUBLANES + tm, :] = _dot(h_ref[...], wgb_ref[...])
    acc = cb_ref[...] + cw_ref[kw - 1:kw, :] * gbuf_ref[SUBLANES:SUBLANES + tm, :]
    for j in range(kw - 1):
        s = kw - 1 - j
        acc = acc + cw_ref[j:j + 1, :] * gbuf_ref[SUBLANES - s:SUBLANES - s + tm, :]
    gbuf_ref[0:SUBLANES, :] = gbuf_ref[tm:tm + SUBLANES, :]
    a_ref[...] = (jax.nn.gelu(acc, approximate=True) * _dot(h_ref[...], wub_ref[...])).astype(a_ref.dtype)


def _ffn_up(h2, w_gate, w_up, conv_w, conv_b, *, layer, seq, tm=ROW_TILE, tf=FFN_COL_TILE):
    m, d = h2.shape
    f = w_gate.shape[2]
    kw = conv_w.shape[0]
    wspec = pl.BlockSpec((None, d, tf), lambda j, i: (layer, 0, j))
    return pl.pallas_call(
        functools.partial(_ffn_up_kernel, tm=tm, tiles_per_seq=seq // tm),
        grid=(f // tf, m // tm),
        in_specs=[pl.BlockSpec((tm, d), lambda j, i: (i, 0)),
                  wspec, wspec,
                  pl.BlockSpec((kw, tf), lambda j, i: (0, j)),
                  pl.BlockSpec((1, tf), lambda j, i: (0, j))],
        out_specs=pl.BlockSpec((tm, tf), lambda j, i: (i, j)),
        out_shape=jax.ShapeDtypeStruct((m, f), BF16),
        scratch_shapes=[pltpu.VMEM((tm + SUBLANES, tf), F32),
                        pltpu.VMEM((d, tf), BF16),
                        pltpu.VMEM((d, tf), BF16)],
        compiler_params=_params("arbitrary", "arbitrary"),
        name="ffn_up",
    )(h2, w_gate, w_up, conv_w, conv_b.reshape(1, f))


def _ffn_down_kernel(a_ref, wd_ref, x1_ref, g_ref, o_ref):
    o_ref[...] = x1_ref[...] + _rms(_dot(a_ref[...], wd_ref[...]), g_ref[...])


def _ffn_down(act, w_down, x1, gain, tm=SMALL_ROW_TILE):
    m, f = act.shape
    d = w_down.shape[1]
    return pl.pallas_call(
        _ffn_down_kernel,
        grid=(m // tm,),
        in_specs=[pl.BlockSpec((tm, f), lambda i: (i, 0)),
                  pl.BlockSpec((f, d), lambda i: (0, 0), pipeline_mode=pl.Buffered(1)),
                  pl.BlockSpec((tm, d), lambda i: (i, 0)),
                  pl.BlockSpec((1, d), lambda i: (0, 0))],
        out_specs=pl.BlockSpec((tm, d), lambda i: (i, 0)),
        out_shape=jax.ShapeDtypeStruct((m, d), F32),
        compiler_params=_params("parallel"),
        name="ffn_down",
    )(act, w_down, x1, gain.reshape(1, d))


def kernel(x, norm_mix_pre, norm_mix_post, norm_ffn_pre, norm_ffn_post, w_in, b_in, hg_lb_logits, hg_norm,
           ml_conv_w, ml_conv_b, ml_norm, w_up_a, w_up_b, w_out, ffn_w_gate, ffn_w_up, ffn_conv_w, ffn_conv_b,
           ffn_w_down):
    batch, seq, d = x.shape
    depth = w_in.shape[0]
    hg_w = HG_HEADS * HG_HEAD_DIM
    qk_w = ML_HEADS * ML_QK_DIM
    v_w = ML_HEADS * ML_V_DIM
    main_w = 4 * hg_w + 2 * qk_w + 2 * v_w
    gate_off = main_w + 2 * ML_HEADS
    assert w_in.shape[2] == gate_off + 2 * d and w_up_a.shape[1] == hg_w and w_up_b.shape[1] == v_w
    assert seq % ROW_TILE == 0 and main_w % COL_TILE == 0 and ffn_w_gate.shape[2] % FFN_COL_TILE == 0
    w_in_t = jnp.swapaxes(w_in, 1, 2)

    x2 = x.reshape(batch * seq, d)
    for l in range(depth):
        h1, p_small = _rmsnorm_cast(x2, norm_mix_pre[l], w_in_t, b_in[l, main_w:gate_off], layer=l, row0=main_w)
        p_main, w_down, w_g, w_ua, w_ub, w_o = _mm_bias_t(
            h1, w_in_t, b_in[l, :main_w], layer=l, row0=0, n=main_w, tm=ROW_TILE, tn=COL_TILE, out_dtype=BF16,
            side_casts=((ffn_w_down, 0, ffn_w_down.shape[1], False), (w_in_t, gate_off, 2 * d, True),
                        (w_up_a, 0, hg_w, False), (w_up_b, 0, v_w, False), (w_out, 0, d, False)),
            name="in_proj_main")
        x2, h2 = _mixer(p_main, p_small, h1, w_g, b_in[l, gate_off:], x2, hg_lb_logits, hg_norm[l], ml_conv_w[l],
                        ml_conv_b[l], ml_norm[l], w_ua, w_ub, w_o, norm_mix_post[l], norm_ffn_pre[l],
                        seq=seq, layer=l)

        act = _ffn_up(h2, ffn_w_gate, ffn_w_up, ffn_conv_w[l], ffn_conv_b[l], layer=l, seq=seq)
        x2 = _ffn_down(act, w_down, x2, norm_ffn_post[l])
    return x2.reshape(batch, seq, d)
```

```python
import functools

import jax
import jax.numpy as jnp
from jax import lax
from jax.experimental import pallas as pl
from jax.experimental.pallas import tpu as pltpu

HG_HEADS = 8
HG_HEAD_DIM = 128
ML_HEADS = 4
ML_QK_DIM = 128
ML_V_DIM = 256
CHUNK = 64
ML_CHUNK = 128
EPS = 1e-6

F32 = jnp.float32
BF16 = jnp.bfloat16

SUBLANES = 8
LANES = 128
MXU_WIDTH = 256
MIB = 1024 * 1024
ROW_TILE = 1024
COL_TILE = 1024
FFN_COL_TILE = 512
SMALL_ROW_TILE = 256
VMEM_LIMIT_BYTES = 56 * MIB
MIXER_VMEM_LIMIT_BYTES = 63 * MIB + MIB // 2


def _params(*sem, vmem=VMEM_LIMIT_BYTES):
    return pltpu.CompilerParams(dimension_semantics=sem, vmem_limit_bytes=vmem)


def _dot(a, b):
    return jnp.dot(a, b, preferred_element_type=F32)


def _dot_nt(a, b):
    return lax.dot_general(a, b, (((1,), (1,)), ((), ())), preferred_element_type=F32)


def _dot_tn(a, b):
    return lax.dot_general(a, b, (((0,), (0,)), ((), ())), preferred_element_type=F32)


def _rms(u, gain):
    return u * lax.rsqrt(jnp.mean(u * u, axis=-1, keepdims=True) + EPS) * gain


def _rmsnorm_kernel(x_ref, g_ref, ws_ref, bs_ref, o_ref, ps_ref):
    h = _rms(x_ref[...], g_ref[...]).astype(o_ref.dtype)
    o_ref[...] = h
    ps_ref[...] = _dot_nt(h, ws_ref[...].astype(BF16)) + bs_ref[...]


def _rmsnorm_cast(x2, gain, wt, b_small, *, layer, row0, tm=ROW_TILE):
    m, d = x2.shape
    n = b_small.shape[0]
    row0 += layer * wt.shape[1]
    return pl.pallas_call(
        _rmsnorm_kernel,
        grid=(m // tm,),
        in_specs=[pl.BlockSpec((tm, d), lambda i: (i, 0)),
                  pl.BlockSpec((1, d), lambda i: (0, 0)),
                  pl.BlockSpec((pl.Element(n), pl.Element(d)), lambda i: (row0, 0)),
                  pl.BlockSpec((1, n), lambda i: (0, 0))],
        out_specs=[pl.BlockSpec((tm, d), lambda i: (i, 0)),
                   pl.BlockSpec((tm, n), lambda i: (i, 0))],
        out_shape=[jax.ShapeDtypeStruct((m, d), BF16),
                   jax.ShapeDtypeStruct((m, n), F32)],
        compiler_params=_params("parallel"),
        name="rmsnorm_cast",
    )(x2, gain.reshape(1, d), wt.reshape(-1, d), b_small.reshape(1, n))


def _mm_bias_kernel(h_ref, w_ref, b_ref, *rest, sigmoid, side_transpose):
    n_side = len(side_transpose)
    side_in, o_ref, side_out, wb_ref = rest[:n_side], rest[n_side], rest[n_side + 1:2 * n_side + 1], rest[-1]

    @pl.when(pl.program_id(1) == 0)
    def _():
        wb_ref[...] = w_ref[...].astype(BF16)

    for src, dst, transpose in zip(side_in, side_out, side_transpose):
        blk = src[...]
        dst[...] = (blk.T if transpose else blk).astype(dst.dtype)

    acc = _dot_nt(h_ref[...], wb_ref[...]) + b_ref[...]
    if sigmoid:
        acc = jax.nn.sigmoid(acc)
    o_ref[...] = acc.astype(o_ref.dtype)


def _mm_bias_t(h, wt, b, *, layer, row0, n, tm, tn, out_dtype, sigmoid=False, side_casts=(), name):
    m, k = h.shape
    row0 += layer * wt.shape[1]
    wt = wt.reshape(-1, k)
    n_i = m // tm
    steps = (n // tn) * n_i
    side_specs, side_out_specs, side_shapes, side_args = [], [], [], []
    for a, start, rows, transpose in side_casts:
        cols = a.shape[2]
        start += layer * a.shape[1]
        rb = next(r for r in range(LANES, rows + 1, LANES) if rows % r == 0 and rows // r <= steps)
        nb = rows // rb
        blk = lambda j, i, nb=nb: jnp.minimum(j * n_i + i, nb - 1)
        side_specs.append(pl.BlockSpec(
            (pl.Element(rb), pl.Element(cols)),
            lambda j, i, blk=blk, start=start, rb=rb: (pl.multiple_of(start + blk(j, i) * rb, SUBLANES), 0)))
        if transpose:
            side_out_specs.append(pl.BlockSpec((cols, rb), lambda j, i, blk=blk: (0, blk(j, i))))
            side_shapes.append(jax.ShapeDtypeStruct((cols, rows), BF16))
        else:
            side_out_specs.append(pl.BlockSpec((rb, cols), lambda j, i, blk=blk: (blk(j, i), 0)))
            side_shapes.append(jax.ShapeDtypeStruct((rows, cols), BF16))
        side_args.append(a.reshape(-1, cols))
    outs = pl.pallas_call(
        functools.partial(_mm_bias_kernel, sigmoid=sigmoid, side_transpose=tuple(c[3] for c in side_casts)),
        grid=(n // tn, n_i),
        in_specs=[pl.BlockSpec((tm, k), lambda j, i: (i, 0)),
                  pl.BlockSpec((pl.Element(tn), pl.Element(k)),
                               lambda j, i: (pl.multiple_of(row0 + j * tn, SUBLANES), 0)),
                  pl.BlockSpec((1, tn), lambda j, i: (0, j))] + side_specs,
        out_specs=[pl.BlockSpec((tm, tn), lambda j, i: (i, j))] + side_out_specs,
        out_shape=[jax.ShapeDtypeStruct((m, n), out_dtype)] + side_shapes,
        scratch_shapes=[pltpu.VMEM((tn, k), BF16)],
        compiler_params=_params("arbitrary", "arbitrary"),
        name=name,
    )(h, wt, b.reshape(1, n), *side_args)
    return outs if side_casts else outs[0]


def _tril(n):
    r = lax.broadcasted_iota(jnp.int32, (n, n), 0)
    c = lax.broadcasted_iota(jnp.int32, (n, n), 1)
    return r >= c


def _chunk_cumsum(x, chunk):
    pos = lax.broadcasted_iota(jnp.int32, x.shape, 0) & (chunk - 1)
    s = 1
    while s < chunk:
        x = x + jnp.where(pos >= s, pltpu.roll(x, s, axis=0), 0.0)
        s *= 2
    return x


def _chunk_slices(rows, chunk):
    return [slice(c * chunk, (c + 1) * chunk) for c in range(rows // chunk)]


def _hgrn2_head(q_pre, f_pre, v_in, g_pre, lb, gain, s_ref):
    q = jax.nn.silu(q_pre.astype(F32)) * (HG_HEAD_DIM ** -0.5)
    f = lb + (1.0 - lb) * jax.nn.sigmoid(f_pre.astype(F32))
    k = 1.0 - f
    b = _chunk_cumsum(jnp.log(f), CHUNK)
    v = v_in.astype(BF16)
    causal = _tril(CHUNK)
    q_dec = (q * jnp.exp(b)).astype(BF16)
    k_inv = k * jnp.exp(-b)
    k_inv_b = k_inv.astype(BF16)

    chunks = _chunk_slices(q_pre.shape[0], CHUNK)
    decay = [jnp.exp(b[sl.stop - 1:sl.stop, :]) for sl in chunks]
    k_end = [(k_inv[sl] * d).astype(BF16) for sl, d in zip(chunks, decay)]
    yield
    attn = [jnp.where(causal, _dot_nt(q_dec[sl], k_inv_b[sl]), 0.0).astype(BF16) for sl in chunks]
    kv = [_dot_tn(ke, v[sl]) for sl, ke in zip(chunks, k_end)]
    decay_col = [jnp.broadcast_to(d, (SUBLANES, d.shape[1])).T[:, 0:1] for d in decay]
    yield
    s = s_ref[...]
    outs = []
    for c, sl in enumerate(chunks):
        lhs = jnp.concatenate([q_dec[sl], attn[c]], axis=1)
        rhs = jnp.concatenate([s.astype(BF16), v[sl]], axis=0)
        outs.append(_dot(lhs, rhs))
        s = decay_col[c] * s + kv[c]
    s_ref[...] = s
    yield
    o = jnp.concatenate(outs, axis=0)
    return _rms(o, gain) * jax.nn.silu(g_pre.astype(F32))


def _mlstm_head(q, k, v, o_pre, ig_col, b_col, gain, c_ref, n_ref, m_prev):
    rows = q.shape[0]
    qb = q.astype(BF16)
    kb = k.astype(BF16)
    vb = v.astype(BF16)
    wide = (rows, ML_QK_DIM)
    b = jnp.broadcast_to(b_col, wide)
    igb = jnp.broadcast_to(ig_col - b_col, wide)
    causal = _tril(ML_CHUNK)
    chunks = _chunk_slices(rows, ML_CHUNK)
    n_chunks = len(chunks)

    b_last = [b[sl.stop - 1:sl.stop, 0:1] for sl in chunks]
    log_d = [jnp.where(causal, b[sl, :ML_CHUNK] + igb[sl].T[:ML_CHUNK, :], -jnp.inf) for sl in chunks]
    m_intra = [jnp.max(ld, axis=1, keepdims=True) for ld in log_d]
    logw = [bl + igb[sl, 0:1] for bl, sl in zip(b_last, chunks)]
    logw_max = [jnp.max(lw, axis=0, keepdims=True) for lw in logw]
    qk = [_dot_nt(qb[sl], kb[sl]) for sl in chunks]
    yield

    m_in, m_out = [], []
    for c in range(n_chunks):
        m_in.append(m_prev)
        m_prev = jnp.maximum(b_last[c] + m_prev, logw_max[c])
        m_out.append(m_prev)

    m_inter = [b[sl, 0:1] + mi for sl, mi in zip(chunks, m_in)]
    m_tot = [jnp.maximum(a, bb) for a, bb in zip(m_inter, m_intra)]
    scores = [qk[c] * jnp.exp(log_d[c] - m_tot[c]) for c in range(n_chunks)]
    den_intra = [jnp.sum(sc, axis=1, keepdims=True) for sc in scores]
    k_w = [k[sl] * jnp.exp(logw[c] - m_out[c]) for c, sl in enumerate(chunks)]
    k_sum = [jnp.sum(kw, axis=0, keepdims=True) for kw in k_w]
    inter_scale = [jnp.exp(m_inter[c] - m_tot[c]) for c in range(n_chunks)]
    q_scaled = [(q[sl] * inter_scale[c]).astype(BF16) for c, sl in enumerate(chunks)]
    yield
    kv = [_dot_tn(k_w[c].astype(BF16), vb[sl]) for c, sl in enumerate(chunks)]
    yield
    cm = c_ref[...]
    n = n_ref[0:1, :]
    outs = []
    for c, sl in enumerate(chunks):
        lhs = jnp.concatenate([q_scaled[c], scores[c].astype(BF16)], axis=1)
        rhs = jnp.concatenate([cm.astype(BF16), vb[sl]], axis=0)
        num = _dot(lhs, rhs)
        den = den_intra[c] + inter_scale[c] * jnp.sum(q[sl] * n, axis=1, keepdims=True)
        outs.append(num / jnp.maximum(jnp.abs(den), jnp.exp(-m_tot[c])))
        decay = jnp.exp(b_last[c] + m_in[c] - m_out[c])
        cm = decay * cm + kv[c]
        n = decay * n + k_sum[c]
    c_ref[...] = cm
    n_ref[...] = jnp.broadcast_to(n, n_ref.shape)
    yield
    hh = jnp.concatenate(outs, axis=0)
    return jax.nn.sigmoid(o_pre.astype(F32)) * _rms(hh, gain), m_prev


def _interleave(*lanes):
    lanes = [iter(lane) for lane in lanes]
    while lanes:
        for lane in list(lanes):
            try:
                next(lane)
            except StopIteration:
                lanes.remove(lane)


def _idle(n):
    for _ in range(n):
        yield


def _chain(*gens):
    for g in gens:
        yield from g


def _causal_conv(buf_ref, x, w_ref, b_ref):
    kw = w_ref.shape[0]
    rows = x.shape[0]
    buf_ref[SUBLANES:SUBLANES + rows, :] = x
    acc = b_ref[...] + w_ref[kw - 1:kw, :] * x
    for j in range(kw - 1):
        s = kw - 1 - j
        acc = acc + w_ref[j:j + 1, :] * buf_ref[SUBLANES - s:SUBLANES - s + rows, :]
    buf_ref[0:SUBLANES, :] = buf_ref[rows:rows + SUBLANES, :]
    return acc


def _mixer_kernel(pm_ref, ps_ref, lbl_ref, hgn_ref, cw_ref, cb_ref, mln_ref,
                  h1_ref, wg_ref, bg_ref, x_ref, wua_ref, wub_ref, wout_ref, gpost_ref, gpre_ref,
                  x1_ref, h2_ref,
                  st_ref, c_ref, n_ref, m_ref, qkbuf_ref, ya_ref, yb_ref, *, layer, tiles_per_seq, n_tiles):
    s = pl.program_id(0)
    r = jnp.minimum(s, n_tiles - 1)

    @pl.when(s == 0)
    def _():
        ya_ref[...] = jnp.zeros_like(ya_ref)
        yb_ref[...] = jnp.zeros_like(yb_ref)

    @pl.when(r % tiles_per_seq == 0)
    def _():
        st_ref[...] = jnp.zeros_like(st_ref)
        c_ref[...] = jnp.zeros_like(c_ref)
        n_ref[...] = jnp.zeros_like(n_ref)
        m_ref[...] = jnp.zeros_like(m_ref)
        qkbuf_ref[0:SUBLANES, :] = jnp.zeros((SUBLANES, qkbuf_ref.shape[1]), F32)

    d = x_ref.shape[1]
    hg_w = HG_HEADS * HG_HEAD_DIM
    qk_w = ML_HEADS * ML_QK_DIM
    qk_off = 4 * hg_w
    v_off = qk_off + 2 * qk_w
    o_off = v_off + ML_HEADS * ML_V_DIM
    n_pieces = d // MXU_WIDTH
    pw = d // n_pieces
    merged = [None] * n_pieces
    mix = [None] * n_pieces
    ya_new = [None] * HG_HEADS
    yb_new = [None] * ML_HEADS

    def gate(cs):
        return jax.nn.sigmoid(_dot(h1_ref[...], wg_ref[:, cs]) + bg_ref[:, cs])

    def lift_lane():
        for j in range(n_pieces):
            cs = slice(j * pw, (j + 1) * pw)
            part_a = gate(cs) * _dot(ya_ref[...], wua_ref[:, cs])
            yield
            part_b = gate(slice(d + j * pw, d + (j + 1) * pw)) * _dot(yb_ref[...], wub_ref[:, cs])
            merged[j] = (part_a + part_b).astype(BF16)
            yield

    def outproj_lane():
        lhs = jnp.concatenate(merged, axis=1)
        for j in range(n_pieces):
            mix[j] = _dot(lhs, wout_ref[:, j * pw:(j + 1) * pw])
            yield

    def hgrn2_piece(h):
        cols = [slice(j * hg_w + h * HG_HEAD_DIM, j * hg_w + (h + 1) * HG_HEAD_DIM) for j in range(4)]
        hs = slice(h * HG_HEAD_DIM, (h + 1) * HG_HEAD_DIM)
        y = yield from _hgrn2_head(pm_ref[:, cols[0]], pm_ref[:, cols[1]], pm_ref[:, cols[2]],
                                   pm_ref[:, cols[3]], lb[:, hs], hgn_ref[:, hs], st_ref.at[h])
        ya_new[h] = y.astype(ya_ref.dtype)

    def mlstm_piece(h):
        qs = slice(h * ML_QK_DIM, (h + 1) * ML_QK_DIM)
        ks = slice(qk_w + h * ML_QK_DIM, qk_w + (h + 1) * ML_QK_DIM)
        vs = slice(h * ML_V_DIM, (h + 1) * ML_V_DIM)
        y, m_new = yield from _mlstm_head(qk[:, qs], qk[:, ks] * (ML_QK_DIM ** -0.5),
                                          pm_ref[:, v_off + vs.start:v_off + vs.stop],
                                          pm_ref[:, o_off + vs.start:o_off + vs.stop],
                                          gates[:, h:h + 1], gates_b[:, ML_HEADS + h:ML_HEADS + h + 1],
                                          mln_ref[:, vs], c_ref.at[h], n_ref.at[h], m_ref[h, 0:1, 0:1])
        m_ref[h] = jnp.broadcast_to(m_new, m_ref.shape[1:])
        yb_new[h] = y.astype(yb_ref.dtype)

    logits = lbl_ref[...]
    e = jnp.exp(logits - jnp.max(logits, axis=0, keepdims=True))
    sm = e / jnp.sum(e, axis=0, keepdims=True)
    lb = jnp.sum(sm[:layer + 1], axis=0, keepdims=True)
    qk = jax.nn.silu(_causal_conv(qkbuf_ref, pm_ref[:, qk_off:qk_off + 2 * qk_w].astype(F32), cw_ref, cb_ref))
    gates = ps_ref[...]
    gates_b = _chunk_cumsum(jax.nn.log_sigmoid(gates), ML_CHUNK)

    _interleave(lift_lane(),
                _chain(*[hgrn2_piece(h) for h in range(0, HG_HEADS, 2)], mlstm_piece(0)),
                _chain(_idle(2), *[hgrn2_piece(h) for h in range(1, HG_HEADS, 2)], mlstm_piece(1)))
    _interleave(outproj_lane(),
                mlstm_piece(2),
                _chain(_idle(2), mlstm_piece(3)))
    x1 = x_ref[...] + _rms(jnp.concatenate(mix, axis=1), gpost_ref[...])
    x1_ref[...] = x1
    h2_ref[...] = _rms(x1, gpre_ref[...]).astype(h2_ref.dtype)
    ya_ref[...] = jnp.concatenate(ya_new, axis=1)
    yb_ref[...] = jnp.concatenate(yb_new, axis=1)


def _mixer(p_main, p_small, h1, w_gate, b_gate, x2, lb_logits, hg_gain, conv_w, conv_b, ml_gain, w_ua, w_ub, w_out,
           g_post, g_pre, *, seq, layer, tm=SMALL_ROW_TILE):
    m, d = x2.shape
    n_tiles = m // tm
    hg_w = HG_HEADS * HG_HEAD_DIM
    qk_w = ML_HEADS * ML_QK_DIM
    v_w = ML_HEADS * ML_V_DIM
    const = lambda s: (0, 0)
    cur = lambda s: (jnp.minimum(s, n_tiles - 1), 0)
    prev = lambda s: (jnp.maximum(s - 1, 0), 0)
    single = pl.Buffered(1)
    return pl.pallas_call(
        functools.partial(_mixer_kernel, layer=layer, tiles_per_seq=seq // tm, n_tiles=n_tiles),
        grid=(n_tiles + 1,),
        in_specs=[pl.BlockSpec((tm, p_main.shape[1]), cur),
                  pl.BlockSpec((tm, p_small.shape[1]), cur),
                  pl.BlockSpec(lb_logits.shape, const),
                  pl.BlockSpec((1, hg_w), const),
                  pl.BlockSpec(conv_w.shape, const),
                  pl.BlockSpec((1, 2 * qk_w), const),
                  pl.BlockSpec((1, v_w), const),
                  pl.BlockSpec((tm, d), prev),
                  pl.BlockSpec((d, 2 * d), const, pipeline_mode=single),
                  pl.BlockSpec((1, 2 * d), const),
                  pl.BlockSpec((tm, d), prev),
                  pl.BlockSpec((hg_w, d), const, pipeline_mode=single),
                  pl.BlockSpec((v_w, d), const, pipeline_mode=single),
                  pl.BlockSpec((d, d), const, pipeline_mode=single),
                  pl.BlockSpec((1, d), const),
                  pl.BlockSpec((1, d), const)],
        out_specs=[pl.BlockSpec((tm, d), prev),
                   pl.BlockSpec((tm, d), prev)],
        out_shape=[jax.ShapeDtypeStruct((m, d), F32),
                   jax.ShapeDtypeStruct((m, d), BF16)],
        scratch_shapes=[pltpu.VMEM((HG_HEADS, HG_HEAD_DIM, HG_HEAD_DIM), F32),
                        pltpu.VMEM((ML_HEADS, ML_QK_DIM, ML_V_DIM), F32),
                        pltpu.VMEM((ML_HEADS, SUBLANES, ML_QK_DIM), F32),
                        pltpu.VMEM((ML_HEADS, SUBLANES, LANES), F32),
                        pltpu.VMEM((tm + SUBLANES, 2 * qk_w), F32),
                        pltpu.VMEM((tm, hg_w), BF16),
                        pltpu.VMEM((tm, v_w), BF16)],
        compiler_params=_params("arbitrary", vmem=MIXER_VMEM_LIMIT_BYTES),
        name="mixer",
    )(p_main, p_small, lb_logits, hg_gain.reshape(1, hg_w), conv_w, conv_b.reshape(1, 2 * qk_w),
      ml_gain.reshape(1, v_w), h1, w_gate, b_gate.reshape(1, 2 * d), x2, w_ua, w_ub, w_out,
      g_post.reshape(1, d), g_pre.reshape(1, d))


def _ffn_up_kernel(h_ref, wg_ref, wu_ref, cw_ref, cb_ref, a_ref, gbuf_ref, wgb_ref, wub_ref, *, tm, tiles_per_seq):
    kw = cw_ref.shape[0]

    @pl.when(pl.program_id(1) == 0)
    def _():
        wgb_ref[...] = wg_ref[...].astype(BF16)
        wub_ref[...] = wu_ref[...].astype(BF16)

    @pl.when(pl.program_id(1) % tiles_per_seq == 0)
    def _():
        gbuf_ref[0:SUBLANES, :] = jnp.zeros((SUBLANES, gbuf_ref.shape[1]), F32)

    gbuf_ref[SUBLANES:SUBLANES + tm, :] = _dot(h_ref[...], wgb_ref[...])
    acc = cb_ref[...] + cw_ref[kw - 1:kw, :] * gbuf_ref[SUBLANES:SUBLANES + tm, :]
    for j in range(kw - 1):
        s = kw - 1 - j
        acc = acc + cw_ref[j:j + 1, :] * gbuf_ref[SUBLANES - s:SUBLANES - s + tm, :]
    gbuf_ref[0:SUBLANES, :] = gbuf_ref[tm:tm + SUBLANES, :]
    a_ref[...] = (jax.nn.gelu(acc, approximate=True) * _dot(h_ref[...], wub_ref[...])).astype(a_ref.dtype)


def _ffn_up(h2, w_gate, w_up, conv_w, conv_b, *, layer, seq, tm=ROW_TILE, tf=FFN_COL_TILE):
    m, d = h2.shape
    f = w_gate.shape[2]
    kw = conv_w.shape[0]
    wspec = pl.BlockSpec((None, d, tf), lambda j, i: (layer, 0, j))
    return pl.pallas_call(
        functools.partial(_ffn_up_kernel, tm=tm, tiles_per_seq=seq // tm),
        grid=(f // tf, m // tm),
        in_specs=[pl.BlockSpec((tm, d), lambda j, i: (i, 0)),
                  wspec, wspec,
                  pl.BlockSpec((kw, tf), lambda j, i: (0, j)),
                  pl.BlockSpec((1, tf), lambda j, i: (0, j))],
        out_specs=pl.BlockSpec((tm, tf), lambda j, i: (i, j)),
        out_shape=jax.ShapeDtypeStruct((m, f), BF16),
        scratch_shapes=[pltpu.VMEM((tm + SUBLANES, tf), F32),
                        pltpu.VMEM((d, tf), BF16),
                        pltpu.VMEM((d, tf), BF16)],
        compiler_params=_params("arbitrary", "arbitrary"),
        name="ffn_up",
    )(h2, w_gate, w_up, conv_w, conv_b.reshape(1, f))


def _ffn_down_kernel(a_ref, wd_ref, x1_ref, g_ref, o_ref):
    o_ref[...] = x1_ref[...] + _rms(_dot(a_ref[...], wd_ref[...]), g_ref[...])


def _ffn_down(act, w_down, x1, gain, tm=SMALL_ROW_TILE):
    m, f = act.shape
    d = w_down.shape[1]
    return pl.pallas_call(
        _ffn_down_kernel,
        grid=(m // tm,),
        in_specs=[pl.BlockSpec((tm, f), lambda i: (i, 0)),
                  pl.BlockSpec((f, d), lambda i: (0, 0), pipeline_mode=pl.Buffered(1)),
                  pl.BlockSpec((tm, d), lambda i: (i, 0)),
                  pl.BlockSpec((1, d), lambda i: (0, 0))],
        out_specs=pl.BlockSpec((tm, d), lambda i: (i, 0)),
        out_shape=jax.ShapeDtypeStruct((m, d), F32),
        compiler_params=_params("parallel"),
        name="ffn_down",
    )(act, w_down, x1, gain.reshape(1, d))


def kernel(x, norm_mix_pre, norm_mix_post, norm_ffn_pre, norm_ffn_post, w_in, b_in, hg_lb_logits, hg_norm,
           ml_conv_w, ml_conv_b, ml_norm, w_up_a, w_up_b, w_out, ffn_w_gate, ffn_w_up, ffn_conv_w, ffn_conv_b,
           ffn_w_down):
    batch, seq, d = x.shape
    depth = w_in.shape[0]
    hg_w = HG_HEADS * HG_HEAD_DIM
    qk_w = ML_HEADS * ML_QK_DIM
    v_w = ML_HEADS * ML_V_DIM
    main_w = 4 * hg_w + 2 * qk_w + 2 * v_w
    gate_off = main_w + 2 * ML_HEADS
    assert w_in.shape[2] == gate_off + 2 * d and w_up_a.shape[1] == hg_w and w_up_b.shape[1] == v_w
    assert seq % ROW_TILE == 0 and main_w % COL_TILE == 0 and ffn_w_gate.shape[2] % FFN_COL_TILE == 0
    w_in_t = jnp.swapaxes(w_in, 1, 2)

    x2 = x.reshape(batch * seq, d)
    for l in range(depth):
        h1, p_small = _rmsnorm_cast(x2, norm_mix_pre[l], w_in_t, b_in[l, main_w:gate_off], layer=l, row0=main_w)
        p_main, w_down, w_g, w_ua, w_ub, w_o = _mm_bias_t(
            h1, w_in_t, b_in[l, :main_w], layer=l, row0=0, n=main_w, tm=ROW_TILE, tn=COL_TILE, out_dtype=BF16,
            side_casts=((ffn_w_down, 0, ffn_w_down.shape[1], False), (w_in_t, gate_off, 2 * d, True),
                        (w_up_a, 0, hg_w, False), (w_up_b, 0, v_w, False), (w_out, 0, d, False)),
            name="in_proj_main")
        x2, h2 = _mixer(p_main, p_small, h1, w_g, b_in[l, gate_off:], x2, hg_lb_logits, hg_norm[l], ml_conv_w[l],
                        ml_conv_b[l], ml_norm[l], w_ua, w_ub, w_o, norm_mix_post[l], norm_ffn_pre[l],
                        seq=seq, layer=l)

        act = _ffn_up(h2, ffn_w_gate, ffn_w_up, ffn_conv_w[l], ffn_conv_b[l], layer=l, seq=seq)
        x2 = _ffn_down(act, w_down, x2, norm_ffn_post[l])
    return x2.reshape(batch, seq, d)
```

```python
import functools

import jax
import jax.numpy as jnp
from jax import lax
from jax.experimental import pallas as pl
from jax.experimental.pallas import tpu as pltpu

HG_HEADS = 8
HG_HEAD_DIM = 128
ML_HEADS = 4
ML_QK_DIM = 128
ML_V_DIM = 256
CHUNK = 64
ML_CHUNK = 128
EPS = 1e-6

F32 = jnp.float32
BF16 = jnp.bfloat16

SUBLANES = 8
LANES = 128
MXU_WIDTH = 256
MIB = 1024 * 1024
ROW_TILE = 1024
COL_TILE = 1024
FFN_COL_TILE = 512
SMALL_ROW_TILE = 256
VMEM_LIMIT_BYTES = 56 * MIB
MIXER_VMEM_LIMIT_BYTES = 63 * MIB + MIB // 2


def _params(*sem, vmem=VMEM_LIMIT_BYTES):
    return pltpu.CompilerParams(dimension_semantics=sem, vmem_limit_bytes=vmem)


def _dot(a, b):
    return jnp.dot(a, b, preferred_element_type=F32)


def _dot_nt(a, b):
    return lax.dot_general(a, b, (((1,), (1,)), ((), ())), preferred_element_type=F32)


def _dot_tn(a, b):
    return lax.dot_general(a, b, (((0,), (0,)), ((), ())), preferred_element_type=F32)


def _rms(u, gain):
    return u * lax.rsqrt(jnp.mean(u * u, axis=-1, keepdims=True) + EPS) * gain


def _rmsnorm_kernel(x_ref, g_ref, ws_ref, bs_ref, o_ref, ps_ref):
    h = _rms(x_ref[...], g_ref[...]).astype(o_ref.dtype)
    o_ref[...] = h
    ps_ref[...] = _dot_nt(h, ws_ref[...].astype(BF16)) + bs_ref[...]


def _rmsnorm_cast(x2, gain, wt, b_small, *, layer, row0, tm=ROW_TILE):
    m, d = x2.shape
    n = b_small.shape[0]
    row0 += layer * wt.shape[1]
    return pl.pallas_call(
        _rmsnorm_kernel,
        grid=(m // tm,),
        in_specs=[pl.BlockSpec((tm, d), lambda i: (i, 0)),
                  pl.BlockSpec((1, d), lambda i: (0, 0)),
                  pl.BlockSpec((pl.Element(n), pl.Element(d)), lambda i: (row0, 0)),
                  pl.BlockSpec((1, n), lambda i: (0, 0))],
        out_specs=[pl.BlockSpec((tm, d), lambda i: (i, 0)),
                   pl.BlockSpec((tm, n), lambda i: (i, 0))],
        out_shape=[jax.ShapeDtypeStruct((m, d), BF16),
                   jax.ShapeDtypeStruct((m, n), F32)],
        compiler_params=_params("parallel"),
        name="rmsnorm_cast",
    )(x2, gain.reshape(1, d), wt.reshape(-1, d), b_small.reshape(1, n))


def _mm_bias_kernel(h_ref, w_ref, b_ref, *rest, sigmoid, side_transpose):
    n_side = len(side_transpose)
    side_in, o_ref, side_out, wb_ref = rest[:n_side], rest[n_side], rest[n_side + 1:2 * n_side + 1], rest[-1]

    @pl.when(pl.program_id(1) == 0)
    def _():
        wb_ref[...] = w_ref[...].astype(BF16)

    for src, dst, transpose in zip(side_in, side_out, side_transpose):
        blk = src[...]
        dst[...] = (blk.T if transpose else blk).astype(dst.dtype)

    acc = _dot_nt(h_ref[...], wb_ref[...]) + b_ref[...]
    if sigmoid:
        acc = jax.nn.sigmoid(acc)
    o_ref[...] = acc.astype(o_ref.dtype)


def _mm_bias_t(h, wt, b, *, layer, row0, n, tm, tn, out_dtype, sigmoid=False, side_casts=(), name):
    m, k = h.shape
    row0 += layer * wt.shape[1]
    wt = wt.reshape(-1, k)
    n_i = m // tm
    steps = (n // tn) * n_i
    side_specs, side_out_specs, side_shapes, side_args = [], [], [], []
    for a, start, rows, transpose in side_casts:
        cols = a.shape[2]
        start += layer * a.shape[1]
        rb = next(r for r in range(LANES, rows + 1, LANES) if rows % r == 0 and rows // r <= steps)
        nb = rows // rb
        blk = lambda j, i, nb=nb: jnp.minimum(j * n_i + i, nb - 1)
        side_specs.append(pl.BlockSpec(
            (pl.Element(rb), pl.Element(cols)),
            lambda j, i, blk=blk, start=start, rb=rb: (pl.multiple_of(start + blk(j, i) * rb, SUBLANES), 0)))
        if transpose:
            side_out_specs.append(pl.BlockSpec((cols, rb), lambda j, i, blk=blk: (0, blk(j, i))))
            side_shapes.append(jax.ShapeDtypeStruct((cols, rows), BF16))
        else:
            side_out_specs.append(pl.BlockSpec((rb, cols), lambda j, i, blk=blk: (blk(j, i), 0)))
            side_shapes.append(jax.ShapeDtypeStruct((rows, cols), BF16))
        side_args.append(a.reshape(-1, cols))
    outs = pl.pallas_call(
        functools.partial(_mm_bias_kernel, sigmoid=sigmoid, side_transpose=tuple(c[3] for c in side_casts)),
        grid=(n // tn, n_i),
        in_specs=[pl.BlockSpec((tm, k), lambda j, i: (i, 0)),
                  pl.BlockSpec((pl.Element(tn), pl.Element(k)),
                               lambda j, i: (pl.multiple_of(row0 + j * tn, SUBLANES), 0)),
                  pl.BlockSpec((1, tn), lambda j, i: (0, j))] + side_specs,
        out_specs=[pl.BlockSpec((tm, tn), lambda j, i: (i, j))] + side_out_specs,
        out_shape=[jax.ShapeDtypeStruct((m, n), out_dtype)] + side_shapes,
        scratch_shapes=[pltpu.VMEM((tn, k), BF16)],
        compiler_params=_params("arbitrary", "arbitrary"),
        name=name,
    )(h, wt, b.reshape(1, n), *side_args)
    return outs if side_casts else outs[0]


def _tril(n):
    r = lax.broadcasted_iota(jnp.int32, (n, n), 0)
    c = lax.broadcasted_iota(jnp.int32, (n, n), 1)
    return r >= c


def _chunk_cumsum(x, chunk):
    pos = lax.broadcasted_iota(jnp.int32, x.shape, 0) & (chunk - 1)
    s = 1
    while s < chunk:
        x = x + jnp.where(pos >= s, pltpu.roll(x, s, axis=0), 0.0)
        s *= 2
    return x


def _chunk_slices(rows, chunk):
    return [slice(c * chunk, (c + 1) * chunk) for c in range(rows // chunk)]


def _hgrn2_head(q_pre, f_pre, v_in, g_pre, lb, gain, s_ref):
    q = jax.nn.silu(q_pre.astype(F32)) * (HG_HEAD_DIM ** -0.5)
    f = lb + (1.0 - lb) * jax.nn.sigmoid(f_pre.astype(F32))
    k = 1.0 - f
    b = _chunk_cumsum(jnp.log(f), CHUNK)
    v = v_in.astype(BF16)
    causal = _tril(CHUNK)
    q_dec = (q * jnp.exp(b)).astype(BF16)
    k_inv = k * jnp.exp(-b)
    k_inv_b = k_inv.astype(BF16)

    chunks = _chunk_slices(q_pre.shape[0], CHUNK)
    decay = [jnp.exp(b[sl.stop - 1:sl.stop, :]) for sl in chunks]
    k_end = [(k_inv[sl] * d).astype(BF16) for sl, d in zip(chunks, decay)]
    yield
    attn = [jnp.where(causal, _dot_nt(q_dec[sl], k_inv_b[sl]), 0.0).astype(BF16) for sl in chunks]
    kv = [_dot_tn(ke, v[sl]) for sl, ke in zip(chunks, k_end)]
    decay_col = [jnp.broadcast_to(d, (SUBLANES, d.shape[1])).T[:, 0:1] for d in decay]
    yield
    s = s_ref[...]
    outs = []
    for c, sl in enumerate(chunks):
        lhs = jnp.concatenate([q_dec[sl], attn[c]], axis=1)
        rhs = jnp.concatenate([s.astype(BF16), v[sl]], axis=0)
        outs.append(_dot(lhs, rhs))
        s = decay_col[c] * s + kv[c]
    s_ref[...] = s
    yield
    o = jnp.concatenate(outs, axis=0)
    return _rms(o, gain) * jax.nn.silu(g_pre.astype(F32))


def _mlstm_head(q, k, v, o_pre, ig_col, b_col, gain, c_ref, n_ref, m_prev):
    rows = q.shape[0]
    qb = q.astype(BF16)
    kb = k.astype(BF16)
    vb = v.astype(BF16)
    wide = (rows, ML_QK_DIM)
    b = jnp.broadcast_to(b_col, wide)
    igb = jnp.broadcast_to(ig_col - b_col, wide)
    causal = _tril(ML_CHUNK)
    chunks = _chunk_slices(rows, ML_CHUNK)
    n_chunks = len(chunks)

    b_last = [b[sl.stop - 1:sl.stop, 0:1] for sl in chunks]
    log_d = [jnp.where(causal, b[sl, :ML_CHUNK] + igb[sl].T[:ML_CHUNK, :], -jnp.inf) for sl in chunks]
    m_intra = [jnp.max(ld, axis=1, keepdims=True) for ld in log_d]
    logw = [bl + igb[sl, 0:1] for bl, sl in zip(b_last, chunks)]
    logw_max = [jnp.max(lw, axis=0, keepdims=True) for lw in logw]
    qk = [_dot_nt(qb[sl], kb[sl]) for sl in chunks]
    yield

    m_in, m_out = [], []
    for c in range(n_chunks):
        m_in.append(m_prev)
        m_prev = jnp.maximum(b_last[c] + m_prev, logw_max[c])
        m_out.append(m_prev)

    m_inter = [b[sl, 0:1] + mi for sl, mi in zip(chunks, m_in)]
    m_tot = [jnp.maximum(a, bb) for a, bb in zip(m_inter, m_intra)]
    scores = [qk[c] * jnp.exp(log_d[c] - m_tot[c]) for c in range(n_chunks)]
    den_intra = [jnp.sum(sc, axis=1, keepdims=True) for sc in scores]
    k_w = [k[sl] * jnp.exp(logw[c] - m_out[c]) for c, sl in enumerate(chunks)]
    k_sum = [jnp.sum(kw, axis=0, keepdims=True) for kw in k_w]
    inter_scale = [jnp.exp(m_inter[c] - m_tot[c]) for c in range(n_chunks)]
    q_scaled = [(q[sl] * inter_scale[c]).astype(BF16) for c, sl in enumerate(chunks)]
    yield
    kv = [_dot_tn(k_w[c].astype(BF16), vb[sl]) for c, sl in enumerate(chunks)]
    yield
    cm = c_ref[...]
    n = n_ref[0:1, :]
    outs = []
    for c, sl in enumerate(chunks):
        lhs = jnp.concatenate([q_scaled[c], scores[c].astype(BF16)], axis=1)
        rhs = jnp.concatenate([cm.astype(BF16), vb[sl]], axis=0)
        num = _dot(lhs, rhs)
        den = den_intra[c] + inter_scale[c] * jnp.sum(q[sl] * n, axis=1, keepdims=True)
        outs.append(num / jnp.maximum(jnp.abs(den), jnp.exp(-m_tot[c])))
        decay = jnp.exp(b_last[c] + m_in[c] - m_out[c])
        cm = decay * cm + kv[c]
        n = decay * n + k_sum[c]
    c_ref[...] = cm
    n_ref[...] = jnp.broadcast_to(n, n_ref.shape)
    yield
    hh = jnp.concatenate(outs, axis=0)
    return jax.nn.sigmoid(o_pre.astype(F32)) * _rms(hh, gain), m_prev


def _interleave(*lanes):
    lanes = [iter(lane) for lane in lanes]
    while lanes:
        for lane in list(lanes):
            try:
                next(lane)
            except StopIteration:
                lanes.remove(lane)


def _idle(n):
    for _ in range(n):
        yield


def _chain(*gens):
    for g in gens:
        yield from g


def _causal_conv(buf_ref, x, w_ref, b_ref):
    kw = w_ref.shape[0]
    rows = x.shape[0]
    buf_ref[SUBLANES:SUBLANES + rows, :] = x
    acc = b_ref[...] + w_ref[kw - 1:kw, :] * x
    for j in range(kw - 1):
        s = kw - 1 - j
        acc = acc + w_ref[j:j + 1, :] * buf_ref[SUBLANES - s:SUBLANES - s + rows, :]
    buf_ref[0:SUBLANES, :] = buf_ref[rows:rows + SUBLANES, :]
    return acc


def _mixer_kernel(pm_ref, ps_ref, lbl_ref, hgn_ref, cw_ref, cb_ref, mln_ref,
                  h1_ref, wg_ref, bg_ref, x_ref, wua_ref, wub_ref, wout_ref, gpost_ref, gpre_ref,
                  x1_ref, h2_ref,
                  st_ref, c_ref, n_ref, m_ref, qkbuf_ref, ya_ref, yb_ref, *, layer, tiles_per_seq, n_tiles):
    s = pl.program_id(0)
    r = jnp.minimum(s, n_tiles - 1)

    @pl.when(s == 0)
    def _():
        ya_ref[...] = jnp.zeros_like(ya_ref)
        yb_ref[...] = jnp.zeros_like(yb_ref)

    @pl.when(r % tiles_per_seq == 0)
    def _():
        st_ref[...] = jnp.zeros_like(st_ref)
        c_ref[...] = jnp.zeros_like(c_ref)
        n_ref[...] = jnp.zeros_like(n_ref)
        m_ref[...] = jnp.zeros_like(m_ref)
        qkbuf_ref[0:SUBLANES, :] = jnp.zeros((SUBLANES, qkbuf_ref.shape[1]), F32)

    d = x_ref.shape[1]
    hg_w = HG_HEADS * HG_HEAD_DIM
    qk_w = ML_HEADS * ML_QK_DIM
    qk_off = 4 * hg_w
    v_off = qk_off + 2 * qk_w
    o_off = v_off + ML_HEADS * ML_V_DIM
    n_pieces = d // MXU_WIDTH
    pw = d // n_pieces
    merged = [None] * n_pieces
    mix = [None] * n_pieces
    ya_new = [None] * HG_HEADS
    yb_new = [None] * ML_HEADS

    def gate(cs):
        return jax.nn.sigmoid(_dot(h1_ref[...], wg_ref[:, cs]) + bg_ref[:, cs])

    def lift_lane():
        for j in range(n_pieces):
            cs = slice(j * pw, (j + 1) * pw)
            gate_a = gate(cs)
            gate_b = gate(slice(d + j * pw, d + (j + 1) * pw))
            yield
            merged[j] = (gate_a * _dot(ya_ref[...], wua_ref[:, cs])
                         + gate_b * _dot(yb_ref[...], wub_ref[:, cs])).astype(BF16)
            yield

    def outproj_lane():
        lhs = jnp.concatenate(merged, axis=1)
        for j in range(n_pieces):
            mix[j] = _dot(lhs, wout_ref[:, j * pw:(j + 1) * pw])
            yield

    def hgrn2_piece(h):
        cols = [slice(j * hg_w + h * HG_HEAD_DIM, j * hg_w + (h + 1) * HG_HEAD_DIM) for j in range(4)]
        hs = slice(h * HG_HEAD_DIM, (h + 1) * HG_HEAD_DIM)
        y = yield from _hgrn2_head(pm_ref[:, cols[0]], pm_ref[:, cols[1]], pm_ref[:, cols[2]],
                                   pm_ref[:, cols[3]], lb[:, hs], hgn_ref[:, hs], st_ref.at[h])
        ya_new[h] = y.astype(ya_ref.dtype)

    def mlstm_piece(h):
        qs = slice(h * ML_QK_DIM, (h + 1) * ML_QK_DIM)
        ks = slice(qk_w + h * ML_QK_DIM, qk_w + (h + 1) * ML_QK_DIM)
        vs = slice(h * ML_V_DIM, (h + 1) * ML_V_DIM)
        y, m_new = yield from _mlstm_head(qk[:, qs], qk[:, ks] * (ML_QK_DIM ** -0.5),
                                          pm_ref[:, v_off + vs.start:v_off + vs.stop],
                                          pm_ref[:, o_off + vs.start:o_off + vs.stop],
                                          gates[:, h:h + 1], gates_b[:, ML_HEADS + h:ML_HEADS + h + 1],
                                          mln_ref[:, vs], c_ref.at[h], n_ref.at[h], m_ref[h, 0:1, 0:1])
        m_ref[h] = jnp.broadcast_to(m_new, m_ref.shape[1:])
        yb_new[h] = y.astype(yb_ref.dtype)

    logits = lbl_ref[...]
    e = jnp.exp(logits - jnp.max(logits, axis=0, keepdims=True))
    sm = e / jnp.sum(e, axis=0, keepdims=True)
    lb = jnp.sum(sm[:layer + 1], axis=0, keepdims=True)
    qk = jax.nn.silu(_causal_conv(qkbuf_ref, pm_ref[:, qk_off:qk_off + 2 * qk_w].astype(F32), cw_ref, cb_ref))
    gates = ps_ref[...]
    gates_b = _chunk_cumsum(jax.nn.log_sigmoid(gates), ML_CHUNK)

    _interleave(lift_lane(),
                _chain(*[hgrn2_piece(h) for h in range(0, HG_HEADS, 2)], mlstm_piece(0)),
                _chain(_idle(2), *[hgrn2_piece(h) for h in range(1, HG_HEADS, 2)], mlstm_piece(1)))
    _interleave(outproj_lane(),
                mlstm_piece(2),
                _chain(_idle(2), mlstm_piece(3)))
    x1 = x_ref[...] + _rms(jnp.concatenate(mix, axis=1), gpost_ref[...])
    x1_ref[...] = x1
    h2_ref[...] = _rms(x1, gpre_ref[...]).astype(h2_ref.dtype)
    ya_ref[...] = jnp.concatenate(ya_new, axis=1)
    yb_ref[...] = jnp.concatenate(yb_new, axis=1)


def _mixer(p_main, p_small, h1, w_gate, b_gate, x2, lb_logits, hg_gain, conv_w, conv_b, ml_gain, w_ua, w_ub, w_out,
           g_post, g_pre, *, seq, layer, tm=SMALL_ROW_TILE):
    m, d = x2.shape
    n_tiles = m // tm
    hg_w = HG_HEADS * HG_HEAD_DIM
    qk_w = ML_HEADS * ML_QK_DIM
    v_w = ML_HEADS * ML_V_DIM
    const = lambda s: (0, 0)
    cur = lambda s: (jnp.minimum(s, n_tiles - 1), 0)
    prev = lambda s: (jnp.maximum(s - 1, 0), 0)
    single = pl.Buffered(1)
    return pl.pallas_call(
        functools.partial(_mixer_kernel, layer=layer, tiles_per_seq=seq // tm, n_tiles=n_tiles),
        grid=(n_tiles + 1,),
        in_specs=[pl.BlockSpec((tm, p_main.shape[1]), cur),
                  pl.BlockSpec((tm, p_small.shape[1]), cur),
                  pl.BlockSpec(lb_logits.shape, const),
                  pl.BlockSpec((1, hg_w), const),
                  pl.BlockSpec(conv_w.shape, const),
                  pl.BlockSpec((1, 2 * qk_w), const),
                  pl.BlockSpec((1, v_w), const),
                  pl.BlockSpec((tm, d), prev),
                  pl.BlockSpec((d, 2 * d), const, pipeline_mode=single),
                  pl.BlockSpec((1, 2 * d), const),
                  pl.BlockSpec((tm, d), prev),
                  pl.BlockSpec((hg_w, d), const, pipeline_mode=single),
                  pl.BlockSpec((v_w, d), const, pipeline_mode=single),
                  pl.BlockSpec((d, d), const, pipeline_mode=single),
                  pl.BlockSpec((1, d), const),
                  pl.BlockSpec((1, d), const)],
        out_specs=[pl.BlockSpec((tm, d), prev),
                   pl.BlockSpec((tm, d), prev)],
        out_shape=[jax.ShapeDtypeStruct((m, d), F32),
                   jax.ShapeDtypeStruct((m, d), BF16)],
        scratch_shapes=[pltpu.VMEM((HG_HEADS, HG_HEAD_DIM, HG_HEAD_DIM), F32),
                        pltpu.VMEM((ML_HEADS, ML_QK_DIM, ML_V_DIM), F32),
                        pltpu.VMEM((ML_HEADS, SUBLANES, ML_QK_DIM), F32),
                        pltpu.VMEM((ML_HEADS, SUBLANES, LANES), F32),
                        pltpu.VMEM((tm + SUBLANES, 2 * qk_w), F32),
                        pltpu.VMEM((tm, hg_w), BF16),
                        pltpu.VMEM((tm, v_w), BF16)],
        compiler_params=_params("arbitrary", vmem=MIXER_VMEM_LIMIT_BYTES),
        name="mixer",
    )(p_main, p_small, lb_logits, hg_gain.reshape(1, hg_w), conv_w, conv_b.reshape(1, 2 * qk_w),
      ml_gain.reshape(1, v_w), h1, w_gate, b_gate.reshape(1, 2 * d), x2, w_ua, w_ub, w_out,
      g_post.reshape(1, d), g_pre.reshape(1, d))


def _ffn_up_kernel(h_ref, wg_ref, wu_ref, cw_ref, cb_ref, a_ref, gbuf_ref, wgb_ref, wub_ref, *, tm, tiles_per_seq):
    kw = cw_ref.shape[0]

    @pl.when(pl.program_id(1) == 0)
    def _():
        wgb_ref[...] = wg_ref[...].astype(BF16)
        wub_ref[...] = wu_ref[...].astype(BF16)

    @pl.when(pl.program_id(1) % tiles_per_seq == 0)
    def _():
        gbuf_ref[0:SUBLANES, :] = jnp.zeros((SUBLANES, gbuf_ref.shape[1]), F32)

    gbuf_ref[SUBLANES:SUBLANES + tm, :] = _dot(h_ref[...], wgb_ref[...])
    acc = cb_ref[...] + cw_ref[kw - 1:kw, :] * gbuf_ref[SUBLANES:SUBLANES + tm, :]
    for j in range(kw - 1):
        s = kw - 1 - j
        acc = acc + cw_ref[j:j + 1, :] * gbuf_ref[SUBLANES - s:SUBLANES - s + tm, :]
    gbuf_ref[0:SUBLANES, :] = gbuf_ref[tm:tm + SUBLANES, :]
    a_ref[...] = (jax.nn.gelu(acc, approximate=True) * _dot(h_ref[...], wub_ref[...])).astype(a_ref.dtype)


def _ffn_up(h2, w_gate, w_up, conv_w, conv_b, *, layer, seq, tm=ROW_TILE, tf=FFN_COL_TILE):
    m, d = h2.shape
    f = w_gate.shape[2]
    kw = conv_w.shape[0]
    wspec = pl.BlockSpec((None, d, tf), lambda j, i: (layer, 0, j))
    return pl.pallas_call(
        functools.partial(_ffn_up_kernel, tm=tm, tiles_per_seq=seq // tm),
        grid=(f // tf, m // tm),
        in_specs=[pl.BlockSpec((tm, d), lambda j, i: (i, 0)),
                  wspec, wspec,
                  pl.BlockSpec((kw, tf), lambda j, i: (0, j)),
                  pl.BlockSpec((1, tf), lambda j, i: (0, j))],
        out_specs=pl.BlockSpec((tm, tf), lambda j, i: (i, j)),
        out_shape=jax.ShapeDtypeStruct((m, f), BF16),
        scratch_shapes=[pltpu.VMEM((tm + SUBLANES, tf), F32),
                        pltpu.VMEM((d, tf), BF16),
                        pltpu.VMEM((d, tf), BF16)],
        compiler_params=_params("arbitrary", "arbitrary"),
        name="ffn_up",
    )(h2, w_gate, w_up, conv_w, conv_b.reshape(1, f))


def _ffn_down_kernel(a_ref, wd_ref, x1_ref, g_ref, o_ref):
    o_ref[...] = x1_ref[...] + _rms(_dot(a_ref[...], wd_ref[...]), g_ref[...])


def _ffn_down(act, w_down, x1, gain, tm=2 * SMALL_ROW_TILE):
    m, f = act.shape
    d = w_down.shape[1]
    return pl.pallas_call(
        _ffn_down_kernel,
        grid=(m // tm,),
        in_specs=[pl.BlockSpec((tm, f), lambda i: (i, 0)),
                  pl.BlockSpec((f, d), lambda i: (0, 0), pipeline_mode=pl.Buffered(1)),
                  pl.BlockSpec((tm, d), lambda i: (i, 0)),
                  pl.BlockSpec((1, d), lambda i: (0, 0))],
        out_specs=pl.BlockSpec((tm, d), lambda i: (i, 0)),
        out_shape=jax.ShapeDtypeStruct((m, d), F32),
        compiler_params=_params("parallel"),
        name="ffn_down",
    )(act, w_down, x1, gain.reshape(1, d))


def kernel(x, norm_mix_pre, norm_mix_post, norm_ffn_pre, norm_ffn_post, w_in, b_in, hg_lb_logits, hg_norm,
           ml_conv_w, ml_conv_b, ml_norm, w_up_a, w_up_b, w_out, ffn_w_gate, ffn_w_up, ffn_conv_w, ffn_conv_b,
           ffn_w_down):
    batch, seq, d = x.shape
    depth = w_in.shape[0]
    hg_w = HG_HEADS * HG_HEAD_DIM
    qk_w = ML_HEADS * ML_QK_DIM
    v_w = ML_HEADS * ML_V_DIM
    main_w = 4 * hg_w + 2 * qk_w + 2 * v_w
    gate_off = main_w + 2 * ML_HEADS
    assert w_in.shape[2] == gate_off + 2 * d and w_up_a.shape[1] == hg_w and w_up_b.shape[1] == v_w
    assert seq % ROW_TILE == 0 and main_w % COL_TILE == 0 and ffn_w_gate.shape[2] % FFN_COL_TILE == 0
    w_in_t = jnp.swapaxes(w_in, 1, 2)

    x2 = x.reshape(batch * seq, d)
    for l in range(depth):
        h1, p_small = _rmsnorm_cast(x2, norm_mix_pre[l], w_in_t, b_in[l, main_w:gate_off], layer=l, row0=main_w)
        p_main, w_down, w_g, w_ua, w_ub, w_o = _mm_bias_t(
            h1, w_in_t, b_in[l, :main_w], layer=l, row0=0, n=main_w, tm=ROW_TILE, tn=COL_TILE, out_dtype=BF16,
            side_casts=((ffn_w_down, 0, ffn_w_down.shape[1], False), (w_in_t, gate_off, 2 * d, True),
                        (w_up_a, 0, hg_w, False), (w_up_b, 0, v_w, False), (w_out, 0, d, False)),
            name="in_proj_main")
        x2, h2 = _mixer(p_main, p_small, h1, w_g, b_in[l, gate_off:], x2, hg_lb_logits, hg_norm[l], ml_conv_w[l],
                        ml_conv_b[l], ml_norm[l], w_ua, w_ub, w_o, norm_mix_post[l], norm_ffn_pre[l],
                        seq=seq, layer=l)

        act = _ffn_up(h2, ffn_w_gate, ffn_w_up, ffn_conv_w[l], ffn_conv_b[l], layer=l, seq=seq)
        x2 = _ffn_down(act, w_down, x2, norm_ffn_post[l])
    return x2.reshape(batch, seq, d)
```

```python
import functools

import jax
import jax.numpy as jnp
from jax import lax
from jax.experimental import pallas as pl
from jax.experimental.pallas import tpu as pltpu

HG_HEADS = 8
HG_HEAD_DIM = 128
ML_HEADS = 4
ML_QK_DIM = 128
ML_V_DIM = 256
CHUNK = 64
ML_CHUNK = 128
EPS = 1e-6

F32 = jnp.float32
BF16 = jnp.bfloat16

SUBLANES = 8
LANES = 128
MXU_WIDTH = 256
MIB = 1024 * 1024
ROW_TILE = 1024
COL_TILE = 1024
FFN_COL_TILE = 512
SMALL_ROW_TILE = 256
VMEM_LIMIT_BYTES = 56 * MIB
MIXER_VMEM_LIMIT_BYTES = 63 * MIB + MIB // 2


def _params(*sem, vmem=VMEM_LIMIT_BYTES):
    return pltpu.CompilerParams(dimension_semantics=sem, vmem_limit_bytes=vmem)


def _dot(a, b):
    return jnp.dot(a, b, preferred_element_type=F32)


def _dot_nt(a, b):
    return lax.dot_general(a, b, (((1,), (1,)), ((), ())), preferred_element_type=F32)


def _dot_tn(a, b):
    return lax.dot_general(a, b, (((0,), (0,)), ((), ())), preferred_element_type=F32)


def _rms(u, gain):
    return u * lax.rsqrt(jnp.mean(u * u, axis=-1, keepdims=True) + EPS) * gain


def _rmsnorm_kernel(x_ref, g_ref, ws_ref, bs_ref, o_ref, ps_ref):
    h = _rms(x_ref[...], g_ref[...]).astype(o_ref.dtype)
    o_ref[...] = h
    ps_ref[...] = _dot_nt(h, ws_ref[...].astype(BF16)) + bs_ref[...]


def _rmsnorm_cast(x2, gain, wt, b_small, *, layer, row0, tm=ROW_TILE):
    m, d = x2.shape
    n = b_small.shape[0]
    row0 += layer * wt.shape[1]
    return pl.pallas_call(
        _rmsnorm_kernel,
        grid=(m // tm,),
        in_specs=[pl.BlockSpec((tm, d), lambda i: (i, 0)),
                  pl.BlockSpec((1, d), lambda i: (0, 0)),
                  pl.BlockSpec((pl.Element(n), pl.Element(d)), lambda i: (row0, 0)),
                  pl.BlockSpec((1, n), lambda i: (0, 0))],
        out_specs=[pl.BlockSpec((tm, d), lambda i: (i, 0)),
                   pl.BlockSpec((tm, n), lambda i: (i, 0))],
        out_shape=[jax.ShapeDtypeStruct((m, d), BF16),
                   jax.ShapeDtypeStruct((m, n), F32)],
        compiler_params=_params("parallel"),
        name="rmsnorm_cast",
    )(x2, gain.reshape(1, d), wt.reshape(-1, d), b_small.reshape(1, n))


def _mm_bias_kernel(h_ref, w_ref, b_ref, *rest, sigmoid, side_transpose):
    n_side = len(side_transpose)
    side_in, o_ref, side_out, wb_ref = rest[:n_side], rest[n_side], rest[n_side + 1:2 * n_side + 1], rest[-1]

    @pl.when(pl.program_id(1) == 0)
    def _():
        wb_ref[...] = w_ref[...].astype(BF16)

    for src, dst, transpose in zip(side_in, side_out, side_transpose):
        blk = src[...]
        dst[...] = (blk.T if transpose else blk).astype(dst.dtype)

    acc = _dot_nt(h_ref[...], wb_ref[...]) + b_ref[...]
    if sigmoid:
        acc = jax.nn.sigmoid(acc)
    o_ref[...] = acc.astype(o_ref.dtype)


def _mm_bias_t(h, wt, b, *, layer, row0, n, tm, tn, out_dtype, sigmoid=False, side_casts=(), name):
    m, k = h.shape
    row0 += layer * wt.shape[1]
    wt = wt.reshape(-1, k)
    n_i = m // tm
    steps = (n // tn) * n_i
    side_specs, side_out_specs, side_shapes, side_args = [], [], [], []
    for a, start, rows, transpose in side_casts:
        cols = a.shape[2]
        start += layer * a.shape[1]
        rb = next(r for r in range(LANES, rows + 1, LANES) if rows % r == 0 and rows // r <= steps)
        nb = rows // rb
        blk = lambda j, i, nb=nb: jnp.minimum(j * n_i + i, nb - 1)
        side_specs.append(pl.BlockSpec(
            (pl.Element(rb), pl.Element(cols)),
            lambda j, i, blk=blk, start=start, rb=rb: (pl.multiple_of(start + blk(j, i) * rb, SUBLANES), 0)))
        if transpose:
            side_out_specs.append(pl.BlockSpec((cols, rb), lambda j, i, blk=blk: (0, blk(j, i))))
            side_shapes.append(jax.ShapeDtypeStruct((cols, rows), BF16))
        else:
            side_out_specs.append(pl.BlockSpec((rb, cols), lambda j, i, blk=blk: (blk(j, i), 0)))
            side_shapes.append(jax.ShapeDtypeStruct((rows, cols), BF16))
        side_args.append(a.reshape(-1, cols))
    outs = pl.pallas_call(
        functools.partial(_mm_bias_kernel, sigmoid=sigmoid, side_transpose=tuple(c[3] for c in side_casts)),
        grid=(n // tn, n_i),
        in_specs=[pl.BlockSpec((tm, k), lambda j, i: (i, 0)),
                  pl.BlockSpec((pl.Element(tn), pl.Element(k)),
                               lambda j, i: (pl.multiple_of(row0 + j * tn, SUBLANES), 0)),
                  pl.BlockSpec((1, tn), lambda j, i: (0, j))] + side_specs,
        out_specs=[pl.BlockSpec((tm, tn), lambda j, i: (i, j))] + side_out_specs,
        out_shape=[jax.ShapeDtypeStruct((m, n), out_dtype)] + side_shapes,
        scratch_shapes=[pltpu.VMEM((tn, k), BF16)],
        compiler_params=_params("arbitrary", "arbitrary"),
        name=name,
    )(h, wt, b.reshape(1, n), *side_args)
    return outs if side_casts else outs[0]


def _tril(n):
    r = lax.broadcasted_iota(jnp.int32, (n, n), 0)
    c = lax.broadcasted_iota(jnp.int32, (n, n), 1)
    return r >= c


def _chunk_cumsum(x, chunk):
    pos = lax.broadcasted_iota(jnp.int32, x.shape, 0) & (chunk - 1)
    s = 1
    while s < chunk:
        x = x + jnp.where(pos >= s, pltpu.roll(x, s, axis=0), 0.0)
        s *= 2
    return x


def _chunk_slices(rows, chunk):
    return [slice(c * chunk, (c + 1) * chunk) for c in range(rows // chunk)]


def _hgrn2_head(q_pre, f_pre, v_in, g_pre, lb, gain, s_ref):
    q = jax.nn.silu(q_pre.astype(F32)) * (HG_HEAD_DIM ** -0.5)
    f = lb + (1.0 - lb) * jax.nn.sigmoid(f_pre.astype(F32))
    k = 1.0 - f
    b = _chunk_cumsum(jnp.log(f), CHUNK)
    v = v_in.astype(BF16)
    causal = _tril(CHUNK)
    q_dec = (q * jnp.exp(b)).astype(BF16)
    k_inv = k * jnp.exp(-b)
    k_inv_b = k_inv.astype(BF16)

    chunks = _chunk_slices(q_pre.shape[0], CHUNK)
    decay = [jnp.exp(b[sl.stop - 1:sl.stop, :]) for sl in chunks]
    k_end = [(k_inv[sl] * d).astype(BF16) for sl, d in zip(chunks, decay)]
    yield
    attn = [jnp.where(causal, _dot_nt(q_dec[sl], k_inv_b[sl]), 0.0).astype(BF16) for sl in chunks]
    kv = [_dot_tn(ke, v[sl]) for sl, ke in zip(chunks, k_end)]
    decay_col = [jnp.broadcast_to(d, (SUBLANES, d.shape[1])).T[:, 0:1] for d in decay]
    yield
    s = s_ref[...]
    outs = []
    for c, sl in enumerate(chunks):
        lhs = jnp.concatenate([q_dec[sl], attn[c]], axis=1)
        rhs = jnp.concatenate([s.astype(BF16), v[sl]], axis=0)
        outs.append(_dot(lhs, rhs))
        s = decay_col[c] * s + kv[c]
    s_ref[...] = s
    yield
    o = jnp.concatenate(outs, axis=0)
    return _rms(o, gain) * jax.nn.silu(g_pre.astype(F32))


def _mlstm_head(q, k, v, o_pre, ig_col, b_col, gain, c_ref, n_ref, m_prev):
    rows = q.shape[0]
    qb = q.astype(BF16)
    kb = k.astype(BF16)
    vb = v.astype(BF16)
    wide = (rows, ML_QK_DIM)
    b = jnp.broadcast_to(b_col, wide)
    igb = jnp.broadcast_to(ig_col - b_col, wide)
    causal = _tril(ML_CHUNK)
    chunks = _chunk_slices(rows, ML_CHUNK)
    n_chunks = len(chunks)

    b_last = [b[sl.stop - 1:sl.stop, 0:1] for sl in chunks]
    log_d = [jnp.where(causal, b[sl, :ML_CHUNK] + igb[sl].T[:ML_CHUNK, :], -jnp.inf) for sl in chunks]
    m_intra = [jnp.max(ld, axis=1, keepdims=True) for ld in log_d]
    logw = [bl + igb[sl, 0:1] for bl, sl in zip(b_last, chunks)]
    logw_max = [jnp.max(lw, axis=0, keepdims=True) for lw in logw]
    qk = [_dot_nt(qb[sl], kb[sl]) for sl in chunks]
    yield

    m_in, m_out = [], []
    for c in range(n_chunks):
        m_in.append(m_prev)
        m_prev = jnp.maximum(b_last[c] + m_prev, logw_max[c])
        m_out.append(m_prev)

    m_inter = [b[sl, 0:1] + mi for sl, mi in zip(chunks, m_in)]
    m_tot = [jnp.maximum(a, bb) for a, bb in zip(m_inter, m_intra)]
    scores = [qk[c] * jnp.exp(log_d[c] - m_tot[c]) for c in range(n_chunks)]
    den_intra = [jnp.sum(sc, axis=1, keepdims=True) for sc in scores]
    k_w = [k[sl] * jnp.exp(logw[c] - m_out[c]) for c, sl in enumerate(chunks)]
    k_sum = [jnp.sum(kw, axis=0, keepdims=True) for kw in k_w]
    inter_scale = [jnp.exp(m_inter[c] - m_tot[c]) for c in range(n_chunks)]
    q_scaled = [(q[sl] * inter_scale[c]).astype(BF16) for c, sl in enumerate(chunks)]
    yield
    kv = [_dot_tn(k_w[c].astype(BF16), vb[sl]) for c, sl in enumerate(chunks)]
    yield
    cm = c_ref[...]
    n = n_ref[0:1, :]
    outs = []
    for c, sl in enumerate(chunks):
        lhs = jnp.concatenate([q_scaled[c], scores[c].astype(BF16)], axis=1)
        rhs = jnp.concatenate([cm.astype(BF16), vb[sl]], axis=0)
        num = _dot(lhs, rhs)
        den = den_intra[c] + inter_scale[c] * jnp.sum(q[sl] * n, axis=1, keepdims=True)
        outs.append(num / jnp.maximum(jnp.abs(den), jnp.exp(-m_tot[c])))
        decay = jnp.exp(b_last[c] + m_in[c] - m_out[c])
        cm = decay * cm + kv[c]
        n = decay * n + k_sum[c]
    c_ref[...] = cm
    n_ref[...] = jnp.broadcast_to(n, n_ref.shape)
    yield
    hh = jnp.concatenate(outs, axis=0)
    return jax.nn.sigmoid(o_pre.astype(F32)) * _rms(hh, gain), m_prev


def _interleave(*lanes):
    lanes = [iter(lane) for lane in lanes]
    while lanes:
        for lane in list(lanes):
            try:
                next(lane)
            except StopIteration:
                lanes.remove(lane)


def _idle(n):
    for _ in range(n):
        yield


def _chain(*gens):
    for g in gens:
        yield from g


def _causal_conv(carry_ref, x, w_ref, b_ref):
    kw = w_ref.shape[0]
    rows = x.shape[0]
    prev = carry_ref[...]
    head_row = lax.broadcasted_iota(jnp.int32, prev.shape, 0)
    acc = b_ref[...] + w_ref[kw - 1:kw, :] * x
    for j in range(kw - 1):
        s = kw - 1 - j
        shifted = pltpu.roll(x, s, axis=0)
        head = jnp.where(head_row < s, pltpu.roll(prev, s, axis=0), shifted[:SUBLANES])
        acc = acc + w_ref[j:j + 1, :] * jnp.concatenate([head, shifted[SUBLANES:]], axis=0)
    carry_ref[...] = x[rows - SUBLANES:, :]
    return acc


def _mixer_kernel(pm_ref, ps_ref, lbl_ref, hgn_ref, cw_ref, cb_ref, mln_ref,
                  h1_ref, wg_ref, bg_ref, x_ref, wua_ref, wub_ref, wout_ref, gpost_ref, gpre_ref,
                  x1_ref, h2_ref,
                  st_ref, c_ref, n_ref, m_ref, qkbuf_ref, ya_ref, yb_ref, *, layer, tiles_per_seq, n_tiles):
    s = pl.program_id(0)
    r = jnp.minimum(s, n_tiles - 1)

    @pl.when(s == 0)
    def _():
        ya_ref[...] = jnp.zeros_like(ya_ref)
        yb_ref[...] = jnp.zeros_like(yb_ref)

    @pl.when(r % tiles_per_seq == 0)
    def _():
        st_ref[...] = jnp.zeros_like(st_ref)
        c_ref[...] = jnp.zeros_like(c_ref)
        n_ref[...] = jnp.zeros_like(n_ref)
        m_ref[...] = jnp.zeros_like(m_ref)
        qkbuf_ref[...] = jnp.zeros_like(qkbuf_ref)

    d = x_ref.shape[1]
    hg_w = HG_HEADS * HG_HEAD_DIM
    qk_w = ML_HEADS * ML_QK_DIM
    qk_off = 4 * hg_w
    v_off = qk_off + 2 * qk_w
    o_off = v_off + ML_HEADS * ML_V_DIM
    n_pieces = d // MXU_WIDTH
    pw = d // n_pieces
    merged = [None] * n_pieces
    mix = [None] * n_pieces
    ya_new = [None] * HG_HEADS
    yb_new = [None] * ML_HEADS

    def gate(cs):
        return jax.nn.sigmoid(_dot(h1_ref[...], wg_ref[:, cs]) + bg_ref[:, cs])

    def lift_lane():
        for j in range(n_pieces):
            cs = slice(j * pw, (j + 1) * pw)
            part_a = gate(cs) * _dot(ya_ref[...], wua_ref[:, cs])
            yield
            part_b = gate(slice(d + j * pw, d + (j + 1) * pw)) * _dot(yb_ref[...], wub_ref[:, cs])
            merged[j] = (part_a + part_b).astype(BF16)
            yield

    def outproj_lane():
        lhs = jnp.concatenate(merged, axis=1)
        for j in range(n_pieces):
            mix[j] = _dot(lhs, wout_ref[:, j * pw:(j + 1) * pw])
            yield

    def hgrn2_piece(h):
        cols = [slice(j * hg_w + h * HG_HEAD_DIM, j * hg_w + (h + 1) * HG_HEAD_DIM) for j in range(4)]
        hs = slice(h * HG_HEAD_DIM, (h + 1) * HG_HEAD_DIM)
        y = yield from _hgrn2_head(pm_ref[:, cols[0]], pm_ref[:, cols[1]], pm_ref[:, cols[2]],
                                   pm_ref[:, cols[3]], lb[:, hs], hgn_ref[:, hs], st_ref.at[h])
        ya_new[h] = y.astype(ya_ref.dtype)

    def mlstm_piece(h):
        qs = slice(h * ML_QK_DIM, (h + 1) * ML_QK_DIM)
        ks = slice(qk_w + h * ML_QK_DIM, qk_w + (h + 1) * ML_QK_DIM)
        vs = slice(h * ML_V_DIM, (h + 1) * ML_V_DIM)
        y, m_new = yield from _mlstm_head(qk[:, qs], qk[:, ks] * (ML_QK_DIM ** -0.5),
                                          pm_ref[:, v_off + vs.start:v_off + vs.stop],
                                          pm_ref[:, o_off + vs.start:o_off + vs.stop],
                                          gates[:, h:h + 1], gates_b[:, ML_HEADS + h:ML_HEADS + h + 1],
                                          mln_ref[:, vs], c_ref.at[h], n_ref.at[h], m_ref[h, 0:1, 0:1])
        m_ref[h] = jnp.broadcast_to(m_new, m_ref.shape[1:])
        yb_new[h] = y.astype(yb_ref.dtype)

    logits = lbl_ref[...]
    e = jnp.exp(logits - jnp.max(logits, axis=0, keepdims=True))
    sm = e / jnp.sum(e, axis=0, keepdims=True)
    lb = jnp.sum(sm[:layer + 1], axis=0, keepdims=True)
    qk = jax.nn.silu(_causal_conv(qkbuf_ref, pm_ref[:, qk_off:qk_off + 2 * qk_w].astype(F32), cw_ref, cb_ref))
    gates = ps_ref[...]
    gates_b = _chunk_cumsum(jax.nn.log_sigmoid(gates), ML_CHUNK)

    _interleave(lift_lane(),
                _chain(*[hgrn2_piece(h) for h in range(0, HG_HEADS, 2)], mlstm_piece(0)),
                _chain(_idle(2), *[hgrn2_piece(h) for h in range(1, HG_HEADS, 2)], mlstm_piece(1)))
    _interleave(outproj_lane(),
                mlstm_piece(2),
                _chain(_idle(2), mlstm_piece(3)))
    x1 = x_ref[...] + _rms(jnp.concatenate(mix, axis=1), gpost_ref[...])
    x1_ref[...] = x1
    h2_ref[...] = _rms(x1, gpre_ref[...]).astype(h2_ref.dtype)
    ya_ref[...] = jnp.concatenate(ya_new, axis=1)
    yb_ref[...] = jnp.concatenate(yb_new, axis=1)


def _mixer(p_main, p_small, h1, w_gate, b_gate, x2, lb_logits, hg_gain, conv_w, conv_b, ml_gain, w_ua, w_ub, w_out,
           g_post, g_pre, *, seq, layer, tm=SMALL_ROW_TILE):
    m, d = x2.shape
    n_tiles = m // tm
    hg_w = HG_HEADS * HG_HEAD_DIM
    qk_w = ML_HEADS * ML_QK_DIM
    v_w = ML_HEADS * ML_V_DIM
    const = lambda s: (0, 0)
    cur = lambda s: (jnp.minimum(s, n_tiles - 1), 0)
    prev = lambda s: (jnp.maximum(s - 1, 0), 0)
    single = pl.Buffered(1)
    return pl.pallas_call(
        functools.partial(_mixer_kernel, layer=layer, tiles_per_seq=seq // tm, n_tiles=n_tiles),
        grid=(n_tiles + 1,),
        in_specs=[pl.BlockSpec((tm, p_main.shape[1]), cur),
                  pl.BlockSpec((tm, p_small.shape[1]), cur),
                  pl.BlockSpec(lb_logits.shape, const),
                  pl.BlockSpec((1, hg_w), const),
                  pl.BlockSpec(conv_w.shape, const),
                  pl.BlockSpec((1, 2 * qk_w), const),
                  pl.BlockSpec((1, v_w), const),
                  pl.BlockSpec((tm, d), prev),
                  pl.BlockSpec((d, 2 * d), const, pipeline_mode=single),
                  pl.BlockSpec((1, 2 * d), const),
                  pl.BlockSpec((tm, d), prev),
                  pl.BlockSpec((hg_w, d), const, pipeline_mode=single),
                  pl.BlockSpec((v_w, d), const, pipeline_mode=single),
                  pl.BlockSpec((d, d), const, pipeline_mode=single),
                  pl.BlockSpec((1, d), const),
                  pl.BlockSpec((1, d), const)],
        out_specs=[pl.BlockSpec((tm, d), prev),
                   pl.BlockSpec((tm, d), prev)],
        out_shape=[jax.ShapeDtypeStruct((m, d), F32),
                   jax.ShapeDtypeStruct((m, d), BF16)],
        scratch_shapes=[pltpu.VMEM((HG_HEADS, HG_HEAD_DIM, HG_HEAD_DIM), F32),
                        pltpu.VMEM((ML_HEADS, ML_QK_DIM, ML_V_DIM), F32),
                        pltpu.VMEM((ML_HEADS, SUBLANES, ML_QK_DIM), F32),
                        pltpu.VMEM((ML_HEADS, SUBLANES, LANES), F32),
                        pltpu.VMEM((SUBLANES, 2 * qk_w), F32),
                        pltpu.VMEM((tm, hg_w), BF16),
                        pltpu.VMEM((tm, v_w), BF16)],
        compiler_params=_params("arbitrary", vmem=MIXER_VMEM_LIMIT_BYTES),
        name="mixer",
    )(p_main, p_small, lb_logits, hg_gain.reshape(1, hg_w), conv_w, conv_b.reshape(1, 2 * qk_w),
      ml_gain.reshape(1, v_w), h1, w_gate, b_gate.reshape(1, 2 * d), x2, w_ua, w_ub, w_out,
      g_post.reshape(1, d), g_pre.reshape(1, d))


def _ffn_up_kernel(h_ref, wg_ref, wu_ref, cw_ref, cb_ref, a_ref, gbuf_ref, wgb_ref, wub_ref, *, tm, tiles_per_seq):
    kw = cw_ref.shape[0]

    @pl.when(pl.program_id(1) == 0)
    def _():
        wgb_ref[...] = wg_ref[...].astype(BF16)
        wub_ref[...] = wu_ref[...].astype(BF16)

    @pl.when(pl.program_id(1) % tiles_per_seq == 0)
    def _():
        gbuf_ref[0:SUBLANES, :] = jnp.zeros((SUBLANES, gbuf_ref.shape[1]), F32)

    gbuf_ref[SUBLANES:SUBLANES + tm, :] = _dot(h_ref[...], wgb_ref[...])
    acc = cb_ref[...] + cw_ref[kw - 1:kw, :] * gbuf_ref[SUBLANES:SUBLANES + tm, :]
    for j in range(kw - 1):
        s = kw - 1 - j
        acc = acc + cw_ref[j:j + 1, :] * gbuf_ref[SUBLANES - s:SUBLANES - s + tm, :]
    gbuf_ref[0:SUBLANES, :] = gbuf_ref[tm:tm + SUBLANES, :]
    a_ref[...] = (jax.nn.gelu(acc, approximate=True) * _dot(h_ref[...], wub_ref[...])).astype(a_ref.dtype)


def _ffn_up(h2, w_gate, w_up, conv_w, conv_b, *, layer, seq, tm=ROW_TILE, tf=FFN_COL_TILE):
    m, d = h2.shape
    f = w_gate.shape[2]
    kw = conv_w.shape[0]
    wspec = pl.BlockSpec((None, d, tf), lambda j, i: (layer, 0, j))
    return pl.pallas_call(
        functools.partial(_ffn_up_kernel, tm=tm, tiles_per_seq=seq // tm),
        grid=(f // tf, m // tm),
        in_specs=[pl.BlockSpec((tm, d), lambda j, i: (i, 0)),
                  wspec, wspec,
                  pl.BlockSpec((kw, tf), lambda j, i: (0, j)),
                  pl.BlockSpec((1, tf), lambda j, i: (0, j))],
        out_specs=pl.BlockSpec((tm, tf), lambda j, i: (i, j)),
        out_shape=jax.ShapeDtypeStruct((m, f), BF16),
        scratch_shapes=[pltpu.VMEM((tm + SUBLANES, tf), F32),
                        pltpu.VMEM((d, tf), BF16),
                        pltpu.VMEM((d, tf), BF16)],
        compiler_params=_params("arbitrary", "arbitrary"),
        name="ffn_up",
    )(h2, w_gate, w_up, conv_w, conv_b.reshape(1, f))


def _ffn_down_kernel(a_ref, wd_ref, x1_ref, g_ref, o_ref):
    o_ref[...] = x1_ref[...] + _rms(_dot(a_ref[...], wd_ref[...]), g_ref[...])


def _ffn_down(act, w_down, x1, gain, tm=SMALL_ROW_TILE):
    m, f = act.shape
    d = w_down.shape[1]
    return pl.pallas_call(
        _ffn_down_kernel,
        grid=(m // tm,),
        in_specs=[pl.BlockSpec((tm, f), lambda i: (i, 0)),
                  pl.BlockSpec((f, d), lambda i: (0, 0), pipeline_mode=pl.Buffered(1)),
                  pl.BlockSpec((tm, d), lambda i: (i, 0)),
                  pl.BlockSpec((1, d), lambda i: (0, 0))],
        out_specs=pl.BlockSpec((tm, d), lambda i: (i, 0)),
        out_shape=jax.ShapeDtypeStruct((m, d), F32),
        compiler_params=_params("parallel"),
        name="ffn_down",
    )(act, w_down, x1, gain.reshape(1, d))


def kernel(x, norm_mix_pre, norm_mix_post, norm_ffn_pre, norm_ffn_post, w_in, b_in, hg_lb_logits, hg_norm,
           ml_conv_w, ml_conv_b, ml_norm, w_up_a, w_up_b, w_out, ffn_w_gate, ffn_w_up, ffn_conv_w, ffn_conv_b,
           ffn_w_down):
    batch, seq, d = x.shape
    depth = w_in.shape[0]
    hg_w = HG_HEADS * HG_HEAD_DIM
    qk_w = ML_HEADS * ML_QK_DIM
    v_w = ML_HEADS * ML_V_DIM
    main_w = 4 * hg_w + 2 * qk_w + 2 * v_w
    gate_off = main_w + 2 * ML_HEADS
    assert w_in.shape[2] == gate_off + 2 * d and w_up_a.shape[1] == hg_w and w_up_b.shape[1] == v_w
    assert seq % ROW_TILE == 0 and main_w % COL_TILE == 0 and ffn_w_gate.shape[2] % FFN_COL_TILE == 0
    w_in_t = jnp.swapaxes(w_in, 1, 2)

    x2 = x.reshape(batch * seq, d)
    for l in range(depth):
        h1, p_small = _rmsnorm_cast(x2, norm_mix_pre[l], w_in_t, b_in[l, main_w:gate_off], layer=l, row0=main_w)
        p_main, w_down, w_g, w_ua, w_ub, w_o = _mm_bias_t(
            h1, w_in_t, b_in[l, :main_w], layer=l, row0=0, n=main_w, tm=ROW_TILE, tn=COL_TILE, out_dtype=BF16,
            side_casts=((ffn_w_down, 0, ffn_w_down.shape[1], False), (w_in_t, gate_off, 2 * d, True),
                        (w_up_a, 0, hg_w, False), (w_up_b, 0, v_w, False), (w_out, 0, d, False)),
            name="in_proj_main")
        x2, h2 = _mixer(p_main, p_small, h1, w_g, b_in[l, gate_off:], x2, hg_lb_logits, hg_norm[l], ml_conv_w[l],
                        ml_conv_b[l], ml_norm[l], w_ua, w_ub, w_o, norm_mix_post[l], norm_ffn_pre[l],
                        seq=seq, layer=l)

        act = _ffn_up(h2, ffn_w_gate, ffn_w_up, ffn_conv_w[l], ffn_conv_b[l], layer=l, seq=seq)
        x2 = _ffn_down(act, w_down, x2, norm_ffn_post[l])
    return x2.reshape(batch, seq, d)
```
